```python
import jax, jax.numpy as jnp
from jax import lax
import numpy as np

D_MODEL = 4096
BATCH = 2
SEQ = 8192
DEPTH = 2

F32 = jnp.float32

MEM_LEN = 256
MEM_HEADS = 4
MEM_DH = 128
MEM_WIDTH = MEM_HEADS * MEM_DH

GLA_HEADS = 4
GLA_DK = D_MODEL // 16
GLA_DV = D_MODEL // 8
GLA_LOWRANK = 16
GLA_TAU = 16.0
GLA_KWIDTH = GLA_HEADS * GLA_DK
GLA_WIDTH = GLA_HEADS * GLA_DV

HGRN_EXPAND = 128
HGRN_WIDTH = D_MODEL // 2
HGRN_HEADS = HGRN_WIDTH // HGRN_EXPAND
HGRN_DV = HGRN_WIDTH // HGRN_HEADS
HGRN_FWIDTH = HGRN_HEADS * HGRN_EXPAND

MLSTM_WIDTH = 2 * D_MODEL
MLSTM_HEADS = 4
MLSTM_DH = MLSTM_WIDTH // MLSTM_HEADS
MLSTM_CONV = 4
MLSTM_QKV_BLOCK = 4

LIN_CHUNK = 64
MLSTM_CHUNK = 128

ALPHA = (2 * DEPTH) ** 0.25
BETA = (8 * DEPTH) ** -0.25

N_EVEN = (DEPTH + 1) // 2
N_ODD = DEPTH // 2

EVEN_SPLITS = (GLA_KWIDTH, GLA_KWIDTH, GLA_WIDTH, GLA_WIDTH, GLA_LOWRANK,
               HGRN_FWIDTH, HGRN_FWIDTH, HGRN_WIDTH, HGRN_WIDTH,
               MEM_WIDTH, MEM_WIDTH)
ODD_SPLITS = (MLSTM_WIDTH, MLSTM_WIDTH, MEM_WIDTH, MEM_WIDTH)
EVEN_IN = sum(EVEN_SPLITS)
ODD_IN = sum(ODD_SPLITS)
EVEN_OUT = GLA_WIDTH + HGRN_WIDTH + MEM_WIDTH
ODD_OUT = MLSTM_WIDTH + MEM_WIDTH

kernel_name = "hybrid_gla_hgrn2_mlstm_deepnorm"


def split_cols(h, sizes):
    return jnp.split(h, np.cumsum(sizes)[:-1].tolist(), axis=-1)


def to_heads(a, n_heads):
    B, T, _ = a.shape
    return a.reshape(B, T, n_heads, -1).transpose(0, 2, 1, 3)


def merge_heads(a):
    B, H, T, d = a.shape
    return a.transpose(0, 2, 1, 3).reshape(B, T, H * d)


def layer_norm(x, g, b, eps=1e-5):
    xf = x.astype(F32)
    mu = jnp.mean(xf, axis=-1, keepdims=True)
    var = jnp.mean(jnp.square(xf - mu), axis=-1, keepdims=True)
    return ((xf - mu) * lax.rsqrt(var + eps) * g.astype(F32) + b.astype(F32)).astype(x.dtype)


def rms_norm(x, g, eps=1e-6):
    xf = x.astype(F32)
    return xf * lax.rsqrt(jnp.mean(jnp.square(xf), axis=-1, keepdims=True) + eps) * g.astype(F32)


def head_layer_norm(x, eps=1e-6):
    xf = x.astype(F32)
    mu = jnp.mean(xf, axis=-1, keepdims=True)
    var = jnp.mean(jnp.square(xf - mu), axis=-1, keepdims=True)
    return (xf - mu) * lax.rsqrt(var + eps)


def chunk_gated_linear_attention(q, k, v, log_f, chunk):
    B, H, T, dk = q.shape
    dv = v.shape[-1]
    n = T // chunk

    def to_chunks(a):
        return jnp.moveaxis(a.astype(F32).reshape(B, H, n, chunk, a.shape[-1]), 2, 0)

    causal = jnp.tril(jnp.ones((chunk, chunk), bool))

    def step(S, inp):
        qc, kc, vc, gc = inp
        b = jnp.cumsum(gc, axis=-2)
        o_inter = jnp.einsum('bhld,bhdv->bhlv', qc * jnp.exp(b), S)
        diff = b[:, :, :, None, :] - b[:, :, None, :, :]
        decay = jnp.exp(jnp.where(causal[:, :, None], diff, -jnp.inf))
        scores = jnp.einsum('bhid,bhjd,bhijd->bhij', qc, kc, decay)
        o_intra = jnp.einsum('bhij,bhjv->bhiv', scores, vc)
        b_last = b[:, :, -1, :]
        k_dec = kc * jnp.exp(b_last[:, :, None, :] - b)
        S = jnp.exp(b_last)[..., None] * S + jnp.einsum('bhjd,bhjv->bhdv', k_dec, vc)
        return S, o_inter + o_intra

    S0 = jnp.zeros((B, H, dk, dv), F32)
    _, o = lax.scan(step, S0, (to_chunks(q), to_chunks(k), to_chunks(v), to_chunks(log_f)))
    return jnp.moveaxis(o, 0, 2).reshape(B, H, T, dv)


def chunk_mlstm(q, k, v, i_pre, f_pre, chunk):
    B, H, T, dk = q.shape
    dv = v.shape[-1]
    n = T // chunk
    k = k.astype(F32) * (dk ** -0.5)
    log_f = jax.nn.log_sigmoid(f_pre.astype(F32))

    def to_chunks(a):
        return jnp.moveaxis(a.astype(F32).reshape(B, H, n, chunk, a.shape[-1]), 2, 0)

    def to_chunks_s(a):
        return jnp.moveaxis(a.astype(F32).reshape(B, H, n, chunk), 2, 0)

    causal = jnp.tril(jnp.ones((chunk, chunk), bool))

    def step(carry, inp):
        C, nv, m = carry
        qc, kc, vc, ic, gc = inp
        b = jnp.cumsum(gc, axis=-1)
        D = jnp.where(causal, b[..., :, None] - b[..., None, :] + ic[..., None, :], -jnp.inf)
        inter_log = b + m[..., None]
        m_i = jnp.maximum(inter_log, jnp.max(D, axis=-1))
        w_intra = jnp.exp(D - m_i[..., None])
        w_inter = jnp.exp(inter_log - m_i)
        scores = jnp.einsum('bhid,bhjd->bhij', qc, kc) * w_intra
        num = (jnp.einsum('bhij,bhjv->bhiv', scores, vc)
               + w_inter[..., None] * jnp.einsum('bhid,bhdv->bhiv', qc, C))
        den = jnp.sum(scores, axis=-1) + w_inter * jnp.einsum('bhid,bhd->bhi', qc, nv)
        h = num / jnp.maximum(jnp.abs(den), jnp.exp(-m_i))[..., None]
        b_last = b[..., -1]
        log_wj = b_last[..., None] - b + ic
        m_new = jnp.maximum(b_last + m, jnp.max(log_wj, axis=-1))
        wj = jnp.exp(log_wj - m_new[..., None])
        carry_dec = jnp.exp(b_last + m - m_new)
        C = carry_dec[..., None, None] * C + jnp.einsum('bhjd,bhjv->bhdv', kc * wj[..., None], vc)
        nv = carry_dec[..., None] * nv + jnp.einsum('bhjd,bhj->bhd', kc, wj)
        return (C, nv, m_new), h

    init = (jnp.zeros((B, H, dk, dv), F32), jnp.zeros((B, H, dk), F32), jnp.zeros((B, H), F32))
    _, h = lax.scan(step, init, (to_chunks(q), to_chunks(k), to_chunks(v),
                                 to_chunks_s(i_pre), to_chunks_s(log_f)))
    return jnp.moveaxis(h, 0, 2).reshape(B, H, T, dv)


def memory_attention(q_flat, mem, w_k, w_v):
    B, T, _ = q_flat.shape
    q = q_flat.reshape(B, T, MEM_HEADS, MEM_DH)
    k = (mem @ w_k).reshape(B, -1, MEM_HEADS, MEM_DH)
    v = (mem @ w_v).reshape(B, -1, MEM_HEADS, MEM_DH)
    s = jnp.einsum('bthd,bmhd->bhtm', q, k).astype(F32) * (MEM_DH ** -0.5)
    p = jax.nn.softmax(s, axis=-1)
    return jnp.einsum('bhtm,bmhd->bthd', p, v.astype(F32)).reshape(B, T, MEM_WIDTH)


def causal_conv(x, w, b):
    K = w.shape[0]
    T = x.shape[1]
    xp = jnp.pad(x, ((0, 0), (K - 1, 0), (0, 0)))
    return sum(xp[:, j:j + T] * w[j] for j in range(K)) + b


def block_diag_proj(a, w):
    B, T, _ = a.shape
    nb, bi, bo = w.shape
    return jnp.einsum('btni,nio->btno', a.reshape(B, T, nb, bi), w).reshape(B, T, nb * bo)


def even_layer(x, mem, layer, lb_logits, w_in, gla_w_a2, gla_b_a, gla_norm_g, hgrn_norm_g,
               mem_w_k, mem_w_v, w_out, ln_g, ln_b):
    h = x @ w_in
    gq, gk, gv, gg, ga, hq, hf, hi, hg, mq, mg = split_cols(h, EVEN_SPLITS)
    log_a = jax.nn.log_sigmoid((ga @ gla_w_a2 + gla_b_a).astype(F32)) / GLA_TAU
    o = chunk_gated_linear_attention(to_heads(gq, GLA_HEADS) * (GLA_DK ** -0.5),
                                     to_heads(gk, GLA_HEADS), to_heads(gv, GLA_HEADS),
                                     to_heads(log_a, GLA_HEADS), LIN_CHUNK)
    gla_out = merge_heads(rms_norm(o, gla_norm_g)) * jax.nn.silu(gg.astype(F32))
    lb = jnp.cumsum(jax.nn.softmax(lb_logits.astype(F32), axis=0), axis=0)[layer]
    f = lb + (1.0 - lb) * jax.nn.sigmoid(hf.astype(F32))
    o = chunk_gated_linear_attention(to_heads(jax.nn.silu(hq.astype(F32)), HGRN_HEADS),
                                     to_heads(1.0 - f, HGRN_HEADS), to_heads(hi, HGRN_HEADS),
                                     to_heads(jnp.log(f), HGRN_HEADS), LIN_CHUNK)
    hgrn_out = merge_heads(rms_norm(o, hgrn_norm_g)) * jax.nn.silu(hg.astype(F32))
    mem_out = memory_attention(mq, mem, mem_w_k, mem_w_v) * jax.nn.silu(mg.astype(F32))
    y = jnp.concatenate([gla_out, hgrn_out, mem_out], axis=-1).astype(x.dtype) @ w_out
    return layer_norm(ALPHA * x + y, ln_g, ln_b)


def odd_layer(x, mem, w_in, conv_w, conv_b, w_q, w_k, w_v, w_if, b_if, mh_norm_g, skip,
              mem_w_k, mem_w_v, w_out, ln_g, ln_b):
    h = x @ w_in
    xm, z, mq, mg = split_cols(h, ODD_SPLITS)
    xc = jax.nn.silu(causal_conv(xm, conv_w, conv_b))
    q = block_diag_proj(xc, w_q)
    k = block_diag_proj(xc, w_k)
    v = block_diag_proj(xm, w_v)
    gates = (jnp.concatenate([q, k, v], axis=-1) @ w_if + b_if).astype(F32)
    i_pre = gates[..., :MLSTM_HEADS].transpose(0, 2, 1)
    f_pre = gates[..., MLSTM_HEADS:].transpose(0, 2, 1)
    hh = chunk_mlstm(to_heads(q, MLSTM_HEADS), to_heads(k, MLSTM_HEADS), to_heads(v, MLSTM_HEADS),
                     i_pre, f_pre, MLSTM_CHUNK)
    hn = merge_heads(head_layer_norm(hh)) * mh_norm_g.astype(F32)
    mlstm_out = (hn + skip.astype(F32) * xc.astype(F32)) * jax.nn.silu(z.astype(F32))
    mem_out = memory_attention(mq, mem, mem_w_k, mem_w_v) * jax.nn.silu(mg.astype(F32))
    y = jnp.concatenate([mlstm_out, mem_out], axis=-1).astype(x.dtype) @ w_out
    return layer_norm(ALPHA * x + y, ln_g, ln_b)


def setup_inputs(seed: int = 0) -> dict:
    key = jax.random.key(seed)
    ks = iter(jax.random.split(key, 40))

    def nrm(shape, scale):
        return jax.random.normal(next(ks), shape, F32) * scale

    D = D_MODEL
    b_if_i = nrm((N_ODD, MLSTM_HEADS), 0.1)
    b_if_f = jnp.linspace(3.0, 6.0, MLSTM_HEADS, dtype=F32)[None] + nrm((N_ODD, MLSTM_HEADS), 0.1)
    return {
        "x": nrm((BATCH, SEQ, D), 1.0),
        "mem": nrm((BATCH, MEM_LEN, D), 1.0),
        "hgrn_lb_logits": nrm((DEPTH + 1, HGRN_FWIDTH), 0.1),
        "ev_w_in": nrm((N_EVEN, D, EVEN_IN), D ** -0.5),
        "ev_gla_w_a2": nrm((N_EVEN, GLA_LOWRANK, GLA_KWIDTH), GLA_LOWRANK ** -0.5),
        "ev_gla_b_a": nrm((N_EVEN, GLA_KWIDTH), 0.1),
        "ev_gla_norm_g": 1.0 + nrm((N_EVEN, GLA_DV), 0.02),
        "ev_hgrn_norm_g": 1.0 + nrm((N_EVEN, HGRN_DV), 0.02),
        "ev_mem_w_k": nrm((N_EVEN, D, MEM_WIDTH), D ** -0.5),
        "ev_mem_w_v": nrm((N_EVEN, D, MEM_WIDTH), BETA * D ** -0.5),
        "ev_w_out": nrm((N_EVEN, EVEN_OUT, D), BETA * EVEN_OUT ** -0.5),
        "ev_ln_g": 1.0 + nrm((N_EVEN, D), 0.02),
        "ev_ln_b": nrm((N_EVEN, D), 0.02),
        "od_w_in": nrm((N_ODD, D, ODD_IN), D ** -0.5),
        "od_conv_w": nrm((N_ODD, MLSTM_CONV, MLSTM_WIDTH), MLSTM_CONV ** -0.5),
        "od_conv_b": nrm((N_ODD, MLSTM_WIDTH), 0.02),
        "od_w_q": nrm((N_ODD, MLSTM_WIDTH // MLSTM_QKV_BLOCK, MLSTM_QKV_BLOCK, MLSTM_QKV_BLOCK), MLSTM_QKV_BLOCK ** -0.5),
        "od_w_k": nrm((N_ODD, MLSTM_WIDTH // MLSTM_QKV_BLOCK, MLSTM_QKV_BLOCK, MLSTM_QKV_BLOCK), MLSTM_QKV_BLOCK ** -0.5),
        "od_w_v": nrm((N_ODD, MLSTM_WIDTH // MLSTM_QKV_BLOCK, MLSTM_QKV_BLOCK, MLSTM_QKV_BLOCK), MLSTM_QKV_BLOCK ** -0.5),
        "od_w_if": nrm((N_ODD, 3 * MLSTM_WIDTH, 2 * MLSTM_HEADS), (3 * MLSTM_WIDTH) ** -0.5),
        "od_b_if": jnp.concatenate([b_if_i, b_if_f], axis=-1),
        "od_mh_norm_g": 1.0 + nrm((N_ODD, MLSTM_WIDTH), 0.02),
        "od_skip": 1.0 + nrm((N_ODD, MLSTM_WIDTH), 0.02),
        "od_mem_w_k": nrm((N_ODD, D, MEM_WIDTH), D ** -0.5),
        "od_mem_w_v": nrm((N_ODD, D, MEM_WIDTH), BETA * D ** -0.5),
        "od_w_out": nrm((N_ODD, ODD_OUT, D), BETA * ODD_OUT ** -0.5),
        "od_ln_g": 1.0 + nrm((N_ODD, D), 0.02),
        "od_ln_b": nrm((N_ODD, D), 0.02),
    }


def reference(x, mem, hgrn_lb_logits, ev_w_in, ev_gla_w_a2, ev_gla_b_a, ev_gla_norm_g, ev_hgrn_norm_g,
              ev_mem_w_k, ev_mem_w_v, ev_w_out, ev_ln_g, ev_ln_b, od_w_in, od_conv_w, od_conv_b,
              od_w_q, od_w_k, od_w_v, od_w_if, od_b_if, od_mh_norm_g, od_skip, od_mem_w_k,
              od_mem_w_v, od_w_out, od_ln_g, od_ln_b):
    for layer in range(DEPTH):
        i = layer // 2
        if layer % 2 == 0:
            x = even_layer(x, mem, layer, hgrn_lb_logits, ev_w_in[i], ev_gla_w_a2[i], ev_gla_b_a[i],
                           ev_gla_norm_g[i], ev_hgrn_norm_g[i], ev_mem_w_k[i], ev_mem_w_v[i],
                           ev_w_out[i], ev_ln_g[i], ev_ln_b[i])
        else:
            x = odd_layer(x, mem, od_w_in[i], od_conv_w[i], od_conv_b[i], od_w_q[i], od_w_k[i],
                          od_w_v[i], od_w_if[i], od_b_if[i], od_mh_norm_g[i], od_skip[i],
                          od_mem_w_k[i], od_mem_w_v[i], od_w_out[i], od_ln_g[i], od_ln_b[i])
    return x
```

```python
import functools

import jax
import jax.numpy as jnp
from jax import lax
from jax.experimental import pallas as pl
from jax.experimental.pallas import tpu as pltpu

F32 = jnp.float32
BF16 = jnp.bfloat16

GLA_HEADS = 4
GLA_TAU = 16.0
HGRN_EXPAND = 128
MEM_HEADS = 4

LIN_CHUNK = 64
LIN_SUB = 16
MLSTM_CHUNK = 128

V7X_VMEM_LIMIT_BYTES = 56 * 1024 * 1024
NEG_BIG = -1e30


def _sigmoid(x):
    return 1.0 / (1.0 + jnp.exp(-x))


def _silu(x):
    return x * _sigmoid(x)


def _log_sigmoid(x):
    return jnp.minimum(x, 0.0) - jnp.log(1.0 + jnp.exp(-jnp.abs(x)))


def _dot(a, b):
    return jnp.dot(a, b, preferred_element_type=F32)


def _dot_nt(a, b):
    return lax.dot_general(a, b, (((1,), (1,)), ((), ())), preferred_element_type=F32)


def _dot_tn(a, b):
    return lax.dot_general(a, b, (((0,), (0,)), ((), ())), preferred_element_type=F32)


def _params(*sem):
    return pltpu.CompilerParams(dimension_semantics=sem, vmem_limit_bytes=V7X_VMEM_LIMIT_BYTES)


def _mm_kernel(a_ref, w_ref, o_ref):
    o_ref[...] = _dot(a_ref[...], w_ref[...]).astype(o_ref.dtype)


def _matmul(a, w, out_dtype, tm=1024, tn=512):
    m, k = a.shape
    n = w.shape[1]
    tm, tn = min(tm, m), min(tn, n)
    assert m % tm == 0 and n % tn == 0, (m, n, tm, tn)
    return pl.pallas_call(
        _mm_kernel,
        grid=(m // tm, n // tn),
        in_specs=[pl.BlockSpec((tm, k), lambda i, j: (i, 0)),
                  pl.BlockSpec((k, tn), lambda i, j: (0, j))],
        out_specs=pl.BlockSpec((tm, tn), lambda i, j: (i, j)),
        out_shape=jax.ShapeDtypeStruct((m, n), out_dtype),
        compiler_params=_params("parallel", "arbitrary"),
        name="proj_matmul",
    )(a, w)


def _cumsum_rows(g, tril_bf):
    hi = g.astype(BF16)
    r1 = g - hi.astype(F32)
    mid = r1.astype(BF16)
    lo = (r1 - mid.astype(F32)).astype(BF16)
    return _dot(tril_bf, hi) + _dot(tril_bf, mid) + _dot(tril_bf, lo)


def _lin_attn_chunk(q, k, v_bf, g, st_ref, q_scr, k_scr, b_scr):
    L, dk = q.shape
    C = LIN_SUB
    row = lax.broadcasted_iota(jnp.int32, (L, L), 0)
    col = lax.broadcasted_iota(jnp.int32, (L, L), 1)
    tril_bf = jnp.where(row >= col, 1.0, 0.0).astype(BF16)
    b = _cumsum_rows(g, tril_bf)
    b_last = b[L - 1:L, :]
    q_scr[...] = q
    k_scr[...] = k
    b_scr[...] = b

    o = _dot_nt((q * jnp.exp(b)).astype(BF16), st_ref[...].astype(BF16))

    rowl = lax.broadcasted_iota(jnp.int32, (L, 1), 0)
    ri = lax.broadcasted_iota(jnp.int32, (C, 1), 0)
    lane = lax.broadcasted_iota(jnp.int32, (C, L), 1)
    blocks = []
    for s in range(L // C):
        lo = s * C
        qb = q_scr[lo:lo + C, :]
        bb = b_scr[lo:lo + C, :]
        if s > 0:
            r = b_scr[lo - 1:lo, :]
            qs = qb * jnp.exp(bb - r)
            ks = k * jnp.exp(jnp.where(rowl < lo, r - b, NEG_BIG))
            a = _dot_nt(qs.astype(BF16), ks.astype(BF16))
        else:
            a = jnp.zeros((C, L), F32)
        for j in range(C):
            bj = b_scr[lo + j:lo + j + 1, :]
            kj = k_scr[lo + j:lo + j + 1, :]
            d = jnp.where(ri >= j, bb - bj, NEG_BIG)
            p = qb * kj * jnp.exp(d)
            a = jnp.where(lane == lo + j, jnp.sum(p, axis=-1, keepdims=True), a)
        blocks.append(a)
    scores = jnp.concatenate(blocks, axis=0)
    o = o + _dot(scores.astype(BF16), v_bf)

    k_dec = (k * jnp.exp(b_last - b)).astype(BF16)
    st_ref[...] = st_ref[...] * jnp.exp(b_last) + _dot_tn(v_bf, k_dec)
    return o


def _rms_gate(o, norm_g, gate, eps=1e-6):
    ms = jnp.mean(o * o, axis=-1, keepdims=True)
    return o * lax.rsqrt(ms + eps) * norm_g * _silu(gate)


def _gla_kernel(q_ref, k_ref, v_ref, gg_ref, ga_ref, wa_ref, ba_ref, ng_ref, o_ref,
                st_ref, q_scr, k_scr, b_scr, *, n_chunks, q_scale):
    @pl.when(pl.program_id(2) == 0)
    def _():
        st_ref[...] = jnp.zeros_like(st_ref)

    def chunk(c, carry):
        rows = pl.ds(pl.multiple_of(c * LIN_CHUNK, LIN_CHUNK), LIN_CHUNK)
        q = q_ref[rows, :].astype(F32) * q_scale
        k = k_ref[rows, :].astype(F32)
        a_pre = _dot(ga_ref[rows, :].astype(BF16), wa_ref[...]) + ba_ref[...]
        g = _log_sigmoid(a_pre) * (1.0 / GLA_TAU)
        o = _lin_attn_chunk(q, k, v_ref[rows, :], g, st_ref, q_scr, k_scr, b_scr)
        o_ref[rows, :] = _rms_gate(o, ng_ref[...], gg_ref[rows, :].astype(F32)).astype(o_ref.dtype)
        return carry

    lax.fori_loop(0, n_chunks, chunk, 0)


def _hgrn_kernel(q_ref, f_ref, v_ref, hg_ref, lb_ref, ng_ref, o_ref,
                 st_ref, q_scr, k_scr, b_scr, *, n_chunks):
    @pl.when(pl.program_id(2) == 0)
    def _():
        st_ref[...] = jnp.zeros_like(st_ref)

    def chunk(c, carry):
        rows = pl.ds(pl.multiple_of(c * LIN_CHUNK, LIN_CHUNK), LIN_CHUNK)
        lb = lb_ref[...]
        f = lb + (1.0 - lb) * _sigmoid(f_ref[rows, :])
        q = _silu(q_ref[rows, :].astype(F32))
        o = _lin_attn_chunk(q, 1.0 - f, v_ref[rows, :], jnp.log(f), st_ref, q_scr, k_scr, b_scr)
        o_ref[rows, :] = _rms_gate(o, ng_ref[...], hg_ref[rows, :].astype(F32)).astype(o_ref.dtype)
        return carry

    lax.fori_loop(0, n_chunks, chunk, 0)


def _col_spec(tb, width, col0):
    assert col0 % width == 0, (col0, width)
    base = col0 // width
    return pl.BlockSpec((None, tb, width), lambda b, h, t: (b, t, base + h))


def _lin_scratch(dk, dv):
    return [pltpu.VMEM((dv, dk), F32), pltpu.VMEM((LIN_CHUNK, dk), F32),
            pltpu.VMEM((LIN_CHUNK, dk), F32), pltpu.VMEM((LIN_CHUNK, dk), F32)]


def _gla_call(h_bf, h_f, wa_pad, b_a, norm_g, *, offs, ga_col, dk, dv, tb):
    B, T, _ = h_bf.shape
    tb = min(tb, T)
    ga_w = wa_pad.shape[0]
    assert ga_col % ga_w == 0
    kern = functools.partial(_gla_kernel, n_chunks=tb // LIN_CHUNK, q_scale=dk ** -0.5)
    return pl.pallas_call(
        kern,
        grid=(B, GLA_HEADS, T // tb),
        in_specs=[_col_spec(tb, dk, offs["gq"]), _col_spec(tb, dk, offs["gk"]),
                  _col_spec(tb, dv, offs["gv"]), _col_spec(tb, dv, offs["gg"]),
                  pl.BlockSpec((None, tb, ga_w), lambda b, h, t: (b, t, ga_col // ga_w)),
                  pl.BlockSpec((ga_w, dk), lambda b, h, t: (0, h)),
                  pl.BlockSpec((1, dk), lambda b, h, t: (0, h)),
                  pl.BlockSpec((1, dv), lambda b, h, t: (0, 0))],
        out_specs=pl.BlockSpec((None, tb, dv), lambda b, h, t: (b, t, h)),
        out_shape=jax.ShapeDtypeStruct((B, T, GLA_HEADS * dv), BF16),
        scratch_shapes=_lin_scratch(dk, dv),
        compiler_params=_params("parallel", "parallel", "arbitrary"),
        name="gla_chunks",
    )(h_bf, h_bf, h_bf, h_bf, h_f, wa_pad, b_a, norm_g)


def _hgrn_call(h_bf, h_f, lb, norm_g, *, offs, heads, dk, dv, tb):
    B, T, _ = h_bf.shape
    tb = min(tb, T)
    kern = functools.partial(_hgrn_kernel, n_chunks=tb // LIN_CHUNK)
    return pl.pallas_call(
        kern,
        grid=(B, heads, T // tb),
        in_specs=[_col_spec(tb, dk, offs["hq"]),
                  pl.BlockSpec((None, tb, dk), lambda b, h, t: (b, t, h)),
                  _col_spec(tb, dv, offs["hi"]), _col_spec(tb, dv, offs["hg"]),
                  pl.BlockSpec((1, dk), lambda b, h, t: (0, h)),
                  pl.BlockSpec((1, dv), lambda b, h, t: (0, 0))],
        out_specs=pl.BlockSpec((None, tb, dv), lambda b, h, t: (b, t, h)),
        out_shape=jax.ShapeDtypeStruct((B, T, heads * dv), BF16),
        scratch_shapes=_lin_scratch(dk, dv),
        compiler_params=_params("parallel", "parallel", "arbitrary"),
        name="hgrn_chunks",
    )(h_bf, h_f, h_bf, h_bf, lb, norm_g)


def _memattn_kernel(q_ref, g_ref, k_ref, v_ref, o_ref, *, dh):
    scale = dh ** -0.5
    for h in range(MEM_HEADS):
        cs = slice(h * dh, (h + 1) * dh)
        s = _dot_nt(q_ref[:, cs], k_ref[:, cs]) * scale
        s = s - jnp.max(s, axis=-1, keepdims=True)
        p = jnp.exp(s)
        p = p / jnp.sum(p, axis=-1, keepdims=True)
        o = _dot(p.astype(BF16), v_ref[:, cs])
        o_ref[:, cs] = (o * _silu(g_ref[:, cs].astype(F32))).astype(o_ref.dtype)


def _memattn_call(h_bf, k_mem, v_mem, *, q_col, g_col, tm=512):
    B, T, _ = h_bf.shape
    _, ml, mw = k_mem.shape
    tm = min(tm, T)
    assert q_col % mw == 0 and g_col % mw == 0
    return pl.pallas_call(
        functools.partial(_memattn_kernel, dh=mw // MEM_HEADS),
        grid=(B, T // tm),
        in_specs=[pl.BlockSpec((None, tm, mw), lambda b, t: (b, t, q_col // mw)),
                  pl.BlockSpec((None, tm, mw), lambda b, t: (b, t, g_col // mw)),
                  pl.BlockSpec((None, ml, mw), lambda b, t: (b, 0, 0)),
                  pl.BlockSpec((None, ml, mw), lambda b, t: (b, 0, 0))],
        out_specs=pl.BlockSpec((None, tm, mw), lambda b, t: (b, t, 0)),
        out_shape=jax.ShapeDtypeStruct((B, T, mw), BF16),
        compiler_params=_params("parallel", "parallel"),
        name="mem_attention",
    )(h_bf, h_bf, k_mem, v_mem)


def _outproj_kernel(*refs, segs, nk, alpha, eps):
    ns = len(segs)
    a_refs = refs[:ns]
    w_ref, x_ref, g_ref, b_ref, o_ref = refs[ns:]
    k = pl.program_id(1)

    @pl.when(k == 0)
    def _():
        o_ref[...] = jnp.zeros_like(o_ref)

    for a_ref, (start, n) in zip(a_refs, segs):
        @pl.when((k >= start) & (k < start + n))
        def _(a_ref=a_ref):
            o_ref[...] += _dot(a_ref[...], w_ref[...])

    @pl.when(k == nk - 1)
    def _():
        y = alpha * x_ref[...] + o_ref[...]
        mu = jnp.mean(y, axis=-1, keepdims=True)
        yc = y - mu
        var = jnp.mean(yc * yc, axis=-1, keepdims=True)
        o_ref[...] = yc * lax.rsqrt(var + eps) * g_ref[...] + b_ref[...]


def _outproj_ln(acts, w_bf, x2d, ln_g, ln_b, alpha, tm=512, tk=512):
    m, d = x2d.shape
    tm = min(tm, m)
    segs, start = [], 0
    for a in acts:
        assert a.shape[0] == m and a.shape[1] % tk == 0
        segs.append((start, a.shape[1] // tk))
        start += a.shape[1] // tk
    nk = start
    assert nk * tk == w_bf.shape[0]

    def a_spec(s0, n):
        return pl.BlockSpec((tm, tk), lambda i, k: (i, jnp.clip(k - s0, 0, n - 1)))

    kern = functools.partial(_outproj_kernel, segs=tuple(segs), nk=nk, alpha=alpha, eps=1e-5)
    return pl.pallas_call(
        kern,
        grid=(m // tm, nk),
        in_specs=[a_spec(s0, n) for s0, n in segs] + [
            pl.BlockSpec((tk, d), lambda i, k: (k, 0)),
            pl.BlockSpec((tm, d), lambda i, k: (i, 0)),
            pl.BlockSpec((1, d), lambda i, k: (0, 0)),
            pl.BlockSpec((1, d), lambda i, k: (0, 0))],
        out_specs=pl.BlockSpec((tm, d), lambda i, k: (i, 0)),
        out_shape=jax.ShapeDtypeStruct((m, d), F32),
        compiler_params=_params("parallel", "arbitrary"),
        name="outproj_layernorm",
    )(*acts, w_bf, x2d, ln_g, ln_b)


def _mlstm_pre_kernel(x_ref, halo_ref, cw_ref, cb_ref, wq_ref, wk_ref, wkt_ref, wv_ref,
                      iq_ref, ik_ref, iv_ref, bif_ref,
                      xc_ref, q_ref, k_ref, kt_ref, v_ref, g_ref, *, n_sub, bw, n_taps):
    t = pl.program_id(1)
    c = pl.program_id(2)
    tm = x_ref.shape[0]

    @pl.when(c == 0)
    def _():
        g_ref[...] = jnp.broadcast_to(bif_ref[...], g_ref.shape)

    gates = jnp.zeros(g_ref.shape, F32)
    for j in range(n_sub):
        cs = slice(j * bw, (j + 1) * bw)
        x_bf = x_ref[:, cs]
        x32 = x_bf.astype(F32)
        halo = jnp.where(t > 0, halo_ref[:, cs].astype(F32), 0.0)
        ext = jnp.concatenate([halo, x32], axis=0)
        acc = jnp.broadcast_to(cb_ref[:, cs], (tm, bw))
        for tap in range(n_taps):
            sh = 8 - (n_taps - 1) + tap
            acc = acc + ext[sh:sh + tm, :] * cw_ref[tap:tap + 1, cs]
        xc_bf = _silu(acc).astype(BF16)
        xc_ref[:, cs] = xc_bf
        q_bf = _dot(xc_bf, wq_ref[j]).astype(BF16)
        k_bf = _dot(xc_bf, wk_ref[j]).astype(BF16)
        v_bf = _dot(x_bf, wv_ref[j]).astype(BF16)
        q_ref[:, cs] = q_bf
        k_ref[:, cs] = k_bf
        v_ref[:, cs] = v_bf
        kt_ref[cs, :] = _dot_nt(wkt_ref[j], xc_bf).astype(BF16)
        gates = gates + _dot(q_bf, iq_ref[cs, :]) + _dot(k_bf, ik_ref[cs, :]) + _dot(v_bf, iv_ref[cs, :])
    g_ref[...] += gates


def _block_diag(w, bw):
    nb, bi, bo = w.shape
    assert bi == bo and bw % bi == 0 and (nb * bi) % bw == 0
    per = bw // bi
    w4 = w.reshape(nb // per, per, bi, bo)
    eye = jnp.eye(per, dtype=w.dtype)
    return jnp.einsum("cnio,nm->cnimo", w4, eye).reshape(nb // per, bw, bw)


def _mlstm_pre_call(h_bf, conv_w, conv_b, w_q, w_k, w_v, w_if, b_if, *, width, tm=512, tc=1024, bw=256):
    B, T, _ = h_bf.shape
    tm, tc = min(tm, T), min(tc, width)
    n_taps = conv_w.shape[0]
    assert tm % 8 == 0 and n_taps <= 8 and width % tc == 0 and tc % bw == 0
    n_sub = tc // bw
    wq = _block_diag(w_q, bw).astype(BF16)
    wk = _block_diag(w_k, bw)
    wkt = jnp.swapaxes(wk, 1, 2).astype(BF16)
    wk = wk.astype(BF16)
    wv = _block_diag(w_v, bw).astype(BF16)
    n_gate = w_if.shape[1]
    wif = jnp.pad(w_if, ((0, 0), (0, 128 - n_gate))).astype(BF16)
    bif = jnp.pad(b_if.astype(F32), (0, 128 - n_gate)).reshape(1, 128)
    hb = tm // 8
    tile_spec = pl.BlockSpec((n_sub, bw, bw), lambda b, t, c: (c, 0, 0))
    act_spec = pl.BlockSpec((None, tm, tc), lambda b, t, c: (b, t, c))
    kern = functools.partial(_mlstm_pre_kernel, n_sub=n_sub, bw=bw, n_taps=n_taps)
    shp = jax.ShapeDtypeStruct((B, T, width), BF16)
    return pl.pallas_call(
        kern,
        grid=(B, T // tm, width // tc),
        in_specs=[act_spec,
                  pl.BlockSpec((None, 8, tc), lambda b, t, c: (b, jnp.maximum(t * hb - 1, 0), c)),
                  pl.BlockSpec((n_taps, tc), lambda b, t, c: (0, c)),
                  pl.BlockSpec((1, tc), lambda b, t, c: (0, c)),
                  tile_spec, tile_spec, tile_spec, tile_spec,
                  pl.BlockSpec((tc, 128), lambda b, t, c: (c, 0)),
                  pl.BlockSpec((tc, 128), lambda b, t, c: (width // tc + c, 0)),
                  pl.BlockSpec((tc, 128), lambda b, t, c: (2 * (width // tc) + c, 0)),
                  pl.BlockSpec((1, 128), lambda b, t, c: (0, 0))],
        out_specs=[act_spec, act_spec, act_spec,
                   pl.BlockSpec((None, tc, tm), lambda b, t, c: (b, c, t)),
                   act_spec,
                   pl.BlockSpec((None, tm, 128), lambda b, t, c: (b, t, 0))],
        out_shape=[shp, shp, shp, jax.ShapeDtypeStruct((B, width, T), BF16), shp,
                   jax.ShapeDtypeStruct((B, T, 128), F32)],
        compiler_params=_params("parallel", "parallel", "arbitrary"),
        name="mlstm_frontend",
    )(h_bf, h_bf, conv_w, conv_b.reshape(1, width), wq, wk, wkt, wv, wif, wif, wif, bif)


def _mlstm_kernel(q_ref, k_ref, kt_ref, v_ref, gc_ref, gr_ref, xc_ref, z_ref, ng_ref, sk_ref, o_ref,
                  c_ref, cb_ref, n_ref, m_ref, *, n_chunks, heads, k_scale, dvb):
    h = pl.program_id(1)
    L = MLSTM_CHUNK
    dk, dv = c_ref.shape

    @pl.when(pl.program_id(2) == 0)
    def _():
        c_ref[...] = jnp.zeros_like(c_ref)
        cb_ref[...] = jnp.zeros_like(cb_ref)
        n_ref[...] = jnp.zeros_like(n_ref)
        m_ref[...] = jnp.zeros_like(m_ref)

    row = lax.broadcasted_iota(jnp.int32, (L, L), 0)
    col = lax.broadcasted_iota(jnp.int32, (L, L), 1)
    causal = row >= col
    glane = lax.broadcasted_iota(jnp.int32, (L, gc_ref.shape[1]), 1)

    for c in range(n_chunks):
        rs = slice(c * L, (c + 1) * L)
        qc = q_ref[rs, :]
        kc = k_ref[rs, :]
        vc = v_ref[rs, :]
        gcol = gc_ref[rs, :]
        i_col = jnp.sum(jnp.where(glane == h, gcol, 0.0), axis=1, keepdims=True)
        f_col = jnp.sum(jnp.where(glane == heads + h, gcol, 0.0), axis=1, keepdims=True)
        i_row = gr_ref[0:1, rs]
        f_row = gr_ref[1:2, rs]
        lf_col = _log_sigmoid(f_col)
        lf_row = _log_sigmoid(f_row)
        b_col = jnp.sum(jnp.where(causal, lf_row, 0.0), axis=1, keepdims=True)
        b_row = jnp.sum(jnp.where(row <= col, lf_col, 0.0), axis=0, keepdims=True)
        b_last = b_col[L - 1:L, :]
        m_prev = m_ref[0:1, 0:1]

        d = jnp.where(causal, b_col - b_row + i_row, NEG_BIG)
        inter_log = b_col + m_prev
        m_i = jnp.maximum(inter_log, jnp.max(d, axis=1, keepdims=True))
        w_intra = jnp.exp(d - m_i)
        w_inter = jnp.exp(inter_log - m_i) * k_scale
        scores = _dot_nt(qc, kc) * k_scale * w_intra
        n_row = n_ref[...]
        den = (jnp.sum(scores, axis=1, keepdims=True)
               + w_inter * jnp.sum(qc.astype(F32) * n_row, axis=1, keepdims=True))
        inv = 1.0 / jnp.maximum(jnp.abs(den), jnp.exp(-m_i))
        num = _dot(scores.astype(BF16), vc) + w_inter * _dot(qc, cb_ref[...])
        hh = num * inv

        log_wj_row = b_last - b_row + i_row
        log_wj_col = b_last - b_col + i_col
        m_new = jnp.maximum(b_last + m_prev, jnp.max(log_wj_row, axis=1, keepdims=True))
        wj_row = jnp.exp(log_wj_row - m_new)
        wj_col = jnp.exp(log_wj_col - m_new)
        dec = jnp.exp(b_last + m_prev - m_new)
        ktw = (kt_ref[:, rs].astype(F32) * wj_row).astype(BF16)
        for j in range(dv // dvb):
            cs = slice(j * dvb, (j + 1) * dvb)
            c_new = c_ref[:, cs] * dec + _dot(ktw, vc[:, cs])
            c_ref[:, cs] = c_new
            cb_ref[:, cs] = c_new.astype(BF16)
        n_ref[...] = n_row * dec + jnp.sum(kc.astype(F32) * wj_col, axis=0, keepdims=True)
        m_ref[...] = jnp.broadcast_to(m_new, m_ref.shape)

        mu = jnp.mean(hh, axis=-1, keepdims=True)
        hc = hh - mu
        var = jnp.mean(hc * hc, axis=-1, keepdims=True)
        hn = hc * lax.rsqrt(var + 1e-6) * ng_ref[...]
        out = (hn + sk_ref[...] * xc_ref[rs, :].astype(F32)) * _silu(z_ref[rs, :].astype(F32))
        o_ref[rs, :] = out.astype(o_ref.dtype)


def _mlstm_call(q, k, kt, v, gates, gates_rows, xc, h_bf, norm_g, skip, *, heads, z_col, tb=256):
    B, T, width = q.shape
    dh = width // heads
    tb = min(tb, T)
    assert tb % MLSTM_CHUNK == 0 and z_col % dh == 0
    dvb = min(512, dh)
    hs = pl.BlockSpec((None, tb, dh), lambda b, h, t: (b, t, h))
    kern = functools.partial(_mlstm_kernel, n_chunks=tb // MLSTM_CHUNK, heads=heads,
                             k_scale=dh ** -0.5, dvb=dvb)
    return pl.pallas_call(
        kern,
        grid=(B, heads, T // tb),
        in_specs=[hs, hs,
                  pl.BlockSpec((None, dh, tb), lambda b, h, t: (b, h, t)),
                  hs,
                  pl.BlockSpec((None, tb, gates.shape[2]), lambda b, h, t: (b, t, 0)),
                  pl.BlockSpec((None, None, 2, tb), lambda b, h, t: (b, h, 0, t)),
                  hs,
                  pl.BlockSpec((None, tb, dh), lambda b, h, t: (b, t, z_col // dh + h)),
                  pl.BlockSpec((1, dh), lambda b, h, t: (0, h)),
                  pl.BlockSpec((1, dh), lambda b, h, t: (0, h))],
        out_specs=hs,
        out_shape=jax.ShapeDtypeStruct((B, T, width), BF16),
        scratch_shapes=[pltpu.VMEM((dh, dh), F32), pltpu.VMEM((dh, dh), BF16),
                        pltpu.VMEM((1, dh), F32), pltpu.VMEM((8, 128), F32)],
        compiler_params=_params("parallel", "parallel", "arbitrary"),
        name="mlstm_chunks",
    )(q, k, kt, v, gates, gates_rows, xc, h_bf, norm_g, skip)


def _mem_kv(mem_bf, w_k, w_v, B):
    mw = w_k.shape[1]
    k = _matmul(mem_bf, w_k.astype(BF16), BF16).reshape(B, -1, mw)
    v = _matmul(mem_bf, w_v.astype(BF16), BF16).reshape(B, -1, mw)
    return k, v


def _even_layer(x, mem_bf, lb, w_in, gla_w_a2, gla_b_a, gla_norm_g, hgrn_norm_g,
                mem_w_k, mem_w_v, w_out, ln_g, ln_b, alpha):
    B, T, D = x.shape
    lowrank, kw = gla_w_a2.shape
    gdv = gla_norm_g.shape[0]
    gw = GLA_HEADS * gdv
    gdk = kw // GLA_HEADS
    fw = lb.shape[0]
    hheads = fw // HGRN_EXPAND
    hdv = hgrn_norm_g.shape[0]
    hw = hheads * hdv
    mw = mem_w_k.shape[1]
    sizes = (kw, kw, gw, gw, lowrank, fw, fw, hw, hw, mw, mw)
    assert sum(sizes) == w_in.shape[1]
    bounds = [0]
    for s in sizes:
        bounds.append(bounds[-1] + s)
    gq, gk, gv, gg, ga, hq, hf, hi, hg, mq, mg = (w_in[:, a:b] for a, b in zip(bounds[:-1], bounds[1:]))

    names = ("gq", "gk", "gv", "gg", "hq", "hi", "hg", "mq", "mg")
    parts = (gq, gk, gv, gg, hq, hi, hg, mq, mg)
    offs, o = {}, 0
    for nme, p in zip(names, parts):
        offs[nme] = o
        o += p.shape[1]
    w_bf = jnp.concatenate(parts, axis=1).astype(BF16)
    ga_w = 256
    w_f = jnp.concatenate([hf, jnp.pad(ga, ((0, 0), (0, ga_w - lowrank)))], axis=1).astype(BF16)

    x_bf = x.reshape(B * T, D).astype(BF16)
    h_bf = _matmul(x_bf, w_bf, BF16).reshape(B, T, -1)
    h_f = _matmul(x_bf, w_f, F32, tn=256).reshape(B, T, -1)

    wa_pad = jnp.pad(gla_w_a2, ((0, ga_w - lowrank), (0, 0))).astype(BF16)
    gla_out = _gla_call(h_bf, h_f, wa_pad, gla_b_a.reshape(1, kw), gla_norm_g.reshape(1, gdv),
                        offs=offs, ga_col=fw, dk=gdk, dv=gdv, tb=256)
    hgrn_out = _hgrn_call(h_bf, h_f, lb.reshape(1, fw), hgrn_norm_g.reshape(1, hdv),
                          offs=offs, heads=hheads, dk=HGRN_EXPAND, dv=hdv, tb=512)
    k_mem, v_mem = _mem_kv(mem_bf, mem_w_k, mem_w_v, B)
    mem_out = _memattn_call(h_bf, k_mem, v_mem, q_col=offs["mq"], g_col=offs["mg"])

    acts = [a.reshape(B * T, -1) for a in (gla_out, hgrn_out, mem_out)]
    y = _outproj_ln(acts, w_out.astype(BF16), x.reshape(B * T, D), ln_g.reshape(1, D), ln_b.reshape(1, D), alpha)
    return y.reshape(B, T, D)


def _odd_layer(x, mem_bf, w_in, conv_w, conv_b, w_q, w_k, w_v, w_if, b_if, mh_norm_g, skip,
               mem_w_k, mem_w_v, w_out, ln_g, ln_b, alpha):
    B, T, D = x.shape
    width = conv_w.shape[1]
    heads = b_if.shape[0] // 2
    mw = mem_w_k.shape[1]
    assert w_in.shape[1] == 2 * width + 2 * mw

    x_bf = x.reshape(B * T, D).astype(BF16)
    h_bf = _matmul(x_bf, w_in.astype(BF16), BF16).reshape(B, T, -1)

    xc, q, k, kt, v, gates = _mlstm_pre_call(h_bf, conv_w, conv_b, w_q, w_k, w_v, w_if, b_if, width=width)
    gates_rows = jnp.transpose(gates[:, :, :2 * heads].reshape(B, T, 2, heads), (0, 3, 2, 1))
    mlstm_out = _mlstm_call(q, k, kt, v, gates, gates_rows, xc, h_bf,
                            mh_norm_g.reshape(1, width), skip.reshape(1, width), heads=heads, z_col=width)
    k_mem, v_mem = _mem_kv(mem_bf, mem_w_k, mem_w_v, B)
    mem_out = _memattn_call(h_bf, k_mem, v_mem, q_col=2 * width, g_col=2 * width + mw)

    acts = [a.reshape(B * T, -1) for a in (mlstm_out, mem_out)]
    y = _outproj_ln(acts, w_out.astype(BF16), x.reshape(B * T, D), ln_g.reshape(1, D), ln_b.reshape(1, D), alpha)
    return y.reshape(B, T, D)


def kernel(x, mem, hgrn_lb_logits, ev_w_in, ev_gla_w_a2, ev_gla_b_a, ev_gla_norm_g, ev_hgrn_norm_g, ev_mem_w_k, ev_mem_w_v, ev_w_out, ev_ln_g, ev_ln_b, od_w_in, od_conv_w, od_conv_b, od_w_q, od_w_k, od_w_v, od_w_if, od_b_if, od_mh_norm_g, od_skip, od_mem_w_k, od_mem_w_v, od_w_out, od_ln_g, od_ln_b):
    depth = ev_w_in.shape[0] + od_w_in.shape[0]
    alpha = (2 * depth) ** 0.25
    B, ml, D = mem.shape
    mem_bf = mem.reshape(B * ml, D).astype(BF16)
    lb_all = jnp.cumsum(jax.nn.softmax(hgrn_lb_logits.astype(F32), axis=0), axis=0)
    for layer in range(depth):
        i = layer // 2
        if layer % 2 == 0:
            x = _even_layer(x, mem_bf, lb_all[layer], ev_w_in[i], ev_gla_w_a2[i], ev_gla_b_a[i],
                            ev_gla_norm_g[i], ev_hgrn_norm_g[i], ev_mem_w_k[i], ev_mem_w_v[i],
                            ev_w_out[i], ev_ln_g[i], ev_ln_b[i], alpha)
        else:
            x = _odd_layer(x, mem_bf, od_w_in[i], od_conv_w[i], od_conv_b[i], od_w_q[i], od_w_k[i],
                           od_w_v[i], od_w_if[i], od_b_if[i], od_mh_norm_g[i], od_skip[i],
                           od_mem_w_k[i], od_mem_w_v[i], od_w_out[i], od_ln_g[i], od_ln_b[i], alpha)
    return x
```

```python
import functools

import jax
import jax.numpy as jnp
import numpy as np
from jax import lax
from jax.experimental import pallas as pl
from jax.experimental.pallas import tpu as pltpu

F32 = jnp.float32
BF16 = jnp.bfloat16

GLA_HEADS = 4
GLA_TAU = 16.0
HGRN_EXPAND = 128
MEM_HEADS = 4

LIN_CHUNK = 64
MLSTM_CHUNK = 256

V7X_VMEM_LIMIT_BYTES = 56 * 1024 * 1024
NEG_BIG = -1e30


def _sigmoid(x):
    return 1.0 / (1.0 + jnp.exp(-x))


def _silu(x):
    return x * _sigmoid(x)


def _log_sigmoid(x):
    return jnp.minimum(x, 0.0) - jnp.log(1.0 + jnp.exp(-jnp.abs(x)))


def _dot(a, b):
    return jnp.dot(a, b, preferred_element_type=F32)


def _dot_nt(a, b):
    return lax.dot_general(a, b, (((1,), (1,)), ((), ())), preferred_element_type=F32)


def _dot_tn(a, b):
    return lax.dot_general(a, b, (((0,), (0,)), ((), ())), preferred_element_type=F32)


def _params(*sem):
    return pltpu.CompilerParams(dimension_semantics=sem, vmem_limit_bytes=V7X_VMEM_LIMIT_BYTES)


def _mm_kernel(a_ref, w_ref, o_ref):
    o_ref[...] = _dot(a_ref[...], w_ref[...]).astype(o_ref.dtype)


def _matmul(a, w, out_dtype, tm=1024, tn=512):
    m, k = a.shape
    n = w.shape[1]
    tm, tn = min(tm, m), min(tn, n)
    assert m % tm == 0 and n % tn == 0, (m, n, tm, tn)
    return pl.pallas_call(
        _mm_kernel,
        grid=(m // tm, n // tn),
        in_specs=[pl.BlockSpec((tm, k), lambda i, j: (i, 0)),
                  pl.BlockSpec((k, tn), lambda i, j: (0, j))],
        out_specs=pl.BlockSpec((tm, tn), lambda i, j: (i, j)),
        out_shape=jax.ShapeDtypeStruct((m, n), out_dtype),
        compiler_params=_params("parallel", "arbitrary"),
        name="proj_matmul",
    )(a, w)


def _lin_constants(L):
    i = np.arange(L)[:, None]
    t = np.arange(L)[None, :]
    sums = [t <= i]
    masks = [i == t]
    m = L // 2
    while m >= 1:
        r = (i // (2 * m)) * 2 * m + m - 1
        second = (i & m) != 0
        sums.append(np.where(second, (t > r) & (t <= i), (t > i) & (t <= r)))
        masks.append((i // (2 * m) == t // (2 * m)) & second & ((t & m) == 0))
        m //= 2
    sums.append(t > i)
    s = np.concatenate(sums, axis=0).astype(np.float32)
    return (jnp.asarray(np.concatenate([s, s, s], axis=1), BF16),
            jnp.asarray(np.stack(masks).astype(np.float32)))


def _lin_attn_chunk(q, k, v_bf, g, st_ref, sums_ref, mask_ref, nh):
    L, wk = q.shape
    dk = wk // nh
    dv = v_bf.shape[1] // nh
    n_lvl = mask_ref.shape[0] - 1
    hi = g.astype(BF16)
    r1 = g - hi.astype(F32)
    mid = r1.astype(BF16)
    lo = (r1 - mid.astype(F32)).astype(BF16)
    p = jnp.exp(_dot(sums_ref[...], jnp.concatenate([hi, mid, lo], axis=0)))
    p_last = p[L - 1:L, :]
    qd = (q * p[0:L]).astype(BF16)
    kd = (k * p[(n_lvl + 1) * L:(n_lvl + 2) * L]).astype(BF16)
    q_bf = q.astype(BF16)
    k_bf = k.astype(BF16)
    rowi = lax.broadcasted_iota(jnp.int32, (L, 1), 0)
    xs = []
    for lvl in range(n_lvl):
        second = (rowi & (L >> (lvl + 1))) != 0
        xs.append((jnp.where(second, q, k) * p[(lvl + 1) * L:(lvl + 2) * L]).astype(BF16))
    outs = []
    for h in range(nh):
        ck = slice(h * dk, (h + 1) * dk)
        cv = slice(h * dv, (h + 1) * dv)
        a = _dot_nt(q_bf[:, ck], k_bf[:, ck]) * mask_ref[0]
        for lvl in range(n_lvl):
            x = xs[lvl][:, ck]
            a = a + _dot_nt(x, x) * mask_ref[lvl + 1]
        st = st_ref[h]
        outs.append(_dot_nt(qd[:, ck], st.astype(BF16)) + _dot(a.astype(BF16), v_bf[:, cv]))
        st_ref[h] = st * p_last[:, ck] + _dot_tn(v_bf[:, cv], kd[:, ck])
    return outs


def _rms_gate(o, norm_g, gate, eps=1e-6):
    ms = jnp.mean(o * o, axis=-1, keepdims=True)
    return o * lax.rsqrt(ms + eps) * norm_g * _silu(gate)


def _gla_kernel(q_ref, k_ref, v_ref, gg_ref, ga_ref, wa_ref, ba_ref, ng_ref, sums_ref, mask_ref, o_ref,
                st_ref, *, n_chunks, q_scale):
    @pl.when(pl.program_id(2) == 0)
    def _():
        st_ref[...] = jnp.zeros_like(st_ref)

    def chunk(c, carry):
        rows = pl.ds(pl.multiple_of(c * LIN_CHUNK, LIN_CHUNK), LIN_CHUNK)
        q = q_ref[rows, :].astype(F32) * q_scale
        k = k_ref[rows, :].astype(F32)
        a_pre = _dot(ga_ref[rows, :].astype(BF16), wa_ref[...]) + ba_ref[...]
        g = _log_sigmoid(a_pre) * (1.0 / GLA_TAU)
        (o,) = _lin_attn_chunk(q, k, v_ref[rows, :], g, st_ref, sums_ref, mask_ref, 1)
        o_ref[rows, :] = _rms_gate(o, ng_ref[...], gg_ref[rows, :].astype(F32)).astype(o_ref.dtype)
        return carry

    lax.fori_loop(0, n_chunks, chunk, 0, unroll=2)


def _hgrn_kernel(q_ref, f_ref, v_ref, hg_ref, lb_ref, ng_ref, sums_ref, mask_ref, o_ref,
                 st_ref, *, n_chunks, nh):
    dv = ng_ref.shape[1]

    @pl.when(pl.program_id(2) == 0)
    def _():
        st_ref[...] = jnp.zeros_like(st_ref)

    def chunk(c, carry):
        rows = pl.ds(pl.multiple_of(c * LIN_CHUNK, LIN_CHUNK), LIN_CHUNK)
        lb = lb_ref[...]
        f = lb + (1.0 - lb) * _sigmoid(f_ref[rows, :])
        q = _silu(q_ref[rows, :].astype(F32))
        outs = _lin_attn_chunk(q, 1.0 - f, v_ref[rows, :], jnp.log(f), st_ref, sums_ref, mask_ref, nh)
        gate = hg_ref[rows, :].astype(F32)
        for h, o in enumerate(outs):
            cv = slice(h * dv, (h + 1) * dv)
            o_ref[rows, cv] = _rms_gate(o, ng_ref[...], gate[:, cv]).astype(o_ref.dtype)
        return carry

    lax.fori_loop(0, n_chunks, chunk, 0, unroll=2)


def _col_spec(tb, width, col0):
    assert col0 % width == 0, (col0, width)
    base = col0 // width
    return pl.BlockSpec((None, tb, width), lambda b, h, t: (b, t, base + h))


def _const_specs(*arrays):
    return [pl.BlockSpec(a.shape, lambda b, h, t, nd=a.ndim: (0,) * nd) for a in arrays]


def _gla_call(h_bf, h_f, wa_pad, b_a, norm_g, *, offs, ga_col, dk, dv, tb):
    B, T, _ = h_bf.shape
    tb = min(tb, T)
    ga_w = wa_pad.shape[0]
    assert ga_col % ga_w == 0
    consts = _lin_constants(LIN_CHUNK)
    kern = functools.partial(_gla_kernel, n_chunks=tb // LIN_CHUNK, q_scale=dk ** -0.5)
    return pl.pallas_call(
        kern,
        grid=(B, GLA_HEADS, T // tb),
        in_specs=[_col_spec(tb, dk, offs["gq"]), _col_spec(tb, dk, offs["gk"]),
                  _col_spec(tb, dv, offs["gv"]), _col_spec(tb, dv, offs["gg"]),
                  pl.BlockSpec((None, tb, ga_w), lambda b, h, t: (b, t, ga_col // ga_w)),
                  pl.BlockSpec((ga_w, dk), lambda b, h, t: (0, h)),
                  pl.BlockSpec((1, dk), lambda b, h, t: (0, h)),
                  pl.BlockSpec((1, dv), lambda b, h, t: (0, 0))] + _const_specs(*consts),
        out_specs=pl.BlockSpec((None, tb, dv), lambda b, h, t: (b, t, h)),
        out_shape=jax.ShapeDtypeStruct((B, T, GLA_HEADS * dv), BF16),
        scratch_shapes=[pltpu.VMEM((1, dv, dk), F32)],
        compiler_params=_params("parallel", "parallel", "arbitrary"),
        name="gla_chunks",
    )(h_bf, h_bf, h_bf, h_bf, h_f, wa_pad, b_a, norm_g, *consts)


def _hgrn_call(h_bf, h_f, lb, norm_g, *, offs, heads, dk, dv, tb, nh=4):
    B, T, _ = h_bf.shape
    tb = min(tb, T)
    nh = min(nh, heads)
    assert heads % nh == 0
    consts = _lin_constants(LIN_CHUNK)
    kern = functools.partial(_hgrn_kernel, n_chunks=tb // LIN_CHUNK, nh=nh)
    return pl.pallas_call(
        kern,
        grid=(B, heads // nh, T // tb),
        in_specs=[_col_spec(tb, nh * dk, offs["hq"]),
                  pl.BlockSpec((None, tb, nh * dk), lambda b, h, t: (b, t, h)),
                  _col_spec(tb, nh * dv, offs["hi"]), _col_spec(tb, nh * dv, offs["hg"]),
                  pl.BlockSpec((1, nh * dk), lambda b, h, t: (0, h)),
                  pl.BlockSpec((1, dv), lambda b, h, t: (0, 0))] + _const_specs(*consts),
        out_specs=pl.BlockSpec((None, tb, nh * dv), lambda b, h, t: (b, t, h)),
        out_shape=jax.ShapeDtypeStruct((B, T, heads * dv), BF16),
        scratch_shapes=[pltpu.VMEM((nh, dv, dk), F32)],
        compiler_params=_params("parallel", "parallel", "arbitrary"),
        name="hgrn_chunks",
    )(h_bf, h_f, h_bf, h_bf, lb, norm_g, *consts)


def _memattn_kernel(q_ref, g_ref, k_ref, v_ref, o_ref, *, dh):
    scale = dh ** -0.5
    for h in range(MEM_HEADS):
        cs = slice(h * dh, (h + 1) * dh)
        s = _dot_nt(q_ref[:, cs], k_ref[:, cs]) * scale
        s = s - jnp.max(s, axis=-1, keepdims=True)
        p = jnp.exp(s)
        p = p / jnp.sum(p, axis=-1, keepdims=True)
        o = _dot(p.astype(BF16), v_ref[:, cs])
        o_ref[:, cs] = (o * _silu(g_ref[:, cs].astype(F32))).astype(o_ref.dtype)


def _memattn_call(h_bf, k_mem, v_mem, *, q_col, g_col, tm=512):
    B, T, _ = h_bf.shape
    _, ml, mw = k_mem.shape
    tm = min(tm, T)
    assert q_col % mw == 0 and g_col % mw == 0
    return pl.pallas_call(
        functools.partial(_memattn_kernel, dh=mw // MEM_HEADS),
        grid=(B, T // tm),
        in_specs=[pl.BlockSpec((None, tm, mw), lambda b, t: (b, t, q_col // mw)),
                  pl.BlockSpec((None, tm, mw), lambda b, t: (b, t, g_col // mw)),
                  pl.BlockSpec((None, ml, mw), lambda b, t: (b, 0, 0)),
                  pl.BlockSpec((None, ml, mw), lambda b, t: (b, 0, 0))],
        out_specs=pl.BlockSpec((None, tm, mw), lambda b, t: (b, t, 0)),
        out_shape=jax.ShapeDtypeStruct((B, T, mw), BF16),
        compiler_params=_params("parallel", "parallel"),
        name="mem_attention",
    )(h_bf, h_bf, k_mem, v_mem)


def _outproj_kernel(*refs, segs, nk, alpha, eps):
    ns = len(segs)
    a_refs = refs[:ns]
    w_ref, x_ref, g_ref, b_ref, o_ref = refs[ns:]
    k = pl.program_id(1)

    @pl.when(k == 0)
    def _():
        o_ref[...] = jnp.zeros_like(o_ref)

    for a_ref, (start, n) in zip(a_refs, segs):
        @pl.when((k >= start) & (k < start + n))
        def _(a_ref=a_ref):
            o_ref[...] += _dot(a_ref[...], w_ref[...])

    @pl.when(k == nk - 1)
    def _():
        y = alpha * x_ref[...] + o_ref[...]
        mu = jnp.mean(y, axis=-1, keepdims=True)
        yc = y - mu
        var = jnp.mean(yc * yc, axis=-1, keepdims=True)
        o_ref[...] = yc * lax.rsqrt(var + eps) * g_ref[...] + b_ref[...]


def _outproj_ln(acts, w_bf, x2d, ln_g, ln_b, alpha, tm=512, tk=512):
    m, d = x2d.shape
    tm = min(tm, m)
    segs, start = [], 0
    for a in acts:
        assert a.shape[0] == m and a.shape[1] % tk == 0
        segs.append((start, a.shape[1] // tk))
        start += a.shape[1] // tk
    nk = start
    assert nk * tk == w_bf.shape[0]

    def a_spec(s0, n):
        return pl.BlockSpec((tm, tk), lambda i, k: (i, jnp.clip(k - s0, 0, n - 1)))

    kern = functools.partial(_outproj_kernel, segs=tuple(segs), nk=nk, alpha=alpha, eps=1e-5)
    return pl.pallas_call(
        kern,
        grid=(m // tm, nk),
        in_specs=[a_spec(s0, n) for s0, n in segs] + [
            pl.BlockSpec((tk, d), lambda i, k: (k, 0)),
            pl.BlockSpec((tm, d), lambda i, k: (i, 0)),
            pl.BlockSpec((1, d), lambda i, k: (0, 0)),
            pl.BlockSpec((1, d), lambda i, k: (0, 0))],
        out_specs=pl.BlockSpec((tm, d), lambda i, k: (i, 0)),
        out_shape=jax.ShapeDtypeStruct((m, d), F32),
        compiler_params=_params("parallel", "arbitrary"),
        name="outproj_layernorm",
    )(*acts, w_bf, x2d, ln_g, ln_b)


def _mlstm_pre_kernel(x_ref, halo_ref, cw_ref, cb_ref, wq_ref, wk_ref, wkt_ref, wv_ref,
                      iq_ref, ik_ref, iv_ref, bif_ref,
                      xc_ref, q_ref, k_ref, kt_ref, v_ref, g_ref, *, n_sub, bw, n_taps):
    t = pl.program_id(1)
    c = pl.program_id(2)
    tm = x_ref.shape[0]

    @pl.when(c == 0)
    def _():
        g_ref[...] = jnp.broadcast_to(bif_ref[...], g_ref.shape)

    gates = jnp.zeros(g_ref.shape, F32)
    for j in range(n_sub):
        cs = slice(j * bw, (j + 1) * bw)
        x_bf = x_ref[:, cs]
        x32 = x_bf.astype(F32)
        halo = jnp.where(t > 0, halo_ref[:, cs].astype(F32), 0.0)
        ext = jnp.concatenate([halo, x32], axis=0)
        acc = jnp.broadcast_to(cb_ref[:, cs], (tm, bw))
        for tap in range(n_taps):
            sh = 8 - (n_taps - 1) + tap
            acc = acc + ext[sh:sh + tm, :] * cw_ref[tap:tap + 1, cs]
        xc_bf = _silu(acc).astype(BF16)
        xc_ref[:, cs] = xc_bf
        q_bf = _dot(xc_bf, wq_ref[j]).astype(BF16)
        k_bf = _dot(xc_bf, wk_ref[j]).astype(BF16)
        v_bf = _dot(x_bf, wv_ref[j]).astype(BF16)
        q_ref[:, cs] = q_bf
        k_ref[:, cs] = k_bf
        v_ref[:, cs] = v_bf
        kt_ref[cs, :] = _dot_nt(wkt_ref[j], xc_bf).astype(BF16)
        gates = gates + _dot(q_bf, iq_ref[cs, :]) + _dot(k_bf, ik_ref[cs, :]) + _dot(v_bf, iv_ref[cs, :])
    g_ref[...] += gates


def _block_diag(w, bw):
    nb, bi, bo = w.shape
    assert bi == bo and bw % bi == 0 and (nb * bi) % bw == 0
    per = bw // bi
    w4 = w.reshape(nb // per, per, bi, bo)
    eye = jnp.eye(per, dtype=w.dtype)
    return jnp.einsum("cnio,nm->cnimo", w4, eye).reshape(nb // per, bw, bw)


def _mlstm_pre_call(h_bf, conv_w, conv_b, w_q, w_k, w_v, w_if, b_if, *, width, tm=512, tc=1024, bw=256):
    B, T, _ = h_bf.shape
    tm, tc = min(tm, T), min(tc, width)
    n_taps = conv_w.shape[0]
    assert tm % 8 == 0 and n_taps <= 8 and width % tc == 0 and tc % bw == 0
    n_sub = tc // bw
    wq = _block_diag(w_q, bw).astype(BF16)
    wk = _block_diag(w_k, bw)
    wkt = jnp.swapaxes(wk, 1, 2).astype(BF16)
    wk = wk.astype(BF16)
    wv = _block_diag(w_v, bw).astype(BF16)
    n_gate = w_if.shape[1]
    wif = jnp.pad(w_if, ((0, 0), (0, 128 - n_gate))).astype(BF16)
    bif = jnp.pad(b_if.astype(F32), (0, 128 - n_gate)).reshape(1, 128)
    hb = tm // 8
    tile_spec = pl.BlockSpec((n_sub, bw, bw), lambda b, t, c: (c, 0, 0))
    act_spec = pl.BlockSpec((None, tm, tc), lambda b, t, c: (b, t, c))
    kern = functools.partial(_mlstm_pre_kernel, n_sub=n_sub, bw=bw, n_taps=n_taps)
    shp = jax.ShapeDtypeStruct((B, T, width), BF16)
    return pl.pallas_call(
        kern,
        grid=(B, T // tm, width // tc),
        in_specs=[act_spec,
                  pl.BlockSpec((None, 8, tc), lambda b, t, c: (b, jnp.maximum(t * hb - 1, 0), c)),
                  pl.BlockSpec((n_taps, tc), lambda b, t, c: (0, c)),
                  pl.BlockSpec((1, tc), lambda b, t, c: (0, c)),
                  tile_spec, tile_spec, tile_spec, tile_spec,
                  pl.BlockSpec((tc, 128), lambda b, t, c: (c, 0)),
                  pl.BlockSpec((tc, 128), lambda b, t, c: (width // tc + c, 0)),
                  pl.BlockSpec((tc, 128), lambda b, t, c: (2 * (width // tc) + c, 0)),
                  pl.BlockSpec((1, 128), lambda b, t, c: (0, 0))],
        out_specs=[act_spec, act_spec, act_spec,
                   pl.BlockSpec((None, tc, tm), lambda b, t, c: (b, c, t)),
                   act_spec,
                   pl.BlockSpec((None, tm, 128), lambda b, t, c: (b, t, 0))],
        out_shape=[shp, shp, shp, jax.ShapeDtypeStruct((B, width, T), BF16), shp,
                   jax.ShapeDtypeStruct((B, T, 128), F32)],
        compiler_params=_params("parallel", "parallel", "arbitrary"),
        name="mlstm_frontend",
    )(h_bf, h_bf, conv_w, conv_b.reshape(1, width), wq, wk, wkt, wv, wif, wif, wif, bif)


def _mlstm_kernel(q_ref, k_ref, kt_ref, v_ref, gc_ref, gr_ref, xc_ref, z_ref, ng_ref, sk_ref, o_ref,
                  c_ref, cb_ref, n_ref, m_ref, *, n_chunks, heads, k_scale, dvb):
    h = pl.program_id(1)
    L = MLSTM_CHUNK
    dk, dv = c_ref.shape

    @pl.when(pl.program_id(2) == 0)
    def _():
        c_ref[...] = jnp.zeros_like(c_ref)
        cb_ref[...] = jnp.zeros_like(cb_ref)
        n_ref[...] = jnp.zeros_like(n_ref)
        m_ref[...] = jnp.zeros_like(m_ref)

    row = lax.broadcasted_iota(jnp.int32, (L, L), 0)
    col = lax.broadcasted_iota(jnp.int32, (L, L), 1)
    causal = row >= col
    glane = lax.broadcasted_iota(jnp.int32, (L, gc_ref.shape[1]), 1)

    for c in range(n_chunks):
        rs = slice(c * L, (c + 1) * L)
        qc = q_ref[rs, :]
        kc = k_ref[rs, :]
        vc = v_ref[rs, :]
        gcol = gc_ref[rs, :]
        i_col = jnp.sum(jnp.where(glane == h, gcol, 0.0), axis=1, keepdims=True)
        f_col = jnp.sum(jnp.where(glane == heads + h, gcol, 0.0), axis=1, keepdims=True)
        i_row = gr_ref[0:1, rs]
        f_row = gr_ref[1:2, rs]
        lf_col = _log_sigmoid(f_col)
        lf_row = _log_sigmoid(f_row)
        b_col = jnp.sum(jnp.where(causal, lf_row, 0.0), axis=1, keepdims=True)
        b_row = jnp.sum(jnp.where(row <= col, lf_col, 0.0), axis=0, keepdims=True)
        b_last = b_col[L - 1:L, :]
        m_prev = m_ref[0:1, 0:1]

        d = jnp.where(causal, b_col - b_row + i_row, NEG_BIG)
        inter_log = b_col + m_prev
        m_i = jnp.maximum(inter_log, jnp.max(d, axis=1, keepdims=True))
        w_intra = jnp.exp(d - m_i)
        w_inter = jnp.exp(inter_log - m_i) * k_scale
        scores = _dot_nt(qc, kc) * k_scale * w_intra
        n_row = n_ref[...]
        den = (jnp.sum(scores, axis=1, keepdims=True)
               + w_inter * jnp.sum(qc.astype(F32) * n_row, axis=1, keepdims=True))
        inv = 1.0 / jnp.maximum(jnp.abs(den), jnp.exp(-m_i))
        num = _dot(scores.astype(BF16), vc) + w_inter * _dot(qc, cb_ref[...])
        hh = num * inv

        log_wj_row = b_last - b_row + i_row
        log_wj_col = b_last - b_col + i_col
        m_new = jnp.maximum(b_last + m_prev, jnp.max(log_wj_row, axis=1, keepdims=True))
        wj_row = jnp.exp(log_wj_row - m_new)
        wj_col = jnp.exp(log_wj_col - m_new)
        dec = jnp.exp(b_last + m_prev - m_new)
        ktw = (kt_ref[:, rs].astype(F32) * wj_row).astype(BF16)
        for j in range(dv // dvb):
            cs = slice(j * dvb, (j + 1) * dvb)
            c_new = c_ref[:, cs] * dec + _dot(ktw, vc[:, cs])
            c_ref[:, cs] = c_new
            cb_ref[:, cs] = c_new.astype(BF16)
        n_ref[...] = n_row * dec + jnp.sum(kc.astype(F32) * wj_col, axis=0, keepdims=True)
        m_ref[...] = jnp.broadcast_to(m_new, m_ref.shape)

        mu = jnp.mean(hh, axis=-1, keepdims=True)
        hc = hh - mu
        var = jnp.mean(hc * hc, axis=-1, keepdims=True)
        hn = hc * lax.rsqrt(var + 1e-6) * ng_ref[...]
        out = (hn + sk_ref[...] * xc_ref[rs, :].astype(F32)) * _silu(z_ref[rs, :].astype(F32))
        o_ref[rs, :] = out.astype(o_ref.dtype)


def _mlstm_call(q, k, kt, v, gates, gates_rows, xc, h_bf, norm_g, skip, *, heads, z_col, tb=256):
    B, T, width = q.shape
    dh = width // heads
    tb = min(tb, T)
    assert tb % MLSTM_CHUNK == 0 and z_col % dh == 0
    dvb = min(512, dh)
    hs = pl.BlockSpec((None, tb, dh), lambda b, h, t: (b, t, h))
    kern = functools.partial(_mlstm_kernel, n_chunks=tb // MLSTM_CHUNK, heads=heads,
                             k_scale=dh ** -0.5, dvb=dvb)
    return pl.pallas_call(
        kern,
        grid=(B, heads, T // tb),
        in_specs=[hs, hs,
                  pl.BlockSpec((None, dh, tb), lambda b, h, t: (b, h, t)),
                  hs,
                  pl.BlockSpec((None, tb, gates.shape[2]), lambda b, h, t: (b, t, 0)),
                  pl.BlockSpec((None, None, 2, tb), lambda b, h, t: (b, h, 0, t)),
                  hs,
                  pl.BlockSpec((None, tb, dh), lambda b, h, t: (b, t, z_col // dh + h)),
                  pl.BlockSpec((1, dh), lambda b, h, t: (0, h)),
                  pl.BlockSpec((1, dh), lambda b, h, t: (0, h))],
        out_specs=hs,
        out_shape=jax.ShapeDtypeStruct((B, T, width), BF16),
        scratch_shapes=[pltpu.VMEM((dh, dh), F32), pltpu.VMEM((dh, dh), BF16),
                        pltpu.VMEM((1, dh), F32), pltpu.VMEM((8, 128), F32)],
        compiler_params=_params("parallel", "parallel", "arbitrary"),
        name="mlstm_chunks",
    )(q, k, kt, v, gates, gates_rows, xc, h_bf, norm_g, skip)


def _mem_kv(mem_bf, w_k, w_v, B):
    mw = w_k.shape[1]
    k = _matmul(mem_bf, w_k.astype(BF16), BF16).reshape(B, -1, mw)
    v = _matmul(mem_bf, w_v.astype(BF16), BF16).reshape(B, -1, mw)
    return k, v


def _even_layer(x, mem_bf, lb, w_in, gla_w_a2, gla_b_a, gla_norm_g, hgrn_norm_g,
                mem_w_k, mem_w_v, w_out, ln_g, ln_b, alpha):
    B, T, D = x.shape
    lowrank, kw = gla_w_a2.shape
    gdv = gla_norm_g.shape[0]
    gw = GLA_HEADS * gdv
    gdk = kw // GLA_HEADS
    fw = lb.shape[0]
    hheads = fw // HGRN_EXPAND
    hdv = hgrn_norm_g.shape[0]
    hw = hheads * hdv
    mw = mem_w_k.shape[1]
    sizes = (kw, kw, gw, gw, lowrank, fw, fw, hw, hw, mw, mw)
    assert sum(sizes) == w_in.shape[1]
    bounds = [0]
    for s in sizes:
        bounds.append(bounds[-1] + s)
    gq, gk, gv, gg, ga, hq, hf, hi, hg, mq, mg = (w_in[:, a:b] for a, b in zip(bounds[:-1], bounds[1:]))

    names = ("gq", "gk", "gv", "gg", "hq", "hi", "hg", "mq", "mg")
    parts = (gq, gk, gv, gg, hq, hi, hg, mq, mg)
    offs, o = {}, 0
    for nme, p in zip(names, parts):
        offs[nme] = o
        o += p.shape[1]
    w_bf = jnp.concatenate(parts, axis=1).astype(BF16)
    ga_w = 256
    w_f = jnp.concatenate([hf, jnp.pad(ga, ((0, 0), (0, ga_w - lowrank)))], axis=1).astype(BF16)

    x_bf = x.reshape(B * T, D).astype(BF16)
    h_bf = _matmul(x_bf, w_bf, BF16).reshape(B, T, -1)
    h_f = _matmul(x_bf, w_f, F32, tn=256).reshape(B, T, -1)

    wa_pad = jnp.pad(gla_w_a2, ((0, ga_w - lowrank), (0, 0))).astype(BF16)
    gla_out = _gla_call(h_bf, h_f, wa_pad, gla_b_a.reshape(1, kw), gla_norm_g.reshape(1, gdv),
                        offs=offs, ga_col=fw, dk=gdk, dv=gdv, tb=256)
    hgrn_out = _hgrn_call(h_bf, h_f, lb.reshape(1, fw), hgrn_norm_g.reshape(1, hdv),
                          offs=offs, heads=hheads, dk=HGRN_EXPAND, dv=hdv, tb=512)
    k_mem, v_mem = _mem_kv(mem_bf, mem_w_k, mem_w_v, B)
    mem_out = _memattn_call(h_bf, k_mem, v_mem, q_col=offs["mq"], g_col=offs["mg"])

    acts = [a.reshape(B * T, -1) for a in (gla_out, hgrn_out, mem_out)]
    y = _outproj_ln(acts, w_out.astype(BF16), x.reshape(B * T, D), ln_g.reshape(1, D), ln_b.reshape(1, D), alpha)
    return y.reshape(B, T, D)


def _odd_layer(x, mem_bf, w_in, conv_w, conv_b, w_q, w_k, w_v, w_if, b_if, mh_norm_g, skip,
               mem_w_k, mem_w_v, w_out, ln_g, ln_b, alpha):
    B, T, D = x.shape
    width = conv_w.shape[1]
    heads = b_if.shape[0] // 2
    mw = mem_w_k.shape[1]
    assert w_in.shape[1] == 2 * width + 2 * mw

    x_bf = x.reshape(B * T, D).astype(BF16)
    h_bf = _matmul(x_bf, w_in.astype(BF16), BF16).reshape(B, T, -1)

    xc, q, k, kt, v, gates = _mlstm_pre_call(h_bf, conv_w, conv_b, w_q, w_k, w_v, w_if, b_if, width=width)
    gates_rows = jnp.transpose(gates[:, :, :2 * heads].reshape(B, T, 2, heads), (0, 3, 2, 1))
    mlstm_out = _mlstm_call(q, k, kt, v, gates, gates_rows, xc, h_bf,
                            mh_norm_g.reshape(1, width), skip.reshape(1, width), heads=heads, z_col=width)
    k_mem, v_mem = _mem_kv(mem_bf, mem_w_k, mem_w_v, B)
    mem_out = _memattn_call(h_bf, k_mem, v_mem, q_col=2 * width, g_col=2 * width + mw)

    acts = [a.reshape(B * T, -1) for a in (mlstm_out, mem_out)]
    y = _outproj_ln(acts, w_out.astype(BF16), x.reshape(B * T, D), ln_g.reshape(1, D), ln_b.reshape(1, D), alpha)
    return y.reshape(B, T, D)


def kernel(x, mem, hgrn_lb_logits, ev_w_in, ev_gla_w_a2, ev_gla_b_a, ev_gla_norm_g, ev_hgrn_norm_g, ev_mem_w_k, ev_mem_w_v, ev_w_out, ev_ln_g, ev_ln_b, od_w_in, od_conv_w, od_conv_b, od_w_q, od_w_k, od_w_v, od_w_if, od_b_if, od_mh_norm_g, od_skip, od_mem_w_k, od_mem_w_v, od_w_out, od_ln_g, od_ln_b):
    depth = ev_w_in.shape[0] + od_w_in.shape[0]
    alpha = (2 * depth) ** 0.25
    B, ml, D = mem.shape
    mem_bf = mem.reshape(B * ml, D).astype(BF16)
    lb_all = jnp.cumsum(jax.nn.softmax(hgrn_lb_logits.astype(F32), axis=0), axis=0)
    for layer in range(depth):
        i = layer // 2
        if layer % 2 == 0:
            x = _even_layer(x, mem_bf, lb_all[layer], ev_w_in[i], ev_gla_w_a2[i], ev_gla_b_a[i],
                            ev_gla_norm_g[i], ev_hgrn_norm_g[i], ev_mem_w_k[i], ev_mem_w_v[i],
                            ev_w_out[i], ev_ln_g[i], ev_ln_b[i], alpha)
        else:
            x = _odd_layer(x, mem_bf, od_w_in[i], od_conv_w[i], od_conv_b[i], od_w_q[i], od_w_k[i],
                           od_w_v[i], od_w_if[i], od_b_if[i], od_mh_norm_g[i], od_skip[i],
                           od_mem_w_k[i], od_mem_w_v[i], od_w_out[i], od_ln_g[i], od_ln_b[i], alpha)
    return x
```

```python
import functools

import jax
import jax.numpy as jnp
import numpy as np
from jax import lax
from jax.experimental import pallas as pl
from jax.experimental.pallas import tpu as pltpu

F32 = jnp.float32
BF16 = jnp.bfloat16

GLA_HEADS = 4
GLA_TAU = 16.0
HGRN_EXPAND = 128
MEM_HEADS = 4

LIN_CHUNK = 128
MLSTM_CHUNK = 256

V7X_VMEM_LIMIT_BYTES = 56 * 1024 * 1024
NEG_BIG = -1e30
LOG2_E = 1.4426950408889634


def _sigmoid(x):
    return 1.0 / (1.0 + jnp.exp(-x))


def _silu(x):
    return x * _sigmoid(x)


def _log_sigmoid(x):
    return jnp.minimum(x, 0.0) - jnp.log(1.0 + jnp.exp(-jnp.abs(x)))


def _dot(a, b):
    return jnp.dot(a, b, preferred_element_type=F32)


def _dot_nt(a, b):
    return lax.dot_general(a, b, (((1,), (1,)), ((), ())), preferred_element_type=F32)


def _dot_tn(a, b):
    return lax.dot_general(a, b, (((0,), (0,)), ((), ())), preferred_element_type=F32)


def _params(*sem):
    return pltpu.CompilerParams(dimension_semantics=sem, vmem_limit_bytes=V7X_VMEM_LIMIT_BYTES)


def _mm_kernel(a_ref, w_ref, o_ref):
    o_ref[...] = _dot(a_ref[...], w_ref[...]).astype(o_ref.dtype)


def _mm_cast_kernel(a_ref, w_ref, o_ref, abf_ref):
    @pl.when(pl.program_id(1) == 0)
    def _():
        abf_ref[...] = a_ref[...].astype(abf_ref.dtype)

    o_ref[...] = _dot(abf_ref[...], w_ref[...]).astype(o_ref.dtype)


def _matmul(a, w, out_dtype, tm=1024, tn=512, skip=None):
    m, k = a.shape
    n = w.shape[1]
    cast = a.dtype != BF16
    tm, tn = min(512 if cast else tm, m), min(tn, n)
    s0, ns = (0, 0) if skip is None else (skip[0] // tn, skip[1] // tn)
    assert m % tm == 0 and n % tn == 0 and (skip is None or (skip[0] % tn == 0 and skip[1] % tn == 0))
    n_out = n - ns * tn
    w_spec = pl.BlockSpec((k, tn), lambda i, j: (0, jnp.where(j < s0, j, j + ns)))
    out_specs = pl.BlockSpec((tm, tn), lambda i, j: (i, j))
    out_shape = jax.ShapeDtypeStruct((m, n_out), out_dtype)
    if cast:
        out_specs = [out_specs, pl.BlockSpec((tm, k), lambda i, j: (i, 0))]
        out_shape = [out_shape, jax.ShapeDtypeStruct((m, k), BF16)]
    return pl.pallas_call(
        _mm_cast_kernel if cast else _mm_kernel,
        grid=(m // tm, n_out // tn),
        in_specs=[pl.BlockSpec((tm, k), lambda i, j: (i, 0)), w_spec],
        out_specs=out_specs,
        out_shape=out_shape,
        compiler_params=_params("parallel", "arbitrary"),
        name="proj_matmul",
    )(a, w)


def _lin_constants(L):
    i = np.arange(L)[:, None]
    t = np.arange(L)[None, :]
    sums = [t <= i]
    masks = [i == t]
    m = L // 2
    while m >= 1:
        r = (i // (2 * m)) * 2 * m + m - 1
        second = (i & m) != 0
        sums.append(np.where(second, (t > r) & (t <= i), (t > i) & (t <= r)))
        masks.append((i // (2 * m) == t // (2 * m)) & second & ((t & m) == 0))
        m //= 2
    sums.append(t > i)
    s = np.concatenate(sums, axis=0).astype(np.float32)
    return (jnp.asarray(np.concatenate([s, s], axis=1), BF16),
            jnp.asarray(np.stack(masks).astype(np.float32)))


def _lin_attn_chunk(q, k, v_bf, g, st_ref, sums_ref, mask_ref, nh):
    L, wk = q.shape
    dk = wk // nh
    dv = v_bf.shape[1] // nh
    n_lvl = mask_ref.shape[0] - 1
    g2 = g * LOG2_E
    hi = g2.astype(BF16)
    lo = (g2 - hi.astype(F32)).astype(BF16)
    p = jnp.exp2(_dot(sums_ref[...], jnp.concatenate([hi, lo], axis=0)))
    p_last = p[L - 1:L, :]
    qd = (q * p[0:L]).astype(BF16)
    kd = (k * p[(n_lvl + 1) * L:(n_lvl + 2) * L]).astype(BF16)
    q_bf = q.astype(BF16)
    k_bf = k.astype(BF16)
    rowi = lax.broadcasted_iota(jnp.int32, (L, 1), 0)
    xs = []
    for lvl in range(n_lvl):
        second = (rowi & (L >> (lvl + 1))) != 0
        xs.append((jnp.where(second, q, k) * p[(lvl + 1) * L:(lvl + 2) * L]).astype(BF16))
    outs = []
    for h in range(nh):
        ck = slice(h * dk, (h + 1) * dk)
        cv = slice(h * dv, (h + 1) * dv)
        a = _dot_nt(q_bf[:, ck], k_bf[:, ck]) * mask_ref[0]
        for lvl in range(n_lvl):
            x = xs[lvl][:, ck]
            a = a + _dot_nt(x, x) * mask_ref[lvl + 1]
        st = st_ref[h]
        outs.append(_dot_nt(qd[:, ck], st.astype(BF16)) + _dot(a.astype(BF16), v_bf[:, cv]))
        st_ref[h] = st * p_last[:, ck] + _dot_tn(v_bf[:, cv], kd[:, ck])
    return outs


def _rms_gate(o, norm_g, gate, eps=1e-6):
    ms = jnp.mean(o * o, axis=-1, keepdims=True)
    return o * lax.rsqrt(ms + eps) * norm_g * _silu(gate)


def _gla_kernel(q_ref, k_ref, v_ref, gg_ref, ga_ref, wa_ref, ba_ref, ng_ref, sums_ref, mask_ref, o_ref,
                st_ref, *, n_chunks, q_scale):
    @pl.when(pl.program_id(2) == 0)
    def _():
        st_ref[...] = jnp.zeros_like(st_ref)

    def chunk(c, carry):
        rows = pl.ds(pl.multiple_of(c * LIN_CHUNK, LIN_CHUNK), LIN_CHUNK)
        q = q_ref[rows, :].astype(F32) * q_scale
        k = k_ref[rows, :].astype(F32)
        a_pre = _dot(ga_ref[rows, :].astype(BF16), wa_ref[...]) + ba_ref[...]
        g = _log_sigmoid(a_pre) * (1.0 / GLA_TAU)
        (o,) = _lin_attn_chunk(q, k, v_ref[rows, :], g, st_ref, sums_ref, mask_ref, 1)
        o_ref[rows, :] = _rms_gate(o, ng_ref[...], gg_ref[rows, :].astype(F32)).astype(o_ref.dtype)
        return carry

    lax.fori_loop(0, n_chunks, chunk, 0, unroll=2)


def _hgrn_kernel(q_ref, f_ref, v_ref, hg_ref, lb_ref, ng_ref, sums_ref, mask_ref, o_ref,
                 st_ref, *, n_chunks, nh):
    dv = ng_ref.shape[1]

    @pl.when(pl.program_id(2) == 0)
    def _():
        st_ref[...] = jnp.zeros_like(st_ref)

    def chunk(c, carry):
        rows = pl.ds(pl.multiple_of(c * LIN_CHUNK, LIN_CHUNK), LIN_CHUNK)
        lb = lb_ref[...]
        f = lb + (1.0 - lb) * _sigmoid(f_ref[rows, :])
        q = _silu(q_ref[rows, :].astype(F32))
        outs = _lin_attn_chunk(q, 1.0 - f, v_ref[rows, :], jnp.log(f), st_ref, sums_ref, mask_ref, nh)
        gate = hg_ref[rows, :].astype(F32)
        for h, o in enumerate(outs):
            cv = slice(h * dv, (h + 1) * dv)
            o_ref[rows, cv] = _rms_gate(o, ng_ref[...], gate[:, cv]).astype(o_ref.dtype)
        return carry

    lax.fori_loop(0, n_chunks, chunk, 0, unroll=2)


def _col_spec(tb, width, col0):
    assert col0 % width == 0, (col0, width)
    base = col0 // width
    return pl.BlockSpec((None, tb, width), lambda b, h, t: (b, t, base + h))


def _const_specs(*arrays):
    return [pl.BlockSpec(a.shape, lambda b, h, t, nd=a.ndim: (0,) * nd) for a in arrays]


def _gla_call(h_bf, h_f, wa_pad, b_a, norm_g, *, offs, ga_col, dk, dv, tb):
    B, T, _ = h_bf.shape
    tb = min(tb, T)
    ga_w = wa_pad.shape[0]
    assert ga_col % ga_w == 0
    consts = _lin_constants(LIN_CHUNK)
    kern = functools.partial(_gla_kernel, n_chunks=tb // LIN_CHUNK, q_scale=dk ** -0.5)
    return pl.pallas_call(
        kern,
        grid=(B, GLA_HEADS, T // tb),
        in_specs=[_col_spec(tb, dk, offs["gq"]), _col_spec(tb, dk, offs["gk"]),
                  _col_spec(tb, dv, offs["gv"]), _col_spec(tb, dv, offs["gg"]),
                  pl.BlockSpec((None, tb, ga_w), lambda b, h, t: (b, t, ga_col // ga_w)),
                  pl.BlockSpec((ga_w, dk), lambda b, h, t: (0, h)),
                  pl.BlockSpec((1, dk), lambda b, h, t: (0, h)),
                  pl.BlockSpec((1, dv), lambda b, h, t: (0, 0))] + _const_specs(*consts),
        out_specs=pl.BlockSpec((None, tb, dv), lambda b, h, t: (b, t, h)),
        out_shape=jax.ShapeDtypeStruct((B, T, GLA_HEADS * dv), BF16),
        scratch_shapes=[pltpu.VMEM((1, dv, dk), F32)],
        compiler_params=_params("parallel", "parallel", "arbitrary"),
        name="gla_chunks",
    )(h_bf, h_bf, h_bf, h_bf, h_f, wa_pad, b_a, norm_g, *consts)


def _hgrn_call(h_bf, h_f, lb, norm_g, *, offs, heads, dk, dv, tb, nh=4):
    B, T, _ = h_bf.shape
    tb = min(tb, T)
    nh = min(nh, heads)
    assert heads % nh == 0
    consts = _lin_constants(LIN_CHUNK)
    kern = functools.partial(_hgrn_kernel, n_chunks=tb // LIN_CHUNK, nh=nh)
    return pl.pallas_call(
        kern,
        grid=(B, heads // nh, T // tb),
        in_specs=[_col_spec(tb, nh * dk, offs["hq"]),
                  pl.BlockSpec((None, tb, nh * dk), lambda b, h, t: (b, t, h)),
                  _col_spec(tb, nh * dv, offs["hi"]), _col_spec(tb, nh * dv, offs["hg"]),
                  pl.BlockSpec((1, nh * dk), lambda b, h, t: (0, h)),
                  pl.BlockSpec((1, dv), lambda b, h, t: (0, 0))] + _const_specs(*consts),
        out_specs=pl.BlockSpec((None, tb, nh * dv), lambda b, h, t: (b, t, h)),
        out_shape=jax.ShapeDtypeStruct((B, T, heads * dv), BF16),
        scratch_shapes=[pltpu.VMEM((nh, dv, dk), F32)],
        compiler_params=_params("parallel", "parallel", "arbitrary"),
        name="hgrn_chunks",
    )(h_bf, h_f, h_bf, h_bf, lb, norm_g, *consts)


def _memattn_kernel(q_ref, g_ref, k_ref, v_ref, o_ref, *, dh):
    scale = dh ** -0.5
    for h in range(MEM_HEADS):
        cs = slice(h * dh, (h + 1) * dh)
        s = _dot_nt(q_ref[:, cs], k_ref[:, cs]) * scale
        s = s - jnp.max(s, axis=-1, keepdims=True)
        p = jnp.exp(s)
        p = p / jnp.sum(p, axis=-1, keepdims=True)
        o = _dot(p.astype(BF16), v_ref[:, cs])
        o_ref[:, cs] = (o * _silu(g_ref[:, cs].astype(F32))).astype(o_ref.dtype)


def _memattn_call(h_bf, k_mem, v_mem, *, q_col, g_col, tm=512):
    B, T, _ = h_bf.shape
    _, ml, mw = k_mem.shape
    tm = min(tm, T)
    assert q_col % mw == 0 and g_col % mw == 0
    return pl.pallas_call(
        functools.partial(_memattn_kernel, dh=mw // MEM_HEADS),
        grid=(B, T // tm),
        in_specs=[pl.BlockSpec((None, tm, mw), lambda b, t: (b, t, q_col // mw)),
                  pl.BlockSpec((None, tm, mw), lambda b, t: (b, t, g_col // mw)),
                  pl.BlockSpec((None, ml, mw), lambda b, t: (b, 0, 0)),
                  pl.BlockSpec((None, ml, mw), lambda b, t: (b, 0, 0))],
        out_specs=pl.BlockSpec((None, tm, mw), lambda b, t: (b, t, 0)),
        out_shape=jax.ShapeDtypeStruct((B, T, mw), BF16),
        compiler_params=_params("parallel", "parallel"),
        name="mem_attention",
    )(h_bf, h_bf, k_mem, v_mem)


def _outproj_kernel(*refs, segs, nk, alpha, eps, emit_bf16):
    ns = len(segs)
    a_refs = refs[:ns]
    if emit_bf16:
        w_ref, x_hbm, g_ref, b_ref, o_ref, obf_hbm, sem, bf_buf, bf_sem = refs[ns:]
    else:
        w_ref, x_hbm, g_ref, b_ref, o_ref, sem = refs[ns:]
    i = pl.program_id(0)
    k = pl.program_id(1)
    tm, d = o_ref.shape
    rb, cb = min(128, tm), min(512, d)
    n_rb = tm // rb

    def bf_copy(r, slot):
        row0 = pl.multiple_of(i * tm + r * rb, rb)
        return pltpu.make_async_copy(bf_buf.at[slot], obf_hbm.at[pl.ds(row0, rb), :], bf_sem.at[slot])

    @pl.when(k == 0)
    def _():
        cp = pltpu.make_async_copy(x_hbm.at[pl.ds(pl.multiple_of(i * tm, tm), tm), :], o_ref, sem)
        cp.start()
        cp.wait()
        def scale(r, carry):
            rows = pl.ds(pl.multiple_of(r * rb, rb), rb)
            o_ref[rows, :] = alpha * o_ref[rows, :]
            return carry

        lax.fori_loop(0, tm // rb, scale, 0)

    for a_ref, (start, n) in zip(a_refs, segs):
        @pl.when((k >= start) & (k < start + n))
        def _(a_ref=a_ref):
            a = a_ref[...]
            for c in range(0, d, cb):
                o_ref[:, c:c + cb] += _dot(a, w_ref[:, c:c + cb])

    @pl.when(k == nk - 1)
    def _():
        def norm(r, carry):
            rows = pl.ds(pl.multiple_of(r * rb, rb), rb)
            y = o_ref[rows, :]
            mu = jnp.mean(y, axis=-1, keepdims=True)
            yc = y - mu
            var = jnp.mean(yc * yc, axis=-1, keepdims=True)
            out = yc * lax.rsqrt(var + eps) * g_ref[...] + b_ref[...]
            o_ref[rows, :] = out
            if emit_bf16:
                slot = r % 2

                @pl.when(r >= 2)
                def _():
                    bf_copy(r - 2, slot).wait()

                bf_buf[slot] = out.astype(bf_buf.dtype)
                bf_copy(r, slot).start()
            return carry

        lax.fori_loop(0, n_rb, norm, 0)
        if emit_bf16:
            for r in range(max(n_rb - 2, 0), n_rb):
                bf_copy(r, r % 2).wait()


def _outproj_ln(acts, w_bf, x2d, ln_g, ln_b, alpha, emit_bf16, tm=1024, tk=512):
    m, d = x2d.shape
    tm = min(tm, m)
    segs, start = [], 0
    for a in acts:
        assert a.shape[0] == m and a.shape[1] % tk == 0
        segs.append((start, a.shape[1] // tk))
        start += a.shape[1] // tk
    nk = start
    assert nk * tk == w_bf.shape[0]

    def a_spec(s0, n):
        return pl.BlockSpec((tm, tk), lambda i, k: (i, jnp.clip(k - s0, 0, n - 1)))

    kern = functools.partial(_outproj_kernel, segs=tuple(segs), nk=nk, alpha=alpha, eps=1e-5,
                             emit_bf16=emit_bf16)
    rb = min(128, tm)
    out_specs = [pl.BlockSpec((tm, d), lambda i, k: (i, 0))]
    out_shape = [jax.ShapeDtypeStruct((m, d), F32)]
    scratch = [pltpu.SemaphoreType.DMA(())]
    if emit_bf16:
        out_specs.append(pl.BlockSpec(memory_space=pl.ANY))
        out_shape.append(jax.ShapeDtypeStruct((m, d), BF16))
        scratch += [pltpu.VMEM((2, rb, d), BF16), pltpu.SemaphoreType.DMA((2,))]
    outs = pl.pallas_call(
        kern,
        grid=(m // tm, nk),
        in_specs=[a_spec(s0, n) for s0, n in segs] + [
            pl.BlockSpec((tk, d), lambda i, k: (k, 0)),
            pl.BlockSpec(memory_space=pl.ANY),
            pl.BlockSpec((1, d), lambda i, k: (0, 0)),
            pl.BlockSpec((1, d), lambda i, k: (0, 0))],
        out_specs=out_specs,
        out_shape=out_shape,
        scratch_shapes=scratch,
        compiler_params=_params("arbitrary", "arbitrary"),
        name="outproj_layernorm",
    )(*acts, w_bf, x2d, ln_g, ln_b)
    return (outs[0], outs[1]) if emit_bf16 else (outs[0], None)


def _mlstm_pre_kernel(x_ref, halo_ref, cw_ref, cb_ref, wq_ref, wk_ref, wkt_ref, wv_ref,
                      iq_ref, ik_ref, iv_ref, bif_ref,
                      xc_ref, q_ref, k_ref, kt_ref, v_ref, g_ref, *, n_sub, bw, n_taps):
    t = pl.program_id(1)
    c = pl.program_id(2)
    tm = x_ref.shape[0]

    @pl.when(c == 0)
    def _():
        g_ref[...] = jnp.broadcast_to(bif_ref[...], g_ref.shape)

    gates = jnp.zeros(g_ref.shape, F32)
    for j in range(n_sub):
        cs = slice(j * bw, (j + 1) * bw)
        x_bf = x_ref[:, cs]
        x32 = x_bf.astype(F32)
        halo = jnp.where(t > 0, halo_ref[:, cs].astype(F32), 0.0)
        ext = jnp.concatenate([halo, x32], axis=0)
        acc = jnp.broadcast_to(cb_ref[:, cs], (tm, bw))
        for tap in range(n_taps):
            sh = 8 - (n_taps - 1) + tap
            acc = acc + ext[sh:sh + tm, :] * cw_ref[tap:tap + 1, cs]
        xc_bf = _silu(acc).astype(BF16)
        xc_ref[:, cs] = xc_bf
        q_bf = _dot(xc_bf, wq_ref[j]).astype(BF16)
        k_bf = _dot(xc_bf, wk_ref[j]).astype(BF16)
        v_bf = _dot(x_bf, wv_ref[j]).astype(BF16)
        q_ref[:, cs] = q_bf
        k_ref[:, cs] = k_bf
        v_ref[:, cs] = v_bf
        kt_ref[cs, :] = _dot_nt(wkt_ref[j], xc_bf).astype(BF16)
        gates = gates + _dot(q_bf, iq_ref[cs, :]) + _dot(k_bf, ik_ref[cs, :]) + _dot(v_bf, iv_ref[cs, :])
    g_ref[...] += gates


def _block_diag(w, bw):
    nb, bi, bo = w.shape
    assert bi == bo and bw % bi == 0 and (nb * bi) % bw == 0
    per = bw // bi
    w4 = w.reshape(nb // per, per, bi, bo)
    eye = jnp.eye(per, dtype=w.dtype)
    return jnp.einsum("cnio,nm->cnimo", w4, eye).reshape(nb // per, bw, bw)


def _mlstm_pre_call(h_bf, conv_w, conv_b, w_q, w_k, w_v, w_if, b_if, *, width, tm=512, tc=1024, bw=256):
    B, T, _ = h_bf.shape
    tm, tc = min(tm, T), min(tc, width)
    n_taps = conv_w.shape[0]
    assert tm % 8 == 0 and n_taps <= 8 and width % tc == 0 and tc % bw == 0
    n_sub = tc // bw
    wq = _block_diag(w_q, bw).astype(BF16)
    wk = _block_diag(w_k, bw)
    wkt = jnp.swapaxes(wk, 1, 2).astype(BF16)
    wk = wk.astype(BF16)
    wv = _block_diag(w_v, bw).astype(BF16)
    n_gate = w_if.shape[1]
    wif = jnp.pad(w_if, ((0, 0), (0, 128 - n_gate))).astype(BF16)
    bif = jnp.pad(b_if.astype(F32), (0, 128 - n_gate)).reshape(1, 128)
    hb = tm // 8
    tile_spec = pl.BlockSpec((n_sub, bw, bw), lambda b, t, c: (c, 0, 0))
    act_spec = pl.BlockSpec((None, tm, tc), lambda b, t, c: (b, t, c))
    kern = functools.partial(_mlstm_pre_kernel, n_sub=n_sub, bw=bw, n_taps=n_taps)
    shp = jax.ShapeDtypeStruct((B, T, width), BF16)
    return pl.pallas_call(
        kern,
        grid=(B, T // tm, width // tc),
        in_specs=[act_spec,
                  pl.BlockSpec((None, 8, tc), lambda b, t, c: (b, jnp.maximum(t * hb - 1, 0), c)),
                  pl.BlockSpec((n_taps, tc), lambda b, t, c: (0, c)),
                  pl.BlockSpec((1, tc), lambda b, t, c: (0, c)),
                  tile_spec, tile_spec, tile_spec, tile_spec,
                  pl.BlockSpec((tc, 128), lambda b, t, c: (c, 0)),
                  pl.BlockSpec((tc, 128), lambda b, t, c: (width // tc + c, 0)),
                  pl.BlockSpec((tc, 128), lambda b, t, c: (2 * (width // tc) + c, 0)),
                  pl.BlockSpec((1, 128), lambda b, t, c: (0, 0))],
        out_specs=[act_spec, act_spec, act_spec,
                   pl.BlockSpec((None, tc, tm), lambda b, t, c: (b, c, t)),
                   act_spec,
                   pl.BlockSpec((None, tm, 128), lambda b, t, c: (b, t, 0))],
        out_shape=[shp, shp, shp, jax.ShapeDtypeStruct((B, width, T), BF16), shp,
                   jax.ShapeDtypeStruct((B, T, 128), F32)],
        compiler_params=_params("parallel", "parallel", "arbitrary"),
        name="mlstm_frontend",
    )(h_bf, h_bf, conv_w, conv_b.reshape(1, width), wq, wk, wkt, wv, wif, wif, wif, bif)


def _mlstm_kernel(q_ref, k_ref, kt_ref, v_ref, gc_ref, gr_ref, xc_ref, z_ref, ng_ref, sk_ref, o_ref,
                  c_ref, hh_ref, n_ref, m_ref, *, n_chunks, heads, k_scale, dvb):
    h = pl.program_id(1)
    L = MLSTM_CHUNK
    dk, dv = c_ref.shape

    @pl.when(pl.program_id(2) == 0)
    def _():
        c_ref[...] = jnp.zeros_like(c_ref)
        n_ref[...] = jnp.zeros_like(n_ref)
        m_ref[...] = jnp.zeros_like(m_ref)

    row = lax.broadcasted_iota(jnp.int32, (L, L), 0)
    col = lax.broadcasted_iota(jnp.int32, (L, L), 1)
    causal = row >= col
    glane = lax.broadcasted_iota(jnp.int32, (L, gc_ref.shape[1]), 1)

    for c in range(n_chunks):
        rs = slice(c * L, (c + 1) * L)
        qc = q_ref[rs, :]
        kc = k_ref[rs, :]
        vc = v_ref[rs, :]
        gcol = gc_ref[rs, :]
        i_col = jnp.sum(jnp.where(glane == h, gcol, 0.0), axis=1, keepdims=True)
        f_col = jnp.sum(jnp.where(glane == heads + h, gcol, 0.0), axis=1, keepdims=True)
        i_row = gr_ref[0:1, rs]
        f_row = gr_ref[1:2, rs]
        lf_col = _log_sigmoid(f_col)
        lf_row = _log_sigmoid(f_row)
        b_col = jnp.sum(jnp.where(causal, lf_row, 0.0), axis=1, keepdims=True)
        b_row = jnp.sum(jnp.where(row <= col, lf_col, 0.0), axis=0, keepdims=True)
        b_last = b_col[L - 1:L, :]
        m_prev = m_ref[0:1, 0:1]

        d = jnp.where(causal, b_col - b_row + i_row, NEG_BIG)
        inter_log = b_col + m_prev
        m_i = jnp.maximum(inter_log, jnp.max(d, axis=1, keepdims=True))
        w_intra = jnp.exp(d - m_i)
        w_inter = jnp.exp(inter_log - m_i) * k_scale
        scores = _dot_nt(qc, kc) * k_scale * w_intra
        n_row = n_ref[...]
        den = (jnp.sum(scores, axis=1, keepdims=True)
               + w_inter * jnp.sum(qc.astype(F32) * n_row, axis=1, keepdims=True))
        inv = 1.0 / jnp.maximum(jnp.abs(den), jnp.exp(-m_i))
        scores_bf = scores.astype(BF16)

        log_wj_row = b_last - b_row + i_row
        log_wj_col = b_last - b_col + i_col
        m_new = jnp.maximum(b_last + m_prev, jnp.max(log_wj_row, axis=1, keepdims=True))
        wj_row = jnp.exp(log_wj_row - m_new)
        wj_col = jnp.exp(log_wj_col - m_new)
        dec = jnp.exp(b_last + m_prev - m_new)
        ktw = (kt_ref[:, rs].astype(F32) * wj_row).astype(BF16)

        for j in range(dv // dvb):
            cs = slice(j * dvb, (j + 1) * dvb)
            c_old = c_ref[:, cs]
            num = _dot(scores_bf, vc[:, cs]) + w_inter * _dot(qc, c_old.astype(BF16))
            hh_ref[:, cs] = num * inv
            c_ref[:, cs] = c_old * dec + _dot(ktw, vc[:, cs])
        n_ref[...] = n_row * dec + jnp.sum(kc.astype(F32) * wj_col, axis=0, keepdims=True)
        m_ref[...] = jnp.broadcast_to(m_new, m_ref.shape)

        hh = hh_ref[...]
        mu = jnp.mean(hh, axis=-1, keepdims=True)
        hc = hh - mu
        var = jnp.mean(hc * hc, axis=-1, keepdims=True)
        hn = hc * lax.rsqrt(var + 1e-6) * ng_ref[...]
        out = (hn + sk_ref[...] * xc_ref[rs, :].astype(F32)) * _silu(z_ref[rs, :].astype(F32))
        o_ref[rs, :] = out.astype(o_ref.dtype)


def _mlstm_call(q, k, kt, v, gates, gates_rows, xc, h_bf, norm_g, skip, *, heads, z_col, tb=256):
    B, T, width = q.shape
    dh = width // heads
    tb = min(tb, T)
    assert tb % MLSTM_CHUNK == 0 and z_col % dh == 0
    dvb = min(512, dh)
    hs = pl.BlockSpec((None, tb, dh), lambda b, h, t: (b, t, h))
    kern = functools.partial(_mlstm_kernel, n_chunks=tb // MLSTM_CHUNK, heads=heads,
                             k_scale=dh ** -0.5, dvb=dvb)
    return pl.pallas_call(
        kern,
        grid=(B, heads, T // tb),
        in_specs=[hs, hs,
                  pl.BlockSpec((None, dh, tb), lambda b, h, t: (b, h, t)),
                  hs,
                  pl.BlockSpec((None, tb, gates.shape[2]), lambda b, h, t: (b, t, 0)),
                  pl.BlockSpec((None, None, 2, tb), lambda b, h, t: (b, h, 0, t)),
                  hs,
                  pl.BlockSpec((None, tb, dh), lambda b, h, t: (b, t, z_col // dh + h)),
                  pl.BlockSpec((1, dh), lambda b, h, t: (0, h)),
                  pl.BlockSpec((1, dh), lambda b, h, t: (0, h))],
        out_specs=hs,
        out_shape=jax.ShapeDtypeStruct((B, T, width), BF16),
        scratch_shapes=[pltpu.VMEM((dh, dh), F32), pltpu.VMEM((MLSTM_CHUNK, dh), F32),
                        pltpu.VMEM((1, dh), F32), pltpu.VMEM((8, 128), F32)],
        compiler_params=_params("parallel", "parallel", "arbitrary"),
        name="mlstm_chunks",
    )(q, k, kt, v, gates, gates_rows, xc, h_bf, norm_g, skip)


def _mem_kv(mem_bf, w_k, w_v, B):
    mw = w_k.shape[1]
    k = _matmul(mem_bf, w_k.astype(BF16), BF16).reshape(B, -1, mw)
    v = _matmul(mem_bf, w_v.astype(BF16), BF16).reshape(B, -1, mw)
    return k, v


def _project(x2d, x_bf, w_bf, **kw):
    if x_bf is None:
        return _matmul(x2d, w_bf, BF16, **kw)
    return _matmul(x_bf, w_bf, BF16, **kw), x_bf


def _even_layer(x, x_bf, emit_bf16, mem_bf, lb, w_in, gla_w_a2, gla_b_a, gla_norm_g, hgrn_norm_g,
                mem_w_k, mem_w_v, w_out, ln_g, ln_b, alpha):
    B, T, D = x.shape
    lowrank, kw = gla_w_a2.shape
    gdv = gla_norm_g.shape[0]
    gw = GLA_HEADS * gdv
    gdk = kw // GLA_HEADS
    fw = lb.shape[0]
    hheads = fw // HGRN_EXPAND
    hdv = hgrn_norm_g.shape[0]
    hw = hheads * hdv
    mw = mem_w_k.shape[1]
    sizes = (kw, kw, gw, gw, lowrank, fw, fw, hw, hw, mw, mw)
    assert sum(sizes) == w_in.shape[1]
    bounds = [0]
    for s in sizes:
        bounds.append(bounds[-1] + s)
    ga, hf = w_in[:, bounds[4]:bounds[5]], w_in[:, bounds[6]:bounds[7]]

    ga_w = 256
    w_a = w_in[:, :bounds[4]].astype(BF16)
    w_b = w_in[:, bounds[5]:].astype(BF16)
    w_f = jnp.concatenate([hf, jnp.pad(ga, ((0, 0), (0, ga_w - lowrank)))], axis=1).astype(BF16)
    offs_a = {"gq": 0, "gk": kw, "gv": 2 * kw, "gg": 2 * kw + gw}
    offs_b = {"hq": 0, "hi": fw, "hg": fw + hw, "mq": fw + 2 * hw, "mg": fw + 2 * hw + mw}

    x2d = x.reshape(B * T, D)
    h_a, x_bf = _project(x2d, x_bf, w_a)
    h_b = _matmul(x_bf, w_b, BF16, skip=(fw, fw))
    h_f = _matmul(x_bf, w_f, F32, tn=256)
    h_a, h_b, h_f = (h.reshape(B, T, -1) for h in (h_a, h_b, h_f))

    wa_pad = jnp.pad(gla_w_a2, ((0, ga_w - lowrank), (0, 0))).astype(BF16)
    gla_out = _gla_call(h_a, h_f, wa_pad, gla_b_a.reshape(1, kw), gla_norm_g.reshape(1, gdv),
                        offs=offs_a, ga_col=fw, dk=gdk, dv=gdv, tb=256)
    hgrn_out = _hgrn_call(h_b, h_f, lb.reshape(1, fw), hgrn_norm_g.reshape(1, hdv),
                          offs=offs_b, heads=hheads, dk=HGRN_EXPAND, dv=hdv, tb=512)
    k_mem, v_mem = _mem_kv(mem_bf, mem_w_k, mem_w_v, B)
    mem_out = _memattn_call(h_b, k_mem, v_mem, q_col=offs_b["mq"], g_col=offs_b["mg"])

    acts = [a.reshape(B * T, -1) for a in (gla_out, hgrn_out, mem_out)]
    y, y_bf = _outproj_ln(acts, w_out.astype(BF16), x2d, ln_g.reshape(1, D), ln_b.reshape(1, D), alpha, emit_bf16)
    return y.reshape(B, T, D), y_bf


def _odd_layer(x, x_bf, emit_bf16, mem_bf, w_in, conv_w, conv_b, w_q, w_k, w_v, w_if, b_if, mh_norm_g, skip,
               mem_w_k, mem_w_v, w_out, ln_g, ln_b, alpha):
    B, T, D = x.shape
    width = conv_w.shape[1]
    heads = b_if.shape[0] // 2
    mw = mem_w_k.shape[1]
    assert w_in.shape[1] == 2 * width + 2 * mw

    x2d = x.reshape(B * T, D)
    h_bf, x_bf = _project(x2d, x_bf, w_in.astype(BF16))
    h_bf = h_bf.reshape(B, T, -1)

    xc, q, k, kt, v, gates = _mlstm_pre_call(h_bf, conv_w, conv_b, w_q, w_k, w_v, w_if, b_if, width=width)
    gates_rows = jnp.transpose(gates[:, :, :2 * heads].reshape(B, T, 2, heads), (0, 3, 2, 1))
    mlstm_out = _mlstm_call(q, k, kt, v, gates, gates_rows, xc, h_bf,
                            mh_norm_g.reshape(1, width), skip.reshape(1, width), heads=heads, z_col=width)
    k_mem, v_mem = _mem_kv(mem_bf, mem_w_k, mem_w_v, B)
    mem_out = _memattn_call(h_bf, k_mem, v_mem, q_col=2 * width, g_col=2 * width + mw)

    acts = [a.reshape(B * T, -1) for a in (mlstm_out, mem_out)]
    y, y_bf = _outproj_ln(acts, w_out.astype(BF16), x2d, ln_g.reshape(1, D), ln_b.reshape(1, D), alpha, emit_bf16)
    return y.reshape(B, T, D), y_bf


def kernel(x, mem, hgrn_lb_logits, ev_w_in, ev_gla_w_a2, ev_gla_b_a, ev_gla_norm_g, ev_hgrn_norm_g, ev_mem_w_k, ev_mem_w_v, ev_w_out, ev_ln_g, ev_ln_b, od_w_in, od_conv_w, od_conv_b, od_w_q, od_w_k, od_w_v, od_w_if, od_b_if, od_mh_norm_g, od_skip, od_mem_w_k, od_mem_w_v, od_w_out, od_ln_g, od_ln_b):
    depth = ev_w_in.shape[0] + od_w_in.shape[0]
    alpha = (2 * depth) ** 0.25
    B, ml, D = mem.shape
    mem_bf = mem.reshape(B * ml, D).astype(BF16)
    lb_all = jnp.cumsum(jax.nn.softmax(hgrn_lb_logits.astype(F32), axis=0), axis=0)
    x_bf = None
    for layer in range(depth):
        i = layer // 2
        emit_bf16 = layer + 1 < depth
        if layer % 2 == 0:
            x, x_bf = _even_layer(x, x_bf, emit_bf16, mem_bf, lb_all[layer], ev_w_in[i], ev_gla_w_a2[i],
                                  ev_gla_b_a[i], ev_gla_norm_g[i], ev_hgrn_norm_g[i], ev_mem_w_k[i],
                                  ev_mem_w_v[i], ev_w_out[i], ev_ln_g[i], ev_ln_b[i], alpha)
        else:
            x, x_bf = _odd_layer(x, x_bf, emit_bf16, mem_bf, od_w_in[i], od_conv_w[i], od_conv_b[i], od_w_q[i],
                                 od_w_k[i], od_w_v[i], od_w_if[i], od_b_if[i], od_mh_norm_g[i], od_skip[i],
                                 od_mem_w_k[i], od_mem_w_v[i], od_w_out[i], od_ln_g[i], od_ln_b[i], alpha)
    return x
```

```python
import functools

import jax
import jax.numpy as jnp
import numpy as np
from jax import lax
from jax.experimental import pallas as pl
from jax.experimental.pallas import tpu as pltpu

F32 = jnp.float32
BF16 = jnp.bfloat16

GLA_HEADS = 4
GLA_TAU = 16.0
HGRN_EXPAND = 128
MEM_HEADS = 4

LIN_CHUNK = 128
MLSTM_CHUNK = 256

V7X_VMEM_LIMIT_BYTES = 56 * 1024 * 1024
NEG_BIG = -1e30
LOG2_E = 1.4426950408889634


def _sigmoid(x):
    return 1.0 / (1.0 + jnp.exp(-x))


def _silu(x):
    return x * _sigmoid(x)


def _log_sigmoid(x):
    return jnp.minimum(x, 0.0) - jnp.log(1.0 + jnp.exp(-jnp.abs(x)))


def _dot(a, b):
    return jnp.dot(a, b, preferred_element_type=F32)


def _dot_nt(a, b):
    return lax.dot_general(a, b, (((1,), (1,)), ((), ())), preferred_element_type=F32)


def _dot_tn(a, b):
    return lax.dot_general(a, b, (((0,), (0,)), ((), ())), preferred_element_type=F32)


def _params(*sem):
    return pltpu.CompilerParams(dimension_semantics=sem, vmem_limit_bytes=V7X_VMEM_LIMIT_BYTES)


def _mm_kernel(a_ref, w_ref, o_ref):
    o_ref[...] = _dot(a_ref[...], w_ref[...]).astype(o_ref.dtype)


def _matmul(a, w, out_dtype, tm=1024, tn=1024, skip=None):
    m, k = a.shape
    n = w.shape[1]
    tm = min(tm, m)
    while any(v % tn for v in (n,) + tuple(skip or ())):
        tn //= 2
    s0, ns = (0, 0) if skip is None else (skip[0] // tn, skip[1] // tn)
    assert m % tm == 0 and tn % 128 == 0, (m, n, tm, tn)
    n_out = n - ns * tn
    return pl.pallas_call(
        _mm_kernel,
        grid=(m // tm, n_out // tn),
        in_specs=[pl.BlockSpec((tm, k), lambda i, j: (i, 0)),
                  pl.BlockSpec((k, tn), lambda i, j: (0, jnp.where(j < s0, j, j + ns)))],
        out_specs=pl.BlockSpec((tm, tn), lambda i, j: (i, j)),
        out_shape=jax.ShapeDtypeStruct((m, n_out), out_dtype),
        compiler_params=_params("parallel", "arbitrary"),
        name="proj_matmul",
    )(a, w)


def _lin_constants(L):
    i = np.arange(L)[:, None]
    t = np.arange(L)[None, :]
    sums = [t <= i]
    masks = [i == t]
    m = L // 2
    while m >= 1:
        r = (i // (2 * m)) * 2 * m + m - 1
        second = (i & m) != 0
        sums.append(np.where(second, (t > r) & (t <= i), (t > i) & (t <= r)))
        masks.append((i // (2 * m) == t // (2 * m)) & second & ((t & m) == 0))
        m //= 2
    sums.append(t > i)
    s = np.concatenate(sums, axis=0).astype(np.float32)
    return (jnp.asarray(np.concatenate([s, s], axis=1), BF16),
            jnp.asarray(np.stack(masks).astype(np.float32)))


def _lin_attn_chunk(q, k, v_bf, g, st_ref, sums_ref, mask_ref, nh):
    L, wk = q.shape
    dk = wk // nh
    dv = v_bf.shape[1] // nh
    n_lvl = mask_ref.shape[0] - 1
    g2 = g * LOG2_E
    hi = g2.astype(BF16)
    lo = (g2 - hi.astype(F32)).astype(BF16)
    p = jnp.exp2(_dot(sums_ref[...], jnp.concatenate([hi, lo], axis=0)))
    p_last = p[L - 1:L, :]
    qd = (q * p[0:L]).astype(BF16)
    kd = (k * p[(n_lvl + 1) * L:(n_lvl + 2) * L]).astype(BF16)
    q_bf = q.astype(BF16)
    k_bf = k.astype(BF16)
    rowi = lax.broadcasted_iota(jnp.int32, (L, 1), 0)
    xs = []
    for lvl in range(n_lvl):
        second = (rowi & (L >> (lvl + 1))) != 0
        xs.append((jnp.where(second, q, k) * p[(lvl + 1) * L:(lvl + 2) * L]).astype(BF16))
    outs = []
    for h in range(nh):
        ck = slice(h * dk, (h + 1) * dk)
        cv = slice(h * dv, (h + 1) * dv)
        a = _dot_nt(q_bf[:, ck], k_bf[:, ck]) * mask_ref[0]
        for lvl in range(n_lvl):
            x = xs[lvl][:, ck]
            a = a + _dot_nt(x, x) * mask_ref[lvl + 1]
        st = st_ref[h]
        outs.append(_dot_nt(qd[:, ck], st.astype(BF16)) + _dot(a.astype(BF16), v_bf[:, cv]))
        st_ref[h] = st * p_last[:, ck] + _dot_tn(v_bf[:, cv], kd[:, ck])
    return outs


def _rms_gate(o, norm_g, gate, eps=1e-6):
    ms = jnp.mean(o * o, axis=-1, keepdims=True)
    return o * lax.rsqrt(ms + eps) * norm_g * _silu(gate)


def _gla_kernel(q_ref, k_ref, v_ref, gg_ref, ga_ref, wa_ref, ba_ref, ng_ref, sums_ref, mask_ref, o_ref,
                st_ref, *, n_chunks, q_scale):
    @pl.when(pl.program_id(2) == 0)
    def _():
        st_ref[...] = jnp.zeros_like(st_ref)

    def chunk(c, carry):
        rows = pl.ds(pl.multiple_of(c * LIN_CHUNK, LIN_CHUNK), LIN_CHUNK)
        q = q_ref[rows, :].astype(F32) * q_scale
        k = k_ref[rows, :].astype(F32)
        a_pre = _dot(ga_ref[rows, :].astype(BF16), wa_ref[...]) + ba_ref[...]
        g = _log_sigmoid(a_pre) * (1.0 / GLA_TAU)
        (o,) = _lin_attn_chunk(q, k, v_ref[rows, :], g, st_ref, sums_ref, mask_ref, 1)
        o_ref[rows, :] = _rms_gate(o, ng_ref[...], gg_ref[rows, :].astype(F32)).astype(o_ref.dtype)
        return carry

    lax.fori_loop(0, n_chunks, chunk, 0, unroll=4)


def _hgrn_kernel(q_ref, f_ref, v_ref, hg_ref, lb_ref, ng_ref, sums_ref, mask_ref, o_ref,
                 st_ref, *, n_chunks, nh):
    dv = ng_ref.shape[1]

    @pl.when(pl.program_id(2) == 0)
    def _():
        st_ref[...] = jnp.zeros_like(st_ref)

    def chunk(c, carry):
        rows = pl.ds(pl.multiple_of(c * LIN_CHUNK, LIN_CHUNK), LIN_CHUNK)
        lb = lb_ref[...]
        f = lb + (1.0 - lb) * _sigmoid(f_ref[rows, :])
        q = _silu(q_ref[rows, :].astype(F32))
        outs = _lin_attn_chunk(q, 1.0 - f, v_ref[rows, :], jnp.log(f), st_ref, sums_ref, mask_ref, nh)
        gate = hg_ref[rows, :].astype(F32)
        for h, o in enumerate(outs):
            cv = slice(h * dv, (h + 1) * dv)
            o_ref[rows, cv] = _rms_gate(o, ng_ref[...], gate[:, cv]).astype(o_ref.dtype)
        return carry

    lax.fori_loop(0, n_chunks, chunk, 0, unroll=4)


def _col_spec(tb, width, col0):
    assert col0 % width == 0, (col0, width)
    base = col0 // width
    return pl.BlockSpec((None, tb, width), lambda b, h, t: (b, t, base + h))


def _const_specs(*arrays):
    return [pl.BlockSpec(a.shape, lambda b, h, t, nd=a.ndim: (0,) * nd) for a in arrays]


def _gla_call(h_bf, h_f, wa_pad, b_a, norm_g, *, offs, ga_col, dk, dv, tb):
    B, T, _ = h_bf.shape
    tb = min(tb, T)
    ga_w = wa_pad.shape[0]
    assert ga_col % ga_w == 0
    consts = _lin_constants(LIN_CHUNK)
    kern = functools.partial(_gla_kernel, n_chunks=tb // LIN_CHUNK, q_scale=dk ** -0.5)
    return pl.pallas_call(
        kern,
        grid=(B, GLA_HEADS, T // tb),
        in_specs=[_col_spec(tb, dk, offs["gq"]), _col_spec(tb, dk, offs["gk"]),
                  _col_spec(tb, dv, offs["gv"]), _col_spec(tb, dv, offs["gg"]),
                  pl.BlockSpec((None, tb, ga_w), lambda b, h, t: (b, t, ga_col // ga_w)),
                  pl.BlockSpec((ga_w, dk), lambda b, h, t: (0, h)),
                  pl.BlockSpec((1, dk), lambda b, h, t: (0, h)),
                  pl.BlockSpec((1, dv), lambda b, h, t: (0, 0))] + _const_specs(*consts),
        out_specs=pl.BlockSpec((None, tb, dv), lambda b, h, t: (b, t, h)),
        out_shape=jax.ShapeDtypeStruct((B, T, GLA_HEADS * dv), BF16),
        scratch_shapes=[pltpu.VMEM((1, dv, dk), F32)],
        compiler_params=_params("parallel", "parallel", "arbitrary"),
        name="gla_chunks",
    )(h_bf, h_bf, h_bf, h_bf, h_f, wa_pad, b_a, norm_g, *consts)


def _hgrn_call(h_bf, h_f, lb, norm_g, *, offs, heads, dk, dv, tb, nh=4):
    B, T, _ = h_bf.shape
    tb = min(tb, T)
    nh = min(nh, heads)
    assert heads % nh == 0
    consts = _lin_constants(LIN_CHUNK)
    kern = functools.partial(_hgrn_kernel, n_chunks=tb // LIN_CHUNK, nh=nh)
    return pl.pallas_call(
        kern,
        grid=(B, heads // nh, T // tb),
        in_specs=[_col_spec(tb, nh * dk, offs["hq"]),
                  pl.BlockSpec((None, tb, nh * dk), lambda b, h, t: (b, t, h)),
                  _col_spec(tb, nh * dv, offs["hi"]), _col_spec(tb, nh * dv, offs["hg"]),
                  pl.BlockSpec((1, nh * dk), lambda b, h, t: (0, h)),
                  pl.BlockSpec((1, dv), lambda b, h, t: (0, 0))] + _const_specs(*consts),
        out_specs=pl.BlockSpec((None, tb, nh * dv), lambda b, h, t: (b, t, h)),
        out_shape=jax.ShapeDtypeStruct((B, T, heads * dv), BF16),
        scratch_shapes=[pltpu.VMEM((nh, dv, dk), F32)],
        compiler_params=_params("parallel", "parallel", "arbitrary"),
        name="hgrn_chunks",
    )(h_bf, h_f, h_bf, h_bf, lb, norm_g, *consts)


def _memattn_kernel(q_ref, g_ref, k_ref, v_ref, o_ref, *, dh):
    scale = dh ** -0.5
    for h in range(MEM_HEADS):
        cs = slice(h * dh, (h + 1) * dh)
        s = _dot_nt(q_ref[:, cs], k_ref[:, cs]) * scale
        s = s - jnp.max(s, axis=-1, keepdims=True)
        p = jnp.exp(s)
        p = p / jnp.sum(p, axis=-1, keepdims=True)
        o = _dot(p.astype(BF16), v_ref[:, cs])
        o_ref[:, cs] = (o * _silu(g_ref[:, cs].astype(F32))).astype(o_ref.dtype)


def _memattn_call(h_bf, k_mem, v_mem, *, q_col, g_col, tm=512):
    B, T, _ = h_bf.shape
    _, ml, mw = k_mem.shape
    tm = min(tm, T)
    assert q_col % mw == 0 and g_col % mw == 0
    return pl.pallas_call(
        functools.partial(_memattn_kernel, dh=mw // MEM_HEADS),
        grid=(B, T // tm),
        in_specs=[pl.BlockSpec((None, tm, mw), lambda b, t: (b, t, q_col // mw)),
                  pl.BlockSpec((None, tm, mw), lambda b, t: (b, t, g_col // mw)),
                  pl.BlockSpec((None, ml, mw), lambda b, t: (b, 0, 0)),
                  pl.BlockSpec((None, ml, mw), lambda b, t: (b, 0, 0))],
        out_specs=pl.BlockSpec((None, tm, mw), lambda b, t: (b, t, 0)),
        out_shape=jax.ShapeDtypeStruct((B, T, mw), BF16),
        compiler_params=_params("parallel", "parallel"),
        name="mem_attention",
    )(h_bf, h_bf, k_mem, v_mem)


def _outproj_kernel(*refs, segs, nk, alpha, eps, emit_bf16):
    ns = len(segs)
    a_refs = refs[:ns]
    if emit_bf16:
        w_ref, x_hbm, g_ref, b_ref, o_ref, obf_hbm, sem, bf_buf, bf_sem = refs[ns:]
    else:
        w_ref, x_hbm, g_ref, b_ref, o_ref, sem = refs[ns:]
    i = pl.program_id(0)
    k = pl.program_id(1)
    tm, d = o_ref.shape
    rb, cb = min(128, tm), min(512, d)
    n_rb = tm // rb

    def bf_copy(r, slot):
        row0 = pl.multiple_of(i * tm + r * rb, rb)
        return pltpu.make_async_copy(bf_buf.at[slot], obf_hbm.at[pl.ds(row0, rb), :], bf_sem.at[slot])

    @pl.when(k == 0)
    def _():
        cp = pltpu.make_async_copy(x_hbm.at[pl.ds(pl.multiple_of(i * tm, tm), tm), :], o_ref, sem)
        cp.start()
        cp.wait()
        def scale(r, carry):
            rows = pl.ds(pl.multiple_of(r * rb, rb), rb)
            o_ref[rows, :] = alpha * o_ref[rows, :]
            return carry

        lax.fori_loop(0, tm // rb, scale, 0)

    for a_ref, (start, n) in zip(a_refs, segs):
        @pl.when((k >= start) & (k < start + n))
        def _(a_ref=a_ref):
            a = a_ref[...]
            for c in range(0, d, cb):
                o_ref[:, c:c + cb] += _dot(a, w_ref[:, c:c + cb])

    @pl.when(k == nk - 1)
    def _():
        def norm(r, carry):
            rows = pl.ds(pl.multiple_of(r * rb, rb), rb)
            y = o_ref[rows, :]
            mu = jnp.mean(y, axis=-1, keepdims=True)
            yc = y - mu
            var = jnp.mean(yc * yc, axis=-1, keepdims=True)
            out = yc * lax.rsqrt(var + eps) * g_ref[...] + b_ref[...]
            o_ref[rows, :] = out
            if emit_bf16:
                n_slot = bf_buf.shape[0]
                slot = r % n_slot

                @pl.when(r >= n_slot)
                def _():
                    bf_copy(r - n_slot, slot).wait()

                bf_buf[slot] = out.astype(bf_buf.dtype)
                bf_copy(r, slot).start()
            return carry

        lax.fori_loop(0, n_rb, norm, 0)
        if emit_bf16:
            for r in range(max(n_rb - bf_buf.shape[0], 0), n_rb):
                bf_copy(r, r % bf_buf.shape[0]).wait()


def _outproj_ln(acts, w_bf, x2d, ln_g, ln_b, alpha, emit_bf16, tm=1024, tk=512):
    m, d = x2d.shape
    tm = min(tm, m)
    segs, start = [], 0
    for a in acts:
        assert a.shape[0] == m and a.shape[1] % tk == 0
        segs.append((start, a.shape[1] // tk))
        start += a.shape[1] // tk
    nk = start
    assert nk * tk == w_bf.shape[0]

    def a_spec(s0, n):
        return pl.BlockSpec((tm, tk), lambda i, k: (i, jnp.clip(k - s0, 0, n - 1)))

    kern = functools.partial(_outproj_kernel, segs=tuple(segs), nk=nk, alpha=alpha, eps=1e-5,
                             emit_bf16=emit_bf16)
    rb = min(128, tm)
    out_specs = [pl.BlockSpec((tm, d), lambda i, k: (i, 0))]
    out_shape = [jax.ShapeDtypeStruct((m, d), F32)]
    scratch = [pltpu.SemaphoreType.DMA(())]
    if emit_bf16:
        out_specs.append(pl.BlockSpec(memory_space=pl.ANY))
        out_shape.append(jax.ShapeDtypeStruct((m, d), BF16))
        n_slot = 4
        scratch += [pltpu.VMEM((n_slot, rb, d), BF16), pltpu.SemaphoreType.DMA((n_slot,))]
    outs = pl.pallas_call(
        kern,
        grid=(m // tm, nk),
        in_specs=[a_spec(s0, n) for s0, n in segs] + [
            pl.BlockSpec((tk, d), lambda i, k: (k, 0)),
            pl.BlockSpec(memory_space=pl.ANY),
            pl.BlockSpec((1, d), lambda i, k: (0, 0)),
            pl.BlockSpec((1, d), lambda i, k: (0, 0))],
        out_specs=out_specs,
        out_shape=out_shape,
        scratch_shapes=scratch,
        compiler_params=_params("arbitrary", "arbitrary"),
        name="outproj_layernorm",
    )(*acts, w_bf, x2d, ln_g, ln_b)
    return (outs[0], outs[1]) if emit_bf16 else (outs[0], None)


def _mlstm_pre_kernel(x_ref, halo_ref, cw_ref, cb_ref, wq_ref, wk_ref, wkt_ref, wv_ref,
                      iq_ref, ik_ref, iv_ref, bif_ref,
                      xc_ref, q_ref, k_ref, kt_ref, v_ref, g_ref, *, n_sub, bw, n_taps):
    t = pl.program_id(1)
    c = pl.program_id(2)
    tm = x_ref.shape[0]

    @pl.when(c == 0)
    def _():
        g_ref[...] = jnp.broadcast_to(bif_ref[...], g_ref.shape)

    gates = jnp.zeros(g_ref.shape, F32)
    for j in range(n_sub):
        cs = slice(j * bw, (j + 1) * bw)
        x_bf = x_ref[:, cs]
        x32 = x_bf.astype(F32)
        halo = jnp.where(t > 0, halo_ref[:, cs].astype(F32), 0.0)
        ext = jnp.concatenate([halo, x32], axis=0)
        acc = jnp.broadcast_to(cb_ref[:, cs], (tm, bw))
        for tap in range(n_taps):
            sh = 8 - (n_taps - 1) + tap
            acc = acc + ext[sh:sh + tm, :] * cw_ref[tap:tap + 1, cs]
        xc_bf = _silu(acc).astype(BF16)
        xc_ref[:, cs] = xc_bf
        q_bf = _dot(xc_bf, wq_ref[j]).astype(BF16)
        k_bf = _dot(xc_bf, wk_ref[j]).astype(BF16)
        v_bf = _dot(x_bf, wv_ref[j]).astype(BF16)
        q_ref[:, cs] = q_bf
        k_ref[:, cs] = k_bf
        v_ref[:, cs] = v_bf
        kt_ref[cs, :] = _dot_nt(wkt_ref[j], xc_bf).astype(BF16)
        gates = gates + _dot(q_bf, iq_ref[cs, :]) + _dot(k_bf, ik_ref[cs, :]) + _dot(v_bf, iv_ref[cs, :])
    g_ref[...] += gates


def _block_diag(w, bw):
    nb, bi, bo = w.shape
    assert bi == bo and bw % bi == 0 and (nb * bi) % bw == 0
    per = bw // bi
    w4 = w.reshape(nb // per, per, bi, bo)
    eye = jnp.eye(per, dtype=w.dtype)
    return jnp.einsum("cnio,nm->cnimo", w4, eye).reshape(nb // per, bw, bw)


def _mlstm_pre_call(h_bf, conv_w, conv_b, w_q, w_k, w_v, w_if, b_if, *, width, tm=512, tc=1024, bw=256):
    B, T, _ = h_bf.shape
    tm, tc = min(tm, T), min(tc, width)
    n_taps = conv_w.shape[0]
    assert tm % 8 == 0 and n_taps <= 8 and width % tc == 0 and tc % bw == 0
    n_sub = tc // bw
    wq = _block_diag(w_q, bw).astype(BF16)
    wk = _block_diag(w_k, bw)
    wkt = jnp.swapaxes(wk, 1, 2).astype(BF16)
    wk = wk.astype(BF16)
    wv = _block_diag(w_v, bw).astype(BF16)
    n_gate = w_if.shape[1]
    wif = jnp.pad(w_if, ((0, 0), (0, 128 - n_gate))).astype(BF16)
    bif = jnp.pad(b_if.astype(F32), (0, 128 - n_gate)).reshape(1, 128)
    hb = tm // 8
    tile_spec = pl.BlockSpec((n_sub, bw, bw), lambda b, t, c: (c, 0, 0))
    act_spec = pl.BlockSpec((None, tm, tc), lambda b, t, c: (b, t, c))
    kern = functools.partial(_mlstm_pre_kernel, n_sub=n_sub, bw=bw, n_taps=n_taps)
    shp = jax.ShapeDtypeStruct((B, T, width), BF16)
    return pl.pallas_call(
        kern,
        grid=(B, T // tm, width // tc),
        in_specs=[act_spec,
                  pl.BlockSpec((None, 8, tc), lambda b, t, c: (b, jnp.maximum(t * hb - 1, 0), c)),
                  pl.BlockSpec((n_taps, tc), lambda b, t, c: (0, c)),
                  pl.BlockSpec((1, tc), lambda b, t, c: (0, c)),
                  tile_spec, tile_spec, tile_spec, tile_spec,
                  pl.BlockSpec((tc, 128), lambda b, t, c: (c, 0)),
                  pl.BlockSpec((tc, 128), lambda b, t, c: (width // tc + c, 0)),
                  pl.BlockSpec((tc, 128), lambda b, t, c: (2 * (width // tc) + c, 0)),
                  pl.BlockSpec((1, 128), lambda b, t, c: (0, 0))],
        out_specs=[act_spec, act_spec, act_spec,
                   pl.BlockSpec((None, tc, tm), lambda b, t, c: (b, c, t)),
                   act_spec,
                   pl.BlockSpec((None, tm, 128), lambda b, t, c: (b, t, 0))],
        out_shape=[shp, shp, shp, jax.ShapeDtypeStruct((B, width, T), BF16), shp,
                   jax.ShapeDtypeStruct((B, T, 128), F32)],
        compiler_params=_params("parallel", "parallel", "arbitrary"),
        name="mlstm_frontend",
    )(h_bf, h_bf, conv_w, conv_b.reshape(1, width), wq, wk, wkt, wv, wif, wif, wif, bif)


def _mlstm_kernel(q_ref, k_ref, kt_ref, v_ref, gc_ref, gr_ref, xc_ref, z_ref, ng_ref, sk_ref, o_ref,
                  c_ref, hh_ref, n_ref, m_ref, *, n_chunks, heads, k_scale, dvb):
    h = pl.program_id(1)
    L = MLSTM_CHUNK
    dk, dv = c_ref.shape

    @pl.when(pl.program_id(2) == 0)
    def _():
        c_ref[...] = jnp.zeros_like(c_ref)
        n_ref[...] = jnp.zeros_like(n_ref)
        m_ref[...] = jnp.zeros_like(m_ref)

    row = lax.broadcasted_iota(jnp.int32, (L, L), 0)
    col = lax.broadcasted_iota(jnp.int32, (L, L), 1)
    causal = row >= col
    glane = lax.broadcasted_iota(jnp.int32, (L, gc_ref.shape[1]), 1)

    for c in range(n_chunks):
        rs = slice(c * L, (c + 1) * L)
        qc = q_ref[rs, :]
        kc = k_ref[rs, :]
        vc = v_ref[rs, :]
        gcol = gc_ref[rs, :]
        i_col = jnp.sum(jnp.where(glane == h, gcol, 0.0), axis=1, keepdims=True)
        f_col = jnp.sum(jnp.where(glane == heads + h, gcol, 0.0), axis=1, keepdims=True)
        i_row = gr_ref[0:1, rs]
        f_row = gr_ref[1:2, rs]
        lf_col = _log_sigmoid(f_col)
        lf_row = _log_sigmoid(f_row)
        b_col = jnp.sum(jnp.where(causal, lf_row, 0.0), axis=1, keepdims=True)
        b_row = jnp.sum(jnp.where(row <= col, lf_col, 0.0), axis=0, keepdims=True)
        b_last = b_col[L - 1:L, :]
        m_prev = m_ref[0:1, 0:1]

        d = jnp.where(causal, b_col - b_row + i_row, NEG_BIG)
        inter_log = b_col + m_prev
        m_i = jnp.maximum(inter_log, jnp.max(d, axis=1, keepdims=True))
        w_intra = jnp.exp(d - m_i)
        w_inter = jnp.exp(inter_log - m_i) * k_scale
        scores = _dot_nt(qc, kc) * k_scale * w_intra
        n_row = n_ref[...]
        den = (jnp.sum(scores, axis=1, keepdims=True)
               + w_inter * jnp.sum(qc.astype(F32) * n_row, axis=1, keepdims=True))
        inv = 1.0 / jnp.maximum(jnp.abs(den), jnp.exp(-m_i))
        scores_bf = scores.astype(BF16)

        log_wj_row = b_last - b_row + i_row
        log_wj_col = b_last - b_col + i_col
        m_new = jnp.maximum(b_last + m_prev, jnp.max(log_wj_row, axis=1, keepdims=True))
        wj_row = jnp.exp(log_wj_row - m_new)
        wj_col = jnp.exp(log_wj_col - m_new)
        dec = jnp.exp(b_last + m_prev - m_new)
        ktw = (kt_ref[:, rs].astype(F32) * wj_row).astype(BF16)

        for j in range(dv // dvb):
            cs = slice(j * dvb, (j + 1) * dvb)
            c_old = c_ref[:, cs]
            num = _dot(scores_bf, vc[:, cs]) + w_inter * _dot(qc, c_old.astype(BF16))
            hh_ref[:, cs] = num * inv
            c_ref[:, cs] = c_old * dec + _dot(ktw, vc[:, cs])
        n_ref[...] = n_row * dec + jnp.sum(kc.astype(F32) * wj_col, axis=0, keepdims=True)
        m_ref[...] = jnp.broadcast_to(m_new, m_ref.shape)

        hh = hh_ref[...]
        mu = jnp.mean(hh, axis=-1, keepdims=True)
        hc = hh - mu
        var = jnp.mean(hc * hc, axis=-1, keepdims=True)
        hn = hc * lax.rsqrt(var + 1e-6) * ng_ref[...]
        out = (hn + sk_ref[...] * xc_ref[rs, :].astype(F32)) * _silu(z_ref[rs, :].astype(F32))
        o_ref[rs, :] = out.astype(o_ref.dtype)


def _mlstm_call(q, k, kt, v, gates, gates_rows, xc, h_bf, norm_g, skip, *, heads, z_col, tb=MLSTM_CHUNK):
    B, T, width = q.shape
    dh = width // heads
    tb = min(tb, T)
    assert tb % MLSTM_CHUNK == 0 and z_col % dh == 0
    dvb = min(512, dh)
    hs = pl.BlockSpec((None, tb, dh), lambda b, h, t: (b, t, h))
    kern = functools.partial(_mlstm_kernel, n_chunks=tb // MLSTM_CHUNK, heads=heads,
                             k_scale=dh ** -0.5, dvb=dvb)
    return pl.pallas_call(
        kern,
        grid=(B, heads, T // tb),
        in_specs=[hs, hs,
                  pl.BlockSpec((None, dh, tb), lambda b, h, t: (b, h, t)),
                  hs,
                  pl.BlockSpec((None, tb, gates.shape[2]), lambda b, h, t: (b, t, 0)),
                  pl.BlockSpec((None, None, 2, tb), lambda b, h, t: (b, h, 0, t)),
                  hs,
                  pl.BlockSpec((None, tb, dh), lambda b, h, t: (b, t, z_col // dh + h)),
                  pl.BlockSpec((1, dh), lambda b, h, t: (0, h)),
                  pl.BlockSpec((1, dh), lambda b, h, t: (0, h))],
        out_specs=hs,
        out_shape=jax.ShapeDtypeStruct((B, T, width), BF16),
        scratch_shapes=[pltpu.VMEM((dh, dh), F32), pltpu.VMEM((MLSTM_CHUNK, dh), F32),
                        pltpu.VMEM((1, dh), F32), pltpu.VMEM((8, 128), F32)],
        compiler_params=_params("parallel", "parallel", "arbitrary"),
        name="mlstm_chunks",
    )(q, k, kt, v, gates, gates_rows, xc, h_bf, norm_g, skip)


def _mem_kv(mem_bf, w_k, w_v, B):
    mw = w_k.shape[1]
    k = _matmul(mem_bf, w_k.astype(BF16), BF16).reshape(B, -1, mw)
    v = _matmul(mem_bf, w_v.astype(BF16), BF16).reshape(B, -1, mw)
    return k, v


def _even_layer(x, x_bf, emit_bf16, mem_bf, lb, w_in, gla_w_a2, gla_b_a, gla_norm_g, hgrn_norm_g,
                mem_w_k, mem_w_v, w_out, ln_g, ln_b, alpha):
    B, T, D = x.shape
    lowrank, kw = gla_w_a2.shape
    gdv = gla_norm_g.shape[0]
    gw = GLA_HEADS * gdv
    gdk = kw // GLA_HEADS
    fw = lb.shape[0]
    hheads = fw // HGRN_EXPAND
    hdv = hgrn_norm_g.shape[0]
    hw = hheads * hdv
    mw = mem_w_k.shape[1]
    sizes = (kw, kw, gw, gw, lowrank, fw, fw, hw, hw, mw, mw)
    assert sum(sizes) == w_in.shape[1]
    bounds = [0]
    for s in sizes:
        bounds.append(bounds[-1] + s)
    ga, hf = w_in[:, bounds[4]:bounds[5]], w_in[:, bounds[6]:bounds[7]]

    ga_w = 256
    w_a = w_in[:, :bounds[4]].astype(BF16)
    w_b = w_in[:, bounds[5]:].astype(BF16)
    w_f = jnp.concatenate([hf, jnp.pad(ga, ((0, 0), (0, ga_w - lowrank)))], axis=1).astype(BF16)
    offs_a = {"gq": 0, "gk": kw, "gv": 2 * kw, "gg": 2 * kw + gw}
    offs_b = {"hq": 0, "hi": fw, "hg": fw + hw, "mq": fw + 2 * hw, "mg": fw + 2 * hw + mw}

    x2d = x.reshape(B * T, D)
    if x_bf is None:
        x_bf = x2d.astype(BF16)
    h_a = _matmul(x_bf, w_a, BF16)
    h_b = _matmul(x_bf, w_b, BF16, skip=(fw, fw))
    h_f = _matmul(x_bf, w_f, F32, tn=256)
    h_a, h_b, h_f = (h.reshape(B, T, -1) for h in (h_a, h_b, h_f))

    wa_pad = jnp.pad(gla_w_a2, ((0, ga_w - lowrank), (0, 0))).astype(BF16)
    gla_out = _gla_call(h_a, h_f, wa_pad, gla_b_a.reshape(1, kw), gla_norm_g.reshape(1, gdv),
                        offs=offs_a, ga_col=fw, dk=gdk, dv=gdv, tb=512)
    hgrn_out = _hgrn_call(h_b, h_f, lb.reshape(1, fw), hgrn_norm_g.reshape(1, hdv),
                          offs=offs_b, heads=hheads, dk=HGRN_EXPAND, dv=hdv, tb=512)
    k_mem, v_mem = _mem_kv(mem_bf, mem_w_k, mem_w_v, B)
    mem_out = _memattn_call(h_b, k_mem, v_mem, q_col=offs_b["mq"], g_col=offs_b["mg"])

    acts = [a.reshape(B * T, -1) for a in (gla_out, hgrn_out, mem_out)]
    y, y_bf = _outproj_ln(acts, w_out.astype(BF16), x2d, ln_g.reshape(1, D), ln_b.reshape(1, D), alpha, emit_bf16)
    return y.reshape(B, T, D), y_bf


def _odd_layer(x, x_bf, emit_bf16, mem_bf, w_in, conv_w, conv_b, w_q, w_k, w_v, w_if, b_if, mh_norm_g, skip,
               mem_w_k, mem_w_v, w_out, ln_g, ln_b, alpha):
    B, T, D = x.shape
    width = conv_w.shape[1]
    heads = b_if.shape[0] // 2
    mw = mem_w_k.shape[1]
    assert w_in.shape[1] == 2 * width + 2 * mw

    x2d = x.reshape(B * T, D)
    if x_bf is None:
        x_bf = x2d.astype(BF16)
    h_bf = _matmul(x_bf, w_in.astype(BF16), BF16).reshape(B, T, -1)

    xc, q, k, kt, v, gates = _mlstm_pre_call(h_bf, conv_w, conv_b, w_q, w_k, w_v, w_if, b_if, width=width)
    gates_rows = jnp.transpose(gates[:, :, :2 * heads].reshape(B, T, 2, heads), (0, 3, 2, 1))
    mlstm_out = _mlstm_call(q, k, kt, v, gates, gates_rows, xc, h_bf,
                            mh_norm_g.reshape(1, width), skip.reshape(1, width), heads=heads, z_col=width)
    k_mem, v_mem = _mem_kv(mem_bf, mem_w_k, mem_w_v, B)
    mem_out = _memattn_call(h_bf, k_mem, v_mem, q_col=2 * width, g_col=2 * width + mw)

    acts = [a.reshape(B * T, -1) for a in (mlstm_out, mem_out)]
    y, y_bf = _outproj_ln(acts, w_out.astype(BF16), x2d, ln_g.reshape(1, D), ln_b.reshape(1, D), alpha, emit_bf16)
    return y.reshape(B, T, D), y_bf


def kernel(x, mem, hgrn_lb_logits, ev_w_in, ev_gla_w_a2, ev_gla_b_a, ev_gla_norm_g, ev_hgrn_norm_g, ev_mem_w_k, ev_mem_w_v, ev_w_out, ev_ln_g, ev_ln_b, od_w_in, od_conv_w, od_conv_b, od_w_q, od_w_k, od_w_v, od_w_if, od_b_if, od_mh_norm_g, od_skip, od_mem_w_k, od_mem_w_v, od_w_out, od_ln_g, od_ln_b):
    depth = ev_w_in.shape[0] + od_w_in.shape[0]
    alpha = (2 * depth) ** 0.25
    B, ml, D = mem.shape
    mem_bf = mem.reshape(B * ml, D).astype(BF16)
    lb_all = jnp.cumsum(jax.nn.softmax(hgrn_lb_logits.astype(F32), axis=0), axis=0)
    x_bf = None
    for layer in range(depth):
        i = layer // 2
        emit_bf16 = layer + 1 < depth
        if layer % 2 == 0:
            x, x_bf = _even_layer(x, x_bf, emit_bf16, mem_bf, lb_all[layer], ev_w_in[i], ev_gla_w_a2[i],
                                  ev_gla_b_a[i], ev_gla_norm_g[i], ev_hgrn_norm_g[i], ev_mem_w_k[i],
                                  ev_mem_w_v[i], ev_w_out[i], ev_ln_g[i], ev_ln_b[i], alpha)
        else:
            x, x_bf = _odd_layer(x, x_bf, emit_bf16, mem_bf, od_w_in[i], od_conv_w[i], od_conv_b[i], od_w_q[i],
                                 od_w_k[i], od_w_v[i], od_w_if[i], od_b_if[i], od_mh_norm_g[i], od_skip[i],
                                 od_mem_w_k[i], od_mem_w_v[i], od_w_out[i], od_ln_g[i], od_ln_b[i], alpha)
    return x
```

```python
import functools

import jax
import jax.numpy as jnp
import numpy as np
from jax import lax
from jax.experimental import pallas as pl
from jax.experimental.pallas import tpu as pltpu

F32 = jnp.float32
BF16 = jnp.bfloat16

GLA_HEADS = 4
GLA_TAU = 16.0
HGRN_EXPAND = 128
MEM_HEADS = 4

LIN_CHUNK = 128
MLSTM_CHUNK = 256

V7X_VMEM_LIMIT_BYTES = 56 * 1024 * 1024
NEG_BIG = -1e30
LOG2_E = 1.4426950408889634


def _sigmoid(x):
    return 1.0 / (1.0 + jnp.exp(-x))


def _silu(x):
    return x * _sigmoid(x)


def _log_sigmoid(x):
    return jnp.minimum(x, 0.0) - jnp.log(1.0 + jnp.exp(-jnp.abs(x)))


def _dot(a, b):
    return jnp.dot(a, b, preferred_element_type=F32)


def _dot_nt(a, b):
    return lax.dot_general(a, b, (((1,), (1,)), ((), ())), preferred_element_type=F32)


def _dot_tn(a, b):
    return lax.dot_general(a, b, (((0,), (0,)), ((), ())), preferred_element_type=F32)


def _params(*sem):
    return pltpu.CompilerParams(dimension_semantics=sem, vmem_limit_bytes=V7X_VMEM_LIMIT_BYTES)


def _mm_kernel(a_ref, w_ref, o_ref):
    o_ref[...] = _dot(a_ref[...], w_ref[...]).astype(o_ref.dtype)


def _matmul(a, w, out_dtype, tm=1024, tn=1024, cols=None):
    m, k = a.shape
    cols = [(0, w.shape[1])] if cols is None else list(cols)
    tm = min(tm, m)
    while any(v % tn for rng in cols for v in rng):
        tn //= 2
    assert m % tm == 0 and tn % 128 == 0, (m, cols, tm, tn)
    n_out = sum(n for _, n in cols)

    def w_block(i, j):
        blk, done = None, 0
        for first, n in cols:
            here = first // tn + (j - done)
            blk = here if blk is None else jnp.where(j >= done, here, blk)
            done += n // tn
        return 0, blk

    return pl.pallas_call(
        _mm_kernel,
        grid=(m // tm, n_out // tn),
        in_specs=[pl.BlockSpec((tm, k), lambda i, j: (i, 0)),
                  pl.BlockSpec((k, tn), w_block)],
        out_specs=pl.BlockSpec((tm, tn), lambda i, j: (i, j)),
        out_shape=jax.ShapeDtypeStruct((m, n_out), out_dtype),
        compiler_params=_params("parallel", "arbitrary"),
        name="proj_matmul",
    )(a, w)


def _window_cast_kernel(*refs, shift):
    if shift:
        main_ref, tail_ref, o_ref = refs
        full = jnp.concatenate([main_ref[...], tail_ref[...]], axis=1)
        full = pltpu.roll(full, full.shape[1] - shift, axis=1)
        o_ref[...] = full[:, :o_ref.shape[1]].astype(o_ref.dtype)
    else:
        main_ref, o_ref = refs
        o_ref[...] = main_ref[...].astype(o_ref.dtype)


def _window_cast(w, col0, ncols, tk=512, tc=1024):
    k, n = w.shape
    tk = min(tk, k)
    shift = col0 % 128
    base = col0 - shift
    while ncols % tc or base % tc:
        tc //= 2
    assert k % tk == 0 and tc % 128 == 0 and base % tc == 0, (col0, ncols, tc)
    in_specs = [pl.BlockSpec((tk, tc), lambda i, j: (i, base // tc + j))]
    operands = [w]
    if shift:
        assert base + ncols + shift <= n
        in_specs.append(pl.BlockSpec((tk, 128), lambda i, j: (i, (base + (j + 1) * tc) // 128)))
        operands.append(w)
    return pl.pallas_call(
        functools.partial(_window_cast_kernel, shift=shift),
        grid=(k // tk, ncols // tc),
        in_specs=in_specs,
        out_specs=pl.BlockSpec((tk, tc), lambda i, j: (i, j)),
        out_shape=jax.ShapeDtypeStruct((k, ncols), BF16),
        compiler_params=_params("parallel", "parallel"),
        name="weight_window_cast",
    )(*operands)


def _lin_constants(L):
    i = np.arange(L)[:, None]
    t = np.arange(L)[None, :]
    sums = [t <= i]
    masks = [i == t]
    m = L // 2
    while m >= 1:
        r = (i // (2 * m)) * 2 * m + m - 1
        second = (i & m) != 0
        sums.append(np.where(second, (t > r) & (t <= i), (t > i) & (t <= r)))
        masks.append((i // (2 * m) == t // (2 * m)) & second & ((t & m) == 0))
        m //= 2
    sums.append(t > i)
    s = np.concatenate(sums, axis=0).astype(np.float32)
    return (jnp.asarray(np.concatenate([s, s], axis=1), BF16),
            jnp.asarray(np.stack(masks).astype(np.float32)))


def _lin_attn_chunk(q, k, v_bf, g, st_ref, sums_ref, mask_ref, nh):
    L, wk = q.shape
    dk = wk // nh
    dv = v_bf.shape[1] // nh
    n_lvl = mask_ref.shape[0] - 1
    g2 = g * LOG2_E
    hi = g2.astype(BF16)
    lo = (g2 - hi.astype(F32)).astype(BF16)
    p = jnp.exp2(_dot(sums_ref[...], jnp.concatenate([hi, lo], axis=0)))
    p_last = p[L - 1:L, :]
    qd = (q * p[0:L]).astype(BF16)
    kd = (k * p[(n_lvl + 1) * L:(n_lvl + 2) * L]).astype(BF16)
    q_bf = q.astype(BF16)
    k_bf = k.astype(BF16)
    rowi = lax.broadcasted_iota(jnp.int32, (L, 1), 0)
    xs = []
    for lvl in range(n_lvl):
        second = (rowi & (L >> (lvl + 1))) != 0
        xs.append((jnp.where(second, q, k) * p[(lvl + 1) * L:(lvl + 2) * L]).astype(BF16))
    outs = []
    for h in range(nh):
        ck = slice(h * dk, (h + 1) * dk)
        cv = slice(h * dv, (h + 1) * dv)
        a = _dot_nt(q_bf[:, ck], k_bf[:, ck]) * mask_ref[0]
        for lvl in range(n_lvl):
            x = xs[lvl][:, ck]
            a = a + _dot_nt(x, x) * mask_ref[lvl + 1]
        st = st_ref[h]
        outs.append(_dot_nt(qd[:, ck], st.astype(BF16)) + _dot(a.astype(BF16), v_bf[:, cv]))
        st_ref[h] = st * p_last[:, ck] + _dot_tn(v_bf[:, cv], kd[:, ck])
    return outs


def _rms_gate(o, norm_g, gate, eps=1e-6):
    ms = jnp.mean(o * o, axis=-1, keepdims=True)
    return o * lax.rsqrt(ms + eps) * norm_g * _silu(gate)


def _gla_kernel(q_ref, k_ref, v_ref, gg_ref, ga_ref, wa_ref, ba_ref, ng_ref, sums_ref, mask_ref, o_ref,
                st_ref, *, n_chunks, q_scale):
    @pl.when(pl.program_id(2) == 0)
    def _():
        st_ref[...] = jnp.zeros_like(st_ref)

    def chunk(c, carry):
        rows = pl.ds(pl.multiple_of(c * LIN_CHUNK, LIN_CHUNK), LIN_CHUNK)
        q = q_ref[rows, :].astype(F32) * q_scale
        k = k_ref[rows, :].astype(F32)
        a_pre = _dot(ga_ref[rows, :].astype(BF16), wa_ref[...]) + ba_ref[...]
        g = _log_sigmoid(a_pre) * (1.0 / GLA_TAU)
        (o,) = _lin_attn_chunk(q, k, v_ref[rows, :], g, st_ref, sums_ref, mask_ref, 1)
        o_ref[rows, :] = _rms_gate(o, ng_ref[...], gg_ref[rows, :].astype(F32)).astype(o_ref.dtype)
        return carry

    lax.fori_loop(0, n_chunks, chunk, 0, unroll=4)


def _hgrn_kernel(q_ref, f_ref, v_ref, hg_ref, lb_ref, ng_ref, sums_ref, mask_ref, o_ref,
                 st_ref, *, n_chunks, nh):
    dv = ng_ref.shape[1]

    @pl.when(pl.program_id(2) == 0)
    def _():
        st_ref[...] = jnp.zeros_like(st_ref)

    def chunk(c, carry):
        rows = pl.ds(pl.multiple_of(c * LIN_CHUNK, LIN_CHUNK), LIN_CHUNK)
        lb = lb_ref[...]
        f = lb + (1.0 - lb) * _sigmoid(f_ref[rows, :])
        q = _silu(q_ref[rows, :].astype(F32))
        outs = _lin_attn_chunk(q, 1.0 - f, v_ref[rows, :], jnp.log(f), st_ref, sums_ref, mask_ref, nh)
        gate = hg_ref[rows, :].astype(F32)
        for h, o in enumerate(outs):
            cv = slice(h * dv, (h + 1) * dv)
            o_ref[rows, cv] = _rms_gate(o, ng_ref[...], gate[:, cv]).astype(o_ref.dtype)
        return carry

    lax.fori_loop(0, n_chunks, chunk, 0, unroll=4)


def _col_spec(tb, width, col0):
    assert col0 % width == 0, (col0, width)
    base = col0 // width
    return pl.BlockSpec((None, tb, width), lambda b, h, t: (b, t, base + h))


def _const_specs(*arrays):
    return [pl.BlockSpec(a.shape, lambda b, h, t, nd=a.ndim: (0,) * nd) for a in arrays]


def _gla_call(h_bf, h_f, wa_pad, b_a, norm_g, *, offs, ga_col, dk, dv, tb):
    B, T, _ = h_bf.shape
    tb = min(tb, T)
    ga_w = wa_pad.shape[0]
    assert ga_col % ga_w == 0
    consts = _lin_constants(LIN_CHUNK)
    kern = functools.partial(_gla_kernel, n_chunks=tb // LIN_CHUNK, q_scale=dk ** -0.5)
    return pl.pallas_call(
        kern,
        grid=(B, GLA_HEADS, T // tb),
        in_specs=[_col_spec(tb, dk, offs["gq"]), _col_spec(tb, dk, offs["gk"]),
                  _col_spec(tb, dv, offs["gv"]), _col_spec(tb, dv, offs["gg"]),
                  pl.BlockSpec((None, tb, ga_w), lambda b, h, t: (b, t, ga_col // ga_w)),
                  pl.BlockSpec((ga_w, dk), lambda b, h, t: (0, h)),
                  pl.BlockSpec((1, dk), lambda b, h, t: (0, h)),
                  pl.BlockSpec((1, dv), lambda b, h, t: (0, 0))] + _const_specs(*consts),
        out_specs=pl.BlockSpec((None, tb, dv), lambda b, h, t: (b, t, h)),
        out_shape=jax.ShapeDtypeStruct((B, T, GLA_HEADS * dv), BF16),
        scratch_shapes=[pltpu.VMEM((1, dv, dk), F32)],
        compiler_params=_params("parallel", "parallel", "arbitrary"),
        name="gla_chunks",
    )(h_bf, h_bf, h_bf, h_bf, h_f, wa_pad, b_a, norm_g, *consts)


def _hgrn_call(h_bf, h_f, lb, norm_g, *, offs, heads, dk, dv, tb, nh=4):
    B, T, _ = h_bf.shape
    tb = min(tb, T)
    nh = min(nh, heads)
    assert heads % nh == 0
    consts = _lin_constants(LIN_CHUNK)
    kern = functools.partial(_hgrn_kernel, n_chunks=tb // LIN_CHUNK, nh=nh)
    return pl.pallas_call(
        kern,
        grid=(B, heads // nh, T // tb),
        in_specs=[_col_spec(tb, nh * dk, offs["hq"]),
                  pl.BlockSpec((None, tb, nh * dk), lambda b, h, t: (b, t, h)),
                  _col_spec(tb, nh * dv, offs["hi"]), _col_spec(tb, nh * dv, offs["hg"]),
                  pl.BlockSpec((1, nh * dk), lambda b, h, t: (0, h)),
                  pl.BlockSpec((1, dv), lambda b, h, t: (0, 0))] + _const_specs(*consts),
        out_specs=pl.BlockSpec((None, tb, nh * dv), lambda b, h, t: (b, t, h)),
        out_shape=jax.ShapeDtypeStruct((B, T, heads * dv), BF16),
        scratch_shapes=[pltpu.VMEM((nh, dv, dk), F32)],
        compiler_params=_params("parallel", "parallel", "arbitrary"),
        name="hgrn_chunks",
    )(h_bf, h_f, h_bf, h_bf, lb, norm_g, *consts)


def _memattn_kernel(q_ref, g_ref, k_ref, v_ref, o_ref, *, dh):
    scale = dh ** -0.5
    for h in range(MEM_HEADS):
        cs = slice(h * dh, (h + 1) * dh)
        s = _dot_nt(q_ref[:, cs], k_ref[:, cs]) * scale
        s = s - jnp.max(s, axis=-1, keepdims=True)
        p = jnp.exp(s)
        p = p / jnp.sum(p, axis=-1, keepdims=True)
        o = _dot(p.astype(BF16), v_ref[:, cs])
        o_ref[:, cs] = (o * _silu(g_ref[:, cs].astype(F32))).astype(o_ref.dtype)


def _memattn_call(h_bf, k_mem, v_mem, *, q_col, g_col, tm=512):
    B, T, _ = h_bf.shape
    _, ml, mw = k_mem.shape
    tm = min(tm, T)
    assert q_col % mw == 0 and g_col % mw == 0
    return pl.pallas_call(
        functools.partial(_memattn_kernel, dh=mw // MEM_HEADS),
        grid=(B, T // tm),
        in_specs=[pl.BlockSpec((None, tm, mw), lambda b, t: (b, t, q_col // mw)),
                  pl.BlockSpec((None, tm, mw), lambda b, t: (b, t, g_col // mw)),
                  pl.BlockSpec((None, ml, mw), lambda b, t: (b, 0, 0)),
                  pl.BlockSpec((None, ml, mw), lambda b, t: (b, 0, 0))],
        out_specs=pl.BlockSpec((None, tm, mw), lambda b, t: (b, t, 0)),
        out_shape=jax.ShapeDtypeStruct((B, T, mw), BF16),
        compiler_params=_params("parallel", "parallel"),
        name="mem_attention",
    )(h_bf, h_bf, k_mem, v_mem)


def _outproj_kernel(*refs, segs, nk, alpha, eps, emit_bf16):
    ns = len(segs)
    a_refs = refs[:ns]
    if emit_bf16:
        w_ref, x_hbm, g_ref, b_ref, o_ref, obf_hbm, sem, bf_buf, bf_sem = refs[ns:]
    else:
        w_ref, x_hbm, g_ref, b_ref, o_ref, sem = refs[ns:]
    i = pl.program_id(0)
    k = pl.program_id(1)
    tm, d = o_ref.shape
    rb, cb = min(128, tm), min(512, d)
    n_rb = tm // rb

    def bf_copy(r, slot):
        row0 = pl.multiple_of(i * tm + r * rb, rb)
        return pltpu.make_async_copy(bf_buf.at[slot], obf_hbm.at[pl.ds(row0, rb), :], bf_sem.at[slot])

    @pl.when(k == 0)
    def _():
        n_slab = sem.shape[0]
        sb = tm // n_slab
        copies = [pltpu.make_async_copy(x_hbm.at[pl.ds(pl.multiple_of(i * tm + s * sb, sb), sb), :],
                                        o_ref.at[pl.ds(s * sb, sb), :], sem.at[s]) for s in range(n_slab)]
        for cp in copies:
            cp.start()
        for s, cp in enumerate(copies):
            cp.wait()
            a = a_refs[0][s * sb:(s + 1) * sb, :]
            for c in range(0, d, cb):
                o_ref[s * sb:(s + 1) * sb, c:c + cb] = (alpha * o_ref[s * sb:(s + 1) * sb, c:c + cb]
                                                        + _dot(a, w_ref[:, c:c + cb]))

    for a_ref, (start, n) in zip(a_refs, segs):
        @pl.when((k >= max(start, 1)) & (k < start + n))
        def _(a_ref=a_ref):
            a = a_ref[...]
            for c in range(0, d, cb):
                o_ref[:, c:c + cb] += _dot(a, w_ref[:, c:c + cb])

    @pl.when(k == nk - 1)
    def _():
        def norm(r, carry):
            rows = pl.ds(pl.multiple_of(r * rb, rb), rb)
            y = o_ref[rows, :]
            mu = jnp.mean(y, axis=-1, keepdims=True)
            yc = y - mu
            var = jnp.mean(yc * yc, axis=-1, keepdims=True)
            out = yc * lax.rsqrt(var + eps) * g_ref[...] + b_ref[...]
            o_ref[rows, :] = out
            if emit_bf16:
                n_slot = bf_buf.shape[0]
                slot = r % n_slot

                @pl.when(r >= n_slot)
                def _():
                    bf_copy(r - n_slot, slot).wait()

                bf_buf[slot] = out.astype(bf_buf.dtype)
                bf_copy(r, slot).start()
            return carry

        lax.fori_loop(0, n_rb, norm, 0)
        if emit_bf16:
            for r in range(max(n_rb - bf_buf.shape[0], 0), n_rb):
                bf_copy(r, r % bf_buf.shape[0]).wait()


def _outproj_ln(acts, w_bf, x2d, ln_g, ln_b, alpha, emit_bf16, tm=1024, tk=512):
    m, d = x2d.shape
    tm = min(tm, m)
    segs, start = [], 0
    for a in acts:
        assert a.shape[0] == m and a.shape[1] % tk == 0
        segs.append((start, a.shape[1] // tk))
        start += a.shape[1] // tk
    nk = start
    assert nk * tk == w_bf.shape[0]

    def a_spec(s0, n):
        return pl.BlockSpec((tm, tk), lambda i, k: (i, jnp.clip(k - s0, 0, n - 1)))

    kern = functools.partial(_outproj_kernel, segs=tuple(segs), nk=nk, alpha=alpha, eps=1e-5,
                             emit_bf16=emit_bf16)
    rb = min(128, tm)
    out_specs = [pl.BlockSpec((tm, d), lambda i, k: (i, 0))]
    out_shape = [jax.ShapeDtypeStruct((m, d), F32)]
    n_slab = 4 if tm % (4 * 8) == 0 else 1
    scratch = [pltpu.SemaphoreType.DMA((n_slab,))]
    if emit_bf16:
        out_specs.append(pl.BlockSpec(memory_space=pl.ANY))
        out_shape.append(jax.ShapeDtypeStruct((m, d), BF16))
        n_slot = 4
        scratch += [pltpu.VMEM((n_slot, rb, d), BF16), pltpu.SemaphoreType.DMA((n_slot,))]
    outs = pl.pallas_call(
        kern,
        grid=(m // tm, nk),
        in_specs=[a_spec(s0, n) for s0, n in segs] + [
            pl.BlockSpec((tk, d), lambda i, k: (k, 0)),
            pl.BlockSpec(memory_space=pl.ANY),
            pl.BlockSpec((1, d), lambda i, k: (0, 0)),
            pl.BlockSpec((1, d), lambda i, k: (0, 0))],
        out_specs=out_specs,
        out_shape=out_shape,
        scratch_shapes=scratch,
        compiler_params=_params("arbitrary", "arbitrary"),
        name="outproj_layernorm",
    )(*acts, w_bf, x2d, ln_g, ln_b)
    return (outs[0], outs[1]) if emit_bf16 else (outs[0], None)


def _mlstm_pre_kernel(x_ref, halo_ref, cw_ref, cb_ref, wq_ref, wk_ref, wkt_ref, wv_ref,
                      iq_ref, ik_ref, iv_ref, bif_ref,
                      xc_ref, q_ref, k_ref, kt_ref, v_ref, g_ref, *, n_sub, bw, n_taps):
    t = pl.program_id(1)
    c = pl.program_id(2)
    tm = x_ref.shape[0]

    @pl.when(c == 0)
    def _():
        g_ref[...] = jnp.broadcast_to(bif_ref[...], g_ref.shape)

    gates = jnp.zeros(g_ref.shape, F32)
    for j in range(n_sub):
        cs = slice(j * bw, (j + 1) * bw)
        x_bf = x_ref[:, cs]
        x32 = x_bf.astype(F32)
        halo = jnp.where(t > 0, halo_ref[:, cs].astype(F32), 0.0)
        ext = jnp.concatenate([halo, x32], axis=0)
        acc = jnp.broadcast_to(cb_ref[:, cs], (tm, bw))
        for tap in range(n_taps):
            sh = 8 - (n_taps - 1) + tap
            acc = acc + ext[sh:sh + tm, :] * cw_ref[tap:tap + 1, cs]
        xc_bf = _silu(acc).astype(BF16)
        xc_ref[:, cs] = xc_bf
        q_bf = _dot(xc_bf, wq_ref[j]).astype(BF16)
        k_bf = _dot(xc_bf, wk_ref[j]).astype(BF16)
        v_bf = _dot(x_bf, wv_ref[j]).astype(BF16)
        q_ref[:, cs] = q_bf
        k_ref[:, cs] = k_bf
        v_ref[:, cs] = v_bf
        kt_ref[cs, :] = _dot_nt(wkt_ref[j], xc_bf).astype(BF16)
        gates = gates + _dot(q_bf, iq_ref[cs, :]) + _dot(k_bf, ik_ref[cs, :]) + _dot(v_bf, iv_ref[cs, :])
    g_ref[...] += gates


def _block_diag(w, bw):
    nb, bi, bo = w.shape
    assert bi == bo and bw % bi == 0 and (nb * bi) % bw == 0
    per = bw // bi
    w4 = w.reshape(nb // per, per, bi, bo)
    eye = jnp.eye(per, dtype=w.dtype)
    return jnp.einsum("cnio,nm->cnimo", w4, eye).reshape(nb // per, bw, bw)


def _mlstm_pre_call(h_bf, conv_w, conv_b, w_q, w_k, w_v, w_if, b_if, *, width, tm=512, tc=1024, bw=256):
    B, T, _ = h_bf.shape
    tm, tc = min(tm, T), min(tc, width)
    n_taps = conv_w.shape[0]
    assert tm % 8 == 0 and n_taps <= 8 and width % tc == 0 and tc % bw == 0
    n_sub = tc // bw
    wq = _block_diag(w_q, bw).astype(BF16)
    wk = _block_diag(w_k, bw)
    wkt = jnp.swapaxes(wk, 1, 2).astype(BF16)
    wk = wk.astype(BF16)
    wv = _block_diag(w_v, bw).astype(BF16)
    n_gate = w_if.shape[1]
    wif = jnp.pad(w_if, ((0, 0), (0, 128 - n_gate))).astype(BF16)
    bif = jnp.pad(b_if.astype(F32), (0, 128 - n_gate)).reshape(1, 128)
    hb = tm // 8
    tile_spec = pl.BlockSpec((n_sub, bw, bw), lambda b, t, c: (c, 0, 0))
    act_spec = pl.BlockSpec((None, tm, tc), lambda b, t, c: (b, t, c))
    kern = functools.partial(_mlstm_pre_kernel, n_sub=n_sub, bw=bw, n_taps=n_taps)
    shp = jax.ShapeDtypeStruct((B, T, width), BF16)
    return pl.pallas_call(
        kern,
        grid=(B, T // tm, width // tc),
        in_specs=[act_spec,
                  pl.BlockSpec((None, 8, tc), lambda b, t, c: (b, jnp.maximum(t * hb - 1, 0), c)),
                  pl.BlockSpec((n_taps, tc), lambda b, t, c: (0, c)),
                  pl.BlockSpec((1, tc), lambda b, t, c: (0, c)),
                  tile_spec, tile_spec, tile_spec, tile_spec,
                  pl.BlockSpec((tc, 128), lambda b, t, c: (c, 0)),
                  pl.BlockSpec((tc, 128), lambda b, t, c: (width // tc + c, 0)),
                  pl.BlockSpec((tc, 128), lambda b, t, c: (2 * (width // tc) + c, 0)),
                  pl.BlockSpec((1, 128), lambda b, t, c: (0, 0))],
        out_specs=[act_spec, act_spec, act_spec,
                   pl.BlockSpec((None, tc, tm), lambda b, t, c: (b, c, t)),
                   act_spec,
                   pl.BlockSpec((None, tm, 128), lambda b, t, c: (b, t, 0))],
        out_shape=[shp, shp, shp, jax.ShapeDtypeStruct((B, width, T), BF16), shp,
                   jax.ShapeDtypeStruct((B, T, 128), F32)],
        compiler_params=_params("parallel", "parallel", "arbitrary"),
        name="mlstm_frontend",
    )(h_bf, h_bf, conv_w, conv_b.reshape(1, width), wq, wk, wkt, wv, wif, wif, wif, bif)


def _mlstm_kernel(q_ref, k_ref, kt_ref, v_ref, gc_ref, gr_ref, xc_ref, z_ref, ng_ref, sk_ref, o_ref,
                  c_ref, hh_ref, n_ref, m_ref, *, n_chunks, heads, k_scale, dvb):
    h = pl.program_id(1)
    L = MLSTM_CHUNK
    dk, dv = c_ref.shape

    @pl.when(pl.program_id(2) == 0)
    def _():
        c_ref[...] = jnp.zeros_like(c_ref)
        n_ref[...] = jnp.zeros_like(n_ref)
        m_ref[...] = jnp.zeros_like(m_ref)

    row = lax.broadcasted_iota(jnp.int32, (L, L), 0)
    col = lax.broadcasted_iota(jnp.int32, (L, L), 1)
    causal = row >= col
    glane = lax.broadcasted_iota(jnp.int32, (L, gc_ref.shape[1]), 1)

    for c in range(n_chunks):
        rs = slice(c * L, (c + 1) * L)
        qc = q_ref[rs, :]
        kc = k_ref[rs, :]
        vc = v_ref[rs, :]
        gcol = gc_ref[rs, :]
        i_col = jnp.sum(jnp.where(glane == h, gcol, 0.0), axis=1, keepdims=True)
        f_col = jnp.sum(jnp.where(glane == heads + h, gcol, 0.0), axis=1, keepdims=True)
        i_row = gr_ref[0:1, rs]
        f_row = gr_ref[1:2, rs]
        lf_col = _log_sigmoid(f_col)
        lf_row = _log_sigmoid(f_row)
        b_col = jnp.sum(jnp.where(causal, lf_row, 0.0), axis=1, keepdims=True)
        b_row = jnp.sum(jnp.where(row <= col, lf_col, 0.0), axis=0, keepdims=True)
        b_last = b_col[L - 1:L, :]
        m_prev = m_ref[0:1, 0:1]

        d = jnp.where(causal, b_col - b_row + i_row, NEG_BIG)
        inter_log = b_col + m_prev
        m_i = jnp.maximum(inter_log, jnp.max(d, axis=1, keepdims=True))
        w_intra = jnp.exp(d - m_i)
        w_inter = jnp.exp(inter_log - m_i) * k_scale
        scores = _dot_nt(qc, kc) * k_scale * w_intra
        n_row = n_ref[...]
        den = (jnp.sum(scores, axis=1, keepdims=True)
               + w_inter * jnp.sum(qc.astype(F32) * n_row, axis=1, keepdims=True))
        inv = 1.0 / jnp.maximum(jnp.abs(den), jnp.exp(-m_i))
        scores_bf = scores.astype(BF16)

        log_wj_row = b_last - b_row + i_row
        log_wj_col = b_last - b_col + i_col
        m_new = jnp.maximum(b_last + m_prev, jnp.max(log_wj_row, axis=1, keepdims=True))
        wj_row = jnp.exp(log_wj_row - m_new)
        wj_col = jnp.exp(log_wj_col - m_new)
        dec = jnp.exp(b_last + m_prev - m_new)
        ktw = (kt_ref[:, rs].astype(F32) * wj_row).astype(BF16)

        for j in range(dv // dvb):
            cs = slice(j * dvb, (j + 1) * dvb)
            c_old = c_ref[:, cs]
            num = _dot(scores_bf, vc[:, cs]) + w_inter * _dot(qc, c_old.astype(BF16))
            hh_ref[:, cs] = num * inv
            c_ref[:, cs] = c_old * dec + _dot(ktw, vc[:, cs])
        n_ref[...] = n_row * dec + jnp.sum(kc.astype(F32) * wj_col, axis=0, keepdims=True)
        m_ref[...] = jnp.broadcast_to(m_new, m_ref.shape)

        hh = hh_ref[...]
        mu = jnp.mean(hh, axis=-1, keepdims=True)
        hc = hh - mu
        var = jnp.mean(hc * hc, axis=-1, keepdims=True)
        hn = hc * lax.rsqrt(var + 1e-6) * ng_ref[...]
        out = (hn + sk_ref[...] * xc_ref[rs, :].astype(F32)) * _silu(z_ref[rs, :].astype(F32))
        o_ref[rs, :] = out.astype(o_ref.dtype)


def _mlstm_call(q, k, kt, v, gates, gates_rows, xc, h_bf, norm_g, skip, *, heads, z_col, tb=MLSTM_CHUNK):
    B, T, width = q.shape
    dh = width // heads
    tb = min(tb, T)
    assert tb % MLSTM_CHUNK == 0 and z_col % dh == 0
    dvb = min(512, dh)
    hs = pl.BlockSpec((None, tb, dh), lambda b, h, t: (b, t, h))
    kern = functools.partial(_mlstm_kernel, n_chunks=tb // MLSTM_CHUNK, heads=heads,
                             k_scale=dh ** -0.5, dvb=dvb)
    return pl.pallas_call(
        kern,
        grid=(B, heads, T // tb),
        in_specs=[hs, hs,
                  pl.BlockSpec((None, dh, tb), lambda b, h, t: (b, h, t)),
                  hs,
                  pl.BlockSpec((None, tb, gates.shape[2]), lambda b, h, t: (b, t, 0)),
                  pl.BlockSpec((None, None, 2, tb), lambda b, h, t: (b, h, 0, t)),
                  hs,
                  pl.BlockSpec((None, tb, dh), lambda b, h, t: (b, t, z_col // dh + h)),
                  pl.BlockSpec((1, dh), lambda b, h, t: (0, h)),
                  pl.BlockSpec((1, dh), lambda b, h, t: (0, h))],
        out_specs=hs,
        out_shape=jax.ShapeDtypeStruct((B, T, width), BF16),
        scratch_shapes=[pltpu.VMEM((dh, dh), F32), pltpu.VMEM((MLSTM_CHUNK, dh), F32),
                        pltpu.VMEM((1, dh), F32), pltpu.VMEM((8, 128), F32)],
        compiler_params=_params("parallel", "parallel", "arbitrary"),
        name="mlstm_chunks",
    )(q, k, kt, v, gates, gates_rows, xc, h_bf, norm_g, skip)


def _mem_kv(mem_bf, w_k, w_v, B):
    mw = w_k.shape[1]
    k = _matmul(mem_bf, w_k.astype(BF16), BF16).reshape(B, -1, mw)
    v = _matmul(mem_bf, w_v.astype(BF16), BF16).reshape(B, -1, mw)
    return k, v


def _even_layer(x, x_bf, emit_bf16, mem_bf, lb, w_in, gla_w_a2, gla_b_a, gla_norm_g, hgrn_norm_g,
                mem_w_k, mem_w_v, w_out, ln_g, ln_b, alpha):
    B, T, D = x.shape
    lowrank, kw = gla_w_a2.shape
    gdv = gla_norm_g.shape[0]
    gw = GLA_HEADS * gdv
    gdk = kw // GLA_HEADS
    fw = lb.shape[0]
    hheads = fw // HGRN_EXPAND
    hdv = hgrn_norm_g.shape[0]
    hw = hheads * hdv
    mw = mem_w_k.shape[1]
    sizes = (kw, kw, gw, gw, lowrank, fw, fw, hw, hw, mw, mw)
    assert sum(sizes) == w_in.shape[1]
    bounds = [0]
    for s in sizes:
        bounds.append(bounds[-1] + s)
    ga_w = 256
    w_a = _window_cast(w_in, 0, bounds[4])
    w_b = _window_cast(w_in, bounds[5], bounds[-1] - bounds[5])
    w_ga = jnp.pad(w_in[:, bounds[4]:bounds[5]], ((0, 0), (0, ga_w - lowrank))).astype(BF16)
    offs_a = {"gq": 0, "gk": kw, "gv": 2 * kw, "gg": 2 * kw + gw}
    offs_b = {"hq": 0, "hi": fw, "hg": fw + hw, "mq": fw + 2 * hw, "mg": fw + 2 * hw + mw}

    x2d = x.reshape(B * T, D)
    if x_bf is None:
        x_bf = x2d.astype(BF16)
    h_a = _matmul(x_bf, w_a, BF16)
    h_b = _matmul(x_bf, w_b, BF16, cols=[(0, fw), (2 * fw, bounds[-1] - bounds[5] - 2 * fw)])
    h_hf = _matmul(x_bf, w_b, F32, cols=[(fw, fw)])
    h_ga = _matmul(x_bf, w_ga, F32)
    h_a, h_b, h_hf, h_ga = (h.reshape(B, T, -1) for h in (h_a, h_b, h_hf, h_ga))

    wa_pad = jnp.pad(gla_w_a2, ((0, ga_w - lowrank), (0, 0))).astype(BF16)
    gla_out = _gla_call(h_a, h_ga, wa_pad, gla_b_a.reshape(1, kw), gla_norm_g.reshape(1, gdv),
                        offs=offs_a, ga_col=0, dk=gdk, dv=gdv, tb=512)
    hgrn_out = _hgrn_call(h_b, h_hf, lb.reshape(1, fw), hgrn_norm_g.reshape(1, hdv),
                          offs=offs_b, heads=hheads, dk=HGRN_EXPAND, dv=hdv, tb=512)
    k_mem, v_mem = _mem_kv(mem_bf, mem_w_k, mem_w_v, B)
    mem_out = _memattn_call(h_b, k_mem, v_mem, q_col=offs_b["mq"], g_col=offs_b["mg"])

    acts = [a.reshape(B * T, -1) for a in (gla_out, hgrn_out, mem_out)]
    y, y_bf = _outproj_ln(acts, w_out.astype(BF16), x2d, ln_g.reshape(1, D), ln_b.reshape(1, D), alpha, emit_bf16)
    return y.reshape(B, T, D), y_bf


def _odd_layer(x, x_bf, emit_bf16, mem_bf, w_in, conv_w, conv_b, w_q, w_k, w_v, w_if, b_if, mh_norm_g, skip,
               mem_w_k, mem_w_v, w_out, ln_g, ln_b, alpha):
    B, T, D = x.shape
    width = conv_w.shape[1]
    heads = b_if.shape[0] // 2
    mw = mem_w_k.shape[1]
    assert w_in.shape[1] == 2 * width + 2 * mw

    x2d = x.reshape(B * T, D)
    if x_bf is None:
        x_bf = x2d.astype(BF16)
    h_bf = _matmul(x_bf, w_in.astype(BF16), BF16).reshape(B, T, -1)

    xc, q, k, kt, v, gates = _mlstm_pre_call(h_bf, conv_w, conv_b, w_q, w_k, w_v, w_if, b_if, width=width)
    gates_rows = jnp.transpose(gates[:, :, :2 * heads].reshape(B, T, 2, heads), (0, 3, 2, 1))
    mlstm_out = _mlstm_call(q, k, kt, v, gates, gates_rows, xc, h_bf,
                            mh_norm_g.reshape(1, width), skip.reshape(1, width), heads=heads, z_col=width)
    k_mem, v_mem = _mem_kv(mem_bf, mem_w_k, mem_w_v, B)
    mem_out = _memattn_call(h_bf, k_mem, v_mem, q_col=2 * width, g_col=2 * width + mw)

    acts = [a.reshape(B * T, -1) for a in (mlstm_out, mem_out)]
    y, y_bf = _outproj_ln(acts, w_out.astype(BF16), x2d, ln_g.reshape(1, D), ln_b.reshape(1, D), alpha, emit_bf16)
    return y.reshape(B, T, D), y_bf


def kernel(x, mem, hgrn_lb_logits, ev_w_in, ev_gla_w_a2, ev_gla_b_a, ev_gla_norm_g, ev_hgrn_norm_g, ev_mem_w_k, ev_mem_w_v, ev_w_out, ev_ln_g, ev_ln_b, od_w_in, od_conv_w, od_conv_b, od_w_q, od_w_k, od_w_v, od_w_if, od_b_if, od_mh_norm_g, od_skip, od_mem_w_k, od_mem_w_v, od_w_out, od_ln_g, od_ln_b):
    depth = ev_w_in.shape[0] + od_w_in.shape[0]
    alpha = (2 * depth) ** 0.25
    B, ml, D = mem.shape
    mem_bf = mem.reshape(B * ml, D).astype(BF16)
    lb_all = jnp.cumsum(jax.nn.softmax(hgrn_lb_logits.astype(F32), axis=0), axis=0)
    x_bf = None
    for layer in range(depth):
        i = layer // 2
        emit_bf16 = layer + 1 < depth
        if layer % 2 == 0:
            x, x_bf = _even_layer(x, x_bf, emit_bf16, mem_bf, lb_all[layer], ev_w_in[i], ev_gla_w_a2[i],
                                  ev_gla_b_a[i], ev_gla_norm_g[i], ev_hgrn_norm_g[i], ev_mem_w_k[i],
                                  ev_mem_w_v[i], ev_w_out[i], ev_ln_g[i], ev_ln_b[i], alpha)
        else:
            x, x_bf = _odd_layer(x, x_bf, emit_bf16, mem_bf, od_w_in[i], od_conv_w[i], od_conv_b[i], od_w_q[i],
                                 od_w_k[i], od_w_v[i], od_w_if[i], od_b_if[i], od_mh_norm_g[i], od_skip[i],
                                 od_mem_w_k[i], od_mem_w_v[i], od_w_out[i], od_ln_g[i], od_ln_b[i], alpha)
    return x
```

```python
import functools

import jax
import jax.numpy as jnp
import numpy as np
from jax import lax
from jax.experimental import pallas as pl
from jax.experimental.pallas import tpu as pltpu

F32 = jnp.float32
BF16 = jnp.bfloat16

GLA_HEADS = 4
GLA_TAU = 16.0
HGRN_EXPAND = 128
MEM_HEADS = 4

LIN_CHUNK = 128
MLSTM_CHUNK = 256

V7X_VMEM_LIMIT_BYTES = 56 * 1024 * 1024
NEG_BIG = -1e30
LOG2_E = 1.4426950408889634


def _sigmoid(x):
    return 1.0 / (1.0 + jnp.exp(-x))


def _silu(x):
    return x * _sigmoid(x)


def _log_sigmoid(x):
    return jnp.minimum(x, 0.0) - jnp.log(1.0 + jnp.exp(-jnp.abs(x)))


def _dot(a, b):
    return jnp.dot(a, b, preferred_element_type=F32)


def _dot_nt(a, b):
    return lax.dot_general(a, b, (((1,), (1,)), ((), ())), preferred_element_type=F32)


def _dot_tn(a, b):
    return lax.dot_general(a, b, (((0,), (0,)), ((), ())), preferred_element_type=F32)


def _params(*sem):
    return pltpu.CompilerParams(dimension_semantics=sem, vmem_limit_bytes=V7X_VMEM_LIMIT_BYTES)


def _mm_kernel(a_ref, w_ref, o_ref):
    o_ref[...] = _dot(a_ref[...], w_ref[...]).astype(o_ref.dtype)


def _matmul(a, w, out_dtype, tm=1024, tn=1024, cols=None):
    m, k = a.shape
    cols = [(0, w.shape[1])] if cols is None else list(cols)
    tm = min(tm, m)
    while any(v % tn for rng in cols for v in rng):
        tn //= 2
    assert m % tm == 0 and tn % 128 == 0, (m, cols, tm, tn)
    n_out = sum(n for _, n in cols)

    def w_block(i, j):
        blk, done = None, 0
        for first, n in cols:
            here = first // tn + (j - done)
            blk = here if blk is None else jnp.where(j >= done, here, blk)
            done += n // tn
        return 0, blk

    return pl.pallas_call(
        _mm_kernel,
        grid=(m // tm, n_out // tn),
        in_specs=[pl.BlockSpec((tm, k), lambda i, j: (i, 0)),
                  pl.BlockSpec((k, tn), w_block)],
        out_specs=pl.BlockSpec((tm, tn), lambda i, j: (i, j)),
        out_shape=jax.ShapeDtypeStruct((m, n_out), out_dtype),
        compiler_params=_params("parallel", "arbitrary"),
        name="proj_matmul",
    )(a, w)


def _window_cast_kernel(*refs, shift):
    if shift:
        main_ref, tail_ref, o_ref = refs
        full = jnp.concatenate([main_ref[...], tail_ref[...]], axis=1)
        full = pltpu.roll(full, full.shape[1] - shift, axis=1)
        o_ref[...] = full[:, :o_ref.shape[1]].astype(o_ref.dtype)
    else:
        main_ref, o_ref = refs
        o_ref[...] = main_ref[...].astype(o_ref.dtype)


def _window_cast(w_stack, layer, col0, ncols, tk=512, tc=1024):
    _, k, n = w_stack.shape
    tk = min(tk, k)
    shift = col0 % 128
    base = col0 - shift
    while ncols % tc or base % tc:
        tc //= 2
    assert k % tk == 0 and tc % 128 == 0 and col0 + ncols <= n, (col0, ncols, tc)
    in_specs = [pl.BlockSpec((None, tk, tc), lambda i, j: (layer, i, base // tc + j))]
    operands = [w_stack]
    if shift:
        in_specs.append(pl.BlockSpec((None, tk, 128), lambda i, j: (layer, i, (base + (j + 1) * tc) // 128)))
        operands.append(w_stack)
    return pl.pallas_call(
        functools.partial(_window_cast_kernel, shift=shift),
        grid=(k // tk, ncols // tc),
        in_specs=in_specs,
        out_specs=pl.BlockSpec((tk, tc), lambda i, j: (i, j)),
        out_shape=jax.ShapeDtypeStruct((k, ncols), BF16),
        compiler_params=_params("parallel", "parallel"),
        name="weight_window_cast",
    )(*operands)


def _lin_constants(L):
    i = np.arange(L)[:, None]
    t = np.arange(L)[None, :]
    sums = [t <= i]
    masks = [i == t]
    m = L // 2
    while m >= 1:
        r = (i // (2 * m)) * 2 * m + m - 1
        second = (i & m) != 0
        sums.append(np.where(second, (t > r) & (t <= i), (t > i) & (t <= r)))
        masks.append((i // (2 * m) == t // (2 * m)) & second & ((t & m) == 0))
        m //= 2
    sums.append(t > i)
    s = np.concatenate(sums, axis=0).astype(np.float32)
    return (jnp.asarray(np.concatenate([s, s], axis=1), BF16),
            jnp.asarray(np.stack(masks).astype(np.float32)))


def _lin_attn_chunk(q, k, v_bf, g, st_ref, sums_ref, mask_ref, nh):
    L, wk = q.shape
    dk = wk // nh
    dv = v_bf.shape[1] // nh
    n_lvl = mask_ref.shape[0] - 1
    g2 = g * LOG2_E
    hi = g2.astype(BF16)
    lo = (g2 - hi.astype(F32)).astype(BF16)
    p = jnp.exp2(_dot(sums_ref[...], jnp.concatenate([hi, lo], axis=0)))
    p_last = p[L - 1:L, :]
    qd = (q * p[0:L]).astype(BF16)
    kd = (k * p[(n_lvl + 1) * L:(n_lvl + 2) * L]).astype(BF16)
    q_bf = q.astype(BF16)
    k_bf = k.astype(BF16)
    rowi = lax.broadcasted_iota(jnp.int32, (L, 1), 0)
    xs = []
    for lvl in range(n_lvl):
        second = (rowi & (L >> (lvl + 1))) != 0
        xs.append((jnp.where(second, q, k) * p[(lvl + 1) * L:(lvl + 2) * L]).astype(BF16))
    outs = []
    for h in range(nh):
        ck = slice(h * dk, (h + 1) * dk)
        cv = slice(h * dv, (h + 1) * dv)
        a = _dot_nt(q_bf[:, ck], k_bf[:, ck]) * mask_ref[0]
        for lvl in range(n_lvl):
            x = xs[lvl][:, ck]
            a = a + _dot_nt(x, x) * mask_ref[lvl + 1]
        st = st_ref[h]
        outs.append(_dot_nt(qd[:, ck], st.astype(BF16)) + _dot(a.astype(BF16), v_bf[:, cv]))
        st_ref[h] = st * p_last[:, ck] + _dot_tn(v_bf[:, cv], kd[:, ck])
    return outs


def _rms_gate(o, norm_g, gate, eps=1e-6):
    ms = jnp.mean(o * o, axis=-1, keepdims=True)
    return o * lax.rsqrt(ms + eps) * norm_g * _silu(gate)


def _gla_kernel(q_ref, k_ref, v_ref, gg_ref, ga_ref, wa_ref, ba_ref, ng_ref, sums_ref, mask_ref, o_ref,
                st_ref, *, n_chunks, q_scale):
    @pl.when(pl.program_id(2) == 0)
    def _():
        st_ref[...] = jnp.zeros_like(st_ref)

    def chunk(c, carry):
        rows = pl.ds(pl.multiple_of(c * LIN_CHUNK, LIN_CHUNK), LIN_CHUNK)
        q = q_ref[rows, :].astype(F32) * q_scale
        k = k_ref[rows, :].astype(F32)
        a_pre = _dot(ga_ref[rows, :].astype(BF16), wa_ref[...]) + ba_ref[...]
        g = _log_sigmoid(a_pre) * (1.0 / GLA_TAU)
        (o,) = _lin_attn_chunk(q, k, v_ref[rows, :], g, st_ref, sums_ref, mask_ref, 1)
        o_ref[rows, :] = _rms_gate(o, ng_ref[...], gg_ref[rows, :].astype(F32)).astype(o_ref.dtype)
        return carry

    lax.fori_loop(0, n_chunks, chunk, 0, unroll=4)


def _hgrn_kernel(q_ref, f_ref, v_ref, hg_ref, lb_ref, ng_ref, sums_ref, mask_ref, o_ref,
                 st_ref, *, n_chunks, nh):
    dv = ng_ref.shape[1]

    @pl.when(pl.program_id(2) == 0)
    def _():
        st_ref[...] = jnp.zeros_like(st_ref)

    def chunk(c, carry):
        rows = pl.ds(pl.multiple_of(c * LIN_CHUNK, LIN_CHUNK), LIN_CHUNK)
        lb = lb_ref[...]
        f = lb + (1.0 - lb) * _sigmoid(f_ref[rows, :])
        q = _silu(q_ref[rows, :].astype(F32))
        outs = _lin_attn_chunk(q, 1.0 - f, v_ref[rows, :], jnp.log(f), st_ref, sums_ref, mask_ref, nh)
        gate = hg_ref[rows, :].astype(F32)
        for h, o in enumerate(outs):
            cv = slice(h * dv, (h + 1) * dv)
            o_ref[rows, cv] = _rms_gate(o, ng_ref[...], gate[:, cv]).astype(o_ref.dtype)
        return carry

    lax.fori_loop(0, n_chunks, chunk, 0, unroll=4)


def _col_spec(tb, width, col0):
    assert col0 % width == 0, (col0, width)
    base = col0 // width
    return pl.BlockSpec((None, tb, width), lambda b, h, t: (b, t, base + h))


def _const_specs(*arrays):
    return [pl.BlockSpec(a.shape, lambda b, h, t, nd=a.ndim: (0,) * nd) for a in arrays]


def _gla_call(h_bf, h_f, wa_pad, b_a, norm_g, *, offs, ga_col, dk, dv, tb):
    B, T, _ = h_bf.shape
    tb = min(tb, T)
    ga_w = wa_pad.shape[0]
    assert ga_col % ga_w == 0
    consts = _lin_constants(LIN_CHUNK)
    kern = functools.partial(_gla_kernel, n_chunks=tb // LIN_CHUNK, q_scale=dk ** -0.5)
    return pl.pallas_call(
        kern,
        grid=(B, GLA_HEADS, T // tb),
        in_specs=[_col_spec(tb, dk, offs["gq"]), _col_spec(tb, dk, offs["gk"]),
                  _col_spec(tb, dv, offs["gv"]), _col_spec(tb, dv, offs["gg"]),
                  pl.BlockSpec((None, tb, ga_w), lambda b, h, t: (b, t, ga_col // ga_w)),
                  pl.BlockSpec((ga_w, dk), lambda b, h, t: (0, h)),
                  pl.BlockSpec((1, dk), lambda b, h, t: (0, h)),
                  pl.BlockSpec((1, dv), lambda b, h, t: (0, 0))] + _const_specs(*consts),
        out_specs=pl.BlockSpec((None, tb, dv), lambda b, h, t: (b, t, h)),
        out_shape=jax.ShapeDtypeStruct((B, T, GLA_HEADS * dv), BF16),
        scratch_shapes=[pltpu.VMEM((1, dv, dk), F32)],
        compiler_params=_params("parallel", "parallel", "arbitrary"),
        name="gla_chunks",
    )(h_bf, h_bf, h_bf, h_bf, h_f, wa_pad, b_a, norm_g, *consts)


def _hgrn_call(h_bf, h_f, lb, norm_g, *, offs, heads, dk, dv, tb, nh=4):
    B, T, _ = h_bf.shape
    tb = min(tb, T)
    nh = min(nh, heads)
    assert heads % nh == 0
    consts = _lin_constants(LIN_CHUNK)
    kern = functools.partial(_hgrn_kernel, n_chunks=tb // LIN_CHUNK, nh=nh)
    return pl.pallas_call(
        kern,
        grid=(B, heads // nh, T // tb),
        in_specs=[_col_spec(tb, nh * dk, offs["hq"]),
                  pl.BlockSpec((None, tb, nh * dk), lambda b, h, t: (b, t, h)),
                  _col_spec(tb, nh * dv, offs["hi"]), _col_spec(tb, nh * dv, offs["hg"]),
                  pl.BlockSpec((1, nh * dk), lambda b, h, t: (0, h)),
                  pl.BlockSpec((1, dv), lambda b, h, t: (0, 0))] + _const_specs(*consts),
        out_specs=pl.BlockSpec((None, tb, nh * dv), lambda b, h, t: (b, t, h)),
        out_shape=jax.ShapeDtypeStruct((B, T, heads * dv), BF16),
        scratch_shapes=[pltpu.VMEM((nh, dv, dk), F32)],
        compiler_params=_params("parallel", "parallel", "arbitrary"),
        name="hgrn_chunks",
    )(h_bf, h_f, h_bf, h_bf, lb, norm_g, *consts)


def _memattn_kernel(q_ref, g_ref, k_ref, v_ref, o_ref, *, dh):
    scale = dh ** -0.5
    for h in range(MEM_HEADS):
        cs = slice(h * dh, (h + 1) * dh)
        s = _dot_nt(q_ref[:, cs], k_ref[:, cs]) * scale
        s = s - jnp.max(s, axis=-1, keepdims=True)
        p = jnp.exp(s)
        p = p / jnp.sum(p, axis=-1, keepdims=True)
        o = _dot(p.astype(BF16), v_ref[:, cs])
        o_ref[:, cs] = (o * _silu(g_ref[:, cs].astype(F32))).astype(o_ref.dtype)


def _memattn_call(h_bf, k_mem, v_mem, *, q_col, g_col, tm=512):
    B, T, _ = h_bf.shape
    _, ml, mw = k_mem.shape
    tm = min(tm, T)
    assert q_col % mw == 0 and g_col % mw == 0
    return pl.pallas_call(
        functools.partial(_memattn_kernel, dh=mw // MEM_HEADS),
        grid=(B, T // tm),
        in_specs=[pl.BlockSpec((None, tm, mw), lambda b, t: (b, t, q_col // mw)),
                  pl.BlockSpec((None, tm, mw), lambda b, t: (b, t, g_col // mw)),
                  pl.BlockSpec((None, ml, mw), lambda b, t: (b, 0, 0)),
                  pl.BlockSpec((None, ml, mw), lambda b, t: (b, 0, 0))],
        out_specs=pl.BlockSpec((None, tm, mw), lambda b, t: (b, t, 0)),
        out_shape=jax.ShapeDtypeStruct((B, T, mw), BF16),
        compiler_params=_params("parallel", "parallel"),
        name="mem_attention",
    )(h_bf, h_bf, k_mem, v_mem)


def _outproj_kernel(*refs, segs, nk, n_blocks, alpha, eps, emit_bf16):
    ns = len(segs)
    a_refs = refs[:ns]
    if emit_bf16:
        w_ref, x_hbm, g_ref, b_ref, o_hbm, obf_hbm, acc, x_sem, o_sem, bf_buf, bf_sem = refs[ns:]
    else:
        w_ref, x_hbm, g_ref, b_ref, o_hbm, acc, x_sem, o_sem = refs[ns:]
    i = pl.program_id(0)
    k = pl.program_id(1)
    _, tm, d = acc.shape
    rb, cb = min(128, tm), min(512, d)
    n_rb = tm // rb
    slot = i % 2
    o_ref = acc.at[slot]

    def rows_of(blk):
        return pl.ds(pl.multiple_of(blk * tm, tm), tm)

    def x_copy(blk, s):
        return pltpu.make_async_copy(x_hbm.at[rows_of(blk), :], acc.at[s], x_sem.at[s])

    def o_copy(blk, s):
        return pltpu.make_async_copy(acc.at[s], o_hbm.at[rows_of(blk), :], o_sem.at[s])

    def bf_copy(r, bslot):
        row0 = pl.multiple_of(i * tm + r * rb, rb)
        return pltpu.make_async_copy(bf_buf.at[bslot], obf_hbm.at[pl.ds(row0, rb), :], bf_sem.at[bslot])

    @pl.when((i == 0) & (k == 0))
    def _():
        x_copy(0, 0).start()

    @pl.when(k == 0)
    def _():
        x_copy(i, slot).wait()
        a = a_refs[0][...]
        for c in range(0, d, cb):
            o_ref[:, c:c + cb] = alpha * o_ref[:, c:c + cb] + _dot(a, w_ref[:, c:c + cb])

    for a_ref, (start, n) in zip(a_refs, segs):
        @pl.when((k >= max(start, 1)) & (k < start + n))
        def _(a_ref=a_ref):
            a = a_ref[...]
            for c in range(0, d, cb):
                o_ref[:, c:c + cb] += _dot(a, w_ref[:, c:c + cb])

    @pl.when(k == nk - 1)
    def _():
        @pl.when(i >= 1)
        def _():
            o_copy(i - 1, 1 - slot).wait()

        @pl.when(i + 1 < n_blocks)
        def _():
            x_copy(i + 1, 1 - slot).start()

        def norm(r, carry):
            rows = pl.ds(pl.multiple_of(r * rb, rb), rb)
            y = o_ref[rows, :]
            mu = jnp.mean(y, axis=-1, keepdims=True)
            yc = y - mu
            var = jnp.mean(yc * yc, axis=-1, keepdims=True)
            out = yc * lax.rsqrt(var + eps) * g_ref[...] + b_ref[...]
            o_ref[rows, :] = out
            if emit_bf16:
                n_slot = bf_buf.shape[0]
                bslot = r % n_slot

                @pl.when(r >= n_slot)
                def _():
                    bf_copy(r - n_slot, bslot).wait()

                bf_buf[bslot] = out.astype(bf_buf.dtype)
                bf_copy(r, bslot).start()
            return carry

        lax.fori_loop(0, n_rb, norm, 0)
        o_copy(i, slot).start()
        if emit_bf16:
            for r in range(max(n_rb - bf_buf.shape[0], 0), n_rb):
                bf_copy(r, r % bf_buf.shape[0]).wait()

        @pl.when(i == n_blocks - 1)
        def _():
            o_copy(i, slot).wait()


def _outproj_ln(acts, w_bf, x2d, ln_g, ln_b, alpha, emit_bf16, tm=1024, tk=512):
    m, d = x2d.shape
    tm = min(tm, m)
    segs, start = [], 0
    for a in acts:
        assert a.shape[0] == m and a.shape[1] % tk == 0
        segs.append((start, a.shape[1] // tk))
        start += a.shape[1] // tk
    nk = start
    assert nk * tk == w_bf.shape[0]

    def a_spec(s0, n):
        return pl.BlockSpec((tm, tk), lambda i, k: (i, jnp.clip(k - s0, 0, n - 1)))

    kern = functools.partial(_outproj_kernel, segs=tuple(segs), nk=nk, n_blocks=m // tm, alpha=alpha, eps=1e-5,
                             emit_bf16=emit_bf16)
    rb = min(128, tm)
    out_specs = [pl.BlockSpec(memory_space=pl.ANY)]
    out_shape = [jax.ShapeDtypeStruct((m, d), F32)]
    scratch = [pltpu.VMEM((2, tm, d), F32), pltpu.SemaphoreType.DMA((2,)), pltpu.SemaphoreType.DMA((2,))]
    if emit_bf16:
        out_specs.append(pl.BlockSpec(memory_space=pl.ANY))
        out_shape.append(jax.ShapeDtypeStruct((m, d), BF16))
        n_slot = 4
        scratch += [pltpu.VMEM((n_slot, rb, d), BF16), pltpu.SemaphoreType.DMA((n_slot,))]
    outs = pl.pallas_call(
        kern,
        grid=(m // tm, nk),
        in_specs=[a_spec(s0, n) for s0, n in segs] + [
            pl.BlockSpec((tk, d), lambda i, k: (k, 0)),
            pl.BlockSpec(memory_space=pl.ANY),
            pl.BlockSpec((1, d), lambda i, k: (0, 0)),
            pl.BlockSpec((1, d), lambda i, k: (0, 0))],
        out_specs=out_specs,
        out_shape=out_shape,
        scratch_shapes=scratch,
        compiler_params=_params("arbitrary", "arbitrary"),
        name="outproj_layernorm",
    )(*acts, w_bf, x2d, ln_g, ln_b)
    return (outs[0], outs[1]) if emit_bf16 else (outs[0], None)


def _mlstm_pre_kernel(x_ref, halo_ref, cw_ref, cb_ref, wq_ref, wk_ref, wkt_ref, wv_ref,
                      iq_ref, ik_ref, iv_ref, bif_ref,
                      xc_ref, q_ref, k_ref, kt_ref, v_ref, g_ref, *, n_sub, bw, n_taps):
    t = pl.program_id(1)
    c = pl.program_id(2)
    tm = x_ref.shape[0]

    @pl.when(c == 0)
    def _():
        g_ref[...] = jnp.broadcast_to(bif_ref[...], g_ref.shape)

    gates = jnp.zeros(g_ref.shape, F32)
    for j in range(n_sub):
        cs = slice(j * bw, (j + 1) * bw)
        x_bf = x_ref[:, cs]
        x32 = x_bf.astype(F32)
        halo = jnp.where(t > 0, halo_ref[:, cs].astype(F32), 0.0)
        ext = jnp.concatenate([halo, x32], axis=0)
        acc = jnp.broadcast_to(cb_ref[:, cs], (tm, bw))
        for tap in range(n_taps):
            sh = 8 - (n_taps - 1) + tap
            acc = acc + ext[sh:sh + tm, :] * cw_ref[tap:tap + 1, cs]
        xc_bf = _silu(acc).astype(BF16)
        xc_ref[:, cs] = xc_bf
        q_bf = _dot(xc_bf, wq_ref[j]).astype(BF16)
        k_bf = _dot(xc_bf, wk_ref[j]).astype(BF16)
        v_bf = _dot(x_bf, wv_ref[j]).astype(BF16)
        q_ref[:, cs] = q_bf
        k_ref[:, cs] = k_bf
        v_ref[:, cs] = v_bf
        kt_ref[cs, :] = _dot_nt(wkt_ref[j], xc_bf).astype(BF16)
        gates = gates + _dot(q_bf, iq_ref[cs, :]) + _dot(k_bf, ik_ref[cs, :]) + _dot(v_bf, iv_ref[cs, :])
    g_ref[...] += gates


def _block_diag(w, bw):
    nb, bi, bo = w.shape
    assert bi == bo and bw % bi == 0 and (nb * bi) % bw == 0
    per = bw // bi
    w4 = w.reshape(nb // per, per, bi, bo)
    eye = jnp.eye(per, dtype=w.dtype)
    return jnp.einsum("cnio,nm->cnimo", w4, eye).reshape(nb // per, bw, bw)


def _mlstm_pre_call(h_bf, conv_w, conv_b, w_q, w_k, w_v, w_if, b_if, *, width, tm=512, tc=1024, bw=256):
    B, T, _ = h_bf.shape
    tm, tc = min(tm, T), min(tc, width)
    n_taps = conv_w.shape[0]
    assert tm % 8 == 0 and n_taps <= 8 and width % tc == 0 and tc % bw == 0
    n_sub = tc // bw
    wq = _block_diag(w_q, bw).astype(BF16)
    wk = _block_diag(w_k, bw)
    wkt = jnp.swapaxes(wk, 1, 2).astype(BF16)
    wk = wk.astype(BF16)
    wv = _block_diag(w_v, bw).astype(BF16)
    n_gate = w_if.shape[1]
    wif = jnp.pad(w_if, ((0, 0), (0, 128 - n_gate))).astype(BF16)
    bif = jnp.pad(b_if.astype(F32), (0, 128 - n_gate)).reshape(1, 128)
    hb = tm // 8
    tile_spec = pl.BlockSpec((n_sub, bw, bw), lambda b, t, c: (c, 0, 0))
    act_spec = pl.BlockSpec((None, tm, tc), lambda b, t, c: (b, t, c))
    kern = functools.partial(_mlstm_pre_kernel, n_sub=n_sub, bw=bw, n_taps=n_taps)
    shp = jax.ShapeDtypeStruct((B, T, width), BF16)
    return pl.pallas_call(
        kern,
        grid=(B, T // tm, width // tc),
        in_specs=[act_spec,
                  pl.BlockSpec((None, 8, tc), lambda b, t, c: (b, jnp.maximum(t * hb - 1, 0), c)),
                  pl.BlockSpec((n_taps, tc), lambda b, t, c: (0, c)),
                  pl.BlockSpec((1, tc), lambda b, t, c: (0, c)),
                  tile_spec, tile_spec, tile_spec, tile_spec,
                  pl.BlockSpec((tc, 128), lambda b, t, c: (c, 0)),
                  pl.BlockSpec((tc, 128), lambda b, t, c: (width // tc + c, 0)),
                  pl.BlockSpec((tc, 128), lambda b, t, c: (2 * (width // tc) + c, 0)),
                  pl.BlockSpec((1, 128), lambda b, t, c: (0, 0))],
        out_specs=[act_spec, act_spec, act_spec,
                   pl.BlockSpec((None, tc, tm), lambda b, t, c: (b, c, t)),
                   act_spec,
                   pl.BlockSpec((None, tm, 128), lambda b, t, c: (b, t, 0))],
        out_shape=[shp, shp, shp, jax.ShapeDtypeStruct((B, width, T), BF16), shp,
                   jax.ShapeDtypeStruct((B, T, 128), F32)],
        compiler_params=_params("parallel", "parallel", "arbitrary"),
        name="mlstm_frontend",
    )(h_bf, h_bf, conv_w, conv_b.reshape(1, width), wq, wk, wkt, wv, wif, wif, wif, bif)


def _mlstm_kernel(q_ref, k_ref, kt_ref, v_ref, gc_ref, gr_ref, xc_ref, z_ref, ng_ref, sk_ref, o_ref,
                  c_ref, hh_ref, n_ref, m_ref, *, n_chunks, heads, k_scale, dvb):
    h = pl.program_id(1)
    L = MLSTM_CHUNK
    dk, dv = c_ref.shape

    @pl.when(pl.program_id(2) == 0)
    def _():
        c_ref[...] = jnp.zeros_like(c_ref)
        n_ref[...] = jnp.zeros_like(n_ref)
        m_ref[...] = jnp.zeros_like(m_ref)

    row = lax.broadcasted_iota(jnp.int32, (L, L), 0)
    col = lax.broadcasted_iota(jnp.int32, (L, L), 1)
    causal = row >= col
    glane = lax.broadcasted_iota(jnp.int32, (L, gc_ref.shape[1]), 1)

    for c in range(n_chunks):
        rs = slice(c * L, (c + 1) * L)
        qc = q_ref[rs, :]
        kc = k_ref[rs, :]
        vc = v_ref[rs, :]
        gcol = gc_ref[rs, :]
        i_col = jnp.sum(jnp.where(glane == h, gcol, 0.0), axis=1, keepdims=True)
        f_col = jnp.sum(jnp.where(glane == heads + h, gcol, 0.0), axis=1, keepdims=True)
        i_row = gr_ref[0:1, rs]
        f_row = gr_ref[1:2, rs]
        lf_col = _log_sigmoid(f_col)
        lf_row = _log_sigmoid(f_row)
        b_col = jnp.sum(jnp.where(causal, lf_row, 0.0), axis=1, keepdims=True)
        b_row = jnp.sum(jnp.where(row <= col, lf_col, 0.0), axis=0, keepdims=True)
        b_last = b_col[L - 1:L, :]
        m_prev = m_ref[0:1, 0:1]

        d = jnp.where(causal, b_col - b_row + i_row, NEG_BIG)
        inter_log = b_col + m_prev
        m_i = jnp.maximum(inter_log, jnp.max(d, axis=1, keepdims=True))
        w_intra = jnp.exp(d - m_i)
        w_inter = jnp.exp(inter_log - m_i) * k_scale
        scores = _dot_nt(qc, kc) * k_scale * w_intra
        n_row = n_ref[...]
        den = (jnp.sum(scores, axis=1, keepdims=True)
               + w_inter * jnp.sum(qc.astype(F32) * n_row, axis=1, keepdims=True))
        inv = 1.0 / jnp.maximum(jnp.abs(den), jnp.exp(-m_i))
        scores_bf = scores.astype(BF16)

        log_wj_row = b_last - b_row + i_row
        log_wj_col = b_last - b_col + i_col
        m_new = jnp.maximum(b_last + m_prev, jnp.max(log_wj_row, axis=1, keepdims=True))
        wj_row = jnp.exp(log_wj_row - m_new)
        wj_col = jnp.exp(log_wj_col - m_new)
        dec = jnp.exp(b_last + m_prev - m_new)
        ktw = (kt_ref[:, rs].astype(F32) * wj_row).astype(BF16)

        for j in range(dv // dvb):
            cs = slice(j * dvb, (j + 1) * dvb)
            c_old = c_ref[:, cs]
            num = _dot(scores_bf, vc[:, cs]) + w_inter * _dot(qc, c_old.astype(BF16))
            hh_ref[:, cs] = num * inv
            c_ref[:, cs] = c_old * dec + _dot(ktw, vc[:, cs])
        n_ref[...] = n_row * dec + jnp.sum(kc.astype(F32) * wj_col, axis=0, keepdims=True)
        m_ref[...] = jnp.broadcast_to(m_new, m_ref.shape)

        hh = hh_ref[...]
        mu = jnp.mean(hh, axis=-1, keepdims=True)
        hc = hh - mu
        var = jnp.mean(hc * hc, axis=-1, keepdims=True)
        hn = hc * lax.rsqrt(var + 1e-6) * ng_ref[...]
        out = (hn + sk_ref[...] * xc_ref[rs, :].astype(F32)) * _silu(z_ref[rs, :].astype(F32))
        o_ref[rs, :] = out.astype(o_ref.dtype)


def _mlstm_call(q, k, kt, v, gates, gates_rows, xc, h_bf, norm_g, skip, *, heads, z_col, tb=MLSTM_CHUNK):
    B, T, width = q.shape
    dh = width // heads
    tb = min(tb, T)
    assert tb % MLSTM_CHUNK == 0 and z_col % dh == 0
    dvb = min(512, dh)
    hs = pl.BlockSpec((None, tb, dh), lambda b, h, t: (b, t, h))
    kern = functools.partial(_mlstm_kernel, n_chunks=tb // MLSTM_CHUNK, heads=heads,
                             k_scale=dh ** -0.5, dvb=dvb)
    return pl.pallas_call(
        kern,
        grid=(B, heads, T // tb),
        in_specs=[hs, hs,
                  pl.BlockSpec((None, dh, tb), lambda b, h, t: (b, h, t)),
                  hs,
                  pl.BlockSpec((None, tb, gates.shape[2]), lambda b, h, t: (b, t, 0)),
                  pl.BlockSpec((None, None, 2, tb), lambda b, h, t: (b, h, 0, t)),
                  hs,
                  pl.BlockSpec((None, tb, dh), lambda b, h, t: (b, t, z_col // dh + h)),
                  pl.BlockSpec((1, dh), lambda b, h, t: (0, h)),
                  pl.BlockSpec((1, dh), lambda b, h, t: (0, h))],
        out_specs=hs,
        out_shape=jax.ShapeDtypeStruct((B, T, width), BF16),
        scratch_shapes=[pltpu.VMEM((dh, dh), F32), pltpu.VMEM((MLSTM_CHUNK, dh), F32),
                        pltpu.VMEM((1, dh), F32), pltpu.VMEM((8, 128), F32)],
        compiler_params=_params("parallel", "parallel", "arbitrary"),
        name="mlstm_chunks",
    )(q, k, kt, v, gates, gates_rows, xc, h_bf, norm_g, skip)


def _mem_kv(mem_bf, w_k, w_v, B):
    mw = w_k.shape[1]
    k = _matmul(mem_bf, w_k.astype(BF16), BF16).reshape(B, -1, mw)
    v = _matmul(mem_bf, w_v.astype(BF16), BF16).reshape(B, -1, mw)
    return k, v


def _even_layer(x, x_bf, emit_bf16, mem_bf, lb, w_in_stack, li, gla_w_a2, gla_b_a, gla_norm_g, hgrn_norm_g,
                mem_w_k, mem_w_v, w_out, ln_g, ln_b, alpha):
    B, T, D = x.shape
    lowrank, kw = gla_w_a2.shape
    gdv = gla_norm_g.shape[0]
    gw = GLA_HEADS * gdv
    gdk = kw // GLA_HEADS
    fw = lb.shape[0]
    hheads = fw // HGRN_EXPAND
    hdv = hgrn_norm_g.shape[0]
    hw = hheads * hdv
    mw = mem_w_k.shape[1]
    sizes = (kw, kw, gw, gw, lowrank, fw, fw, hw, hw, mw, mw)
    assert sum(sizes) == w_in_stack.shape[2]
    bounds = [0]
    for s in sizes:
        bounds.append(bounds[-1] + s)
    ga_w = 256
    w_a = _window_cast(w_in_stack, li, 0, bounds[4])
    w_b = _window_cast(w_in_stack, li, bounds[5], bounds[-1] - bounds[5])
    w_ga = jnp.pad(w_in_stack[li, :, bounds[4]:bounds[5]], ((0, 0), (0, ga_w - lowrank))).astype(BF16)
    offs_a = {"gq": 0, "gk": kw, "gv": 2 * kw, "gg": 2 * kw + gw}
    offs_b = {"hq": 0, "hi": fw, "hg": fw + hw, "mq": fw + 2 * hw, "mg": fw + 2 * hw + mw}

    x2d = x.reshape(B * T, D)
    if x_bf is None:
        x_bf = x2d.astype(BF16)
    h_a = _matmul(x_bf, w_a, BF16)
    h_b = _matmul(x_bf, w_b, BF16, cols=[(0, fw), (2 * fw, bounds[-1] - bounds[5] - 2 * fw)])
    h_hf = _matmul(x_bf, w_b, F32, cols=[(fw, fw)])
    h_ga = _matmul(x_bf, w_ga, F32)
    h_a, h_b, h_hf, h_ga = (h.reshape(B, T, -1) for h in (h_a, h_b, h_hf, h_ga))

    wa_pad = jnp.pad(gla_w_a2, ((0, ga_w - lowrank), (0, 0))).astype(BF16)
    gla_out = _gla_call(h_a, h_ga, wa_pad, gla_b_a.reshape(1, kw), gla_norm_g.reshape(1, gdv),
                        offs=offs_a, ga_col=0, dk=gdk, dv=gdv, tb=512)
    hgrn_out = _hgrn_call(h_b, h_hf, lb.reshape(1, fw), hgrn_norm_g.reshape(1, hdv),
                          offs=offs_b, heads=hheads, dk=HGRN_EXPAND, dv=hdv, tb=512)
    k_mem, v_mem = _mem_kv(mem_bf, mem_w_k, mem_w_v, B)
    mem_out = _memattn_call(h_b, k_mem, v_mem, q_col=offs_b["mq"], g_col=offs_b["mg"])

    acts = [a.reshape(B * T, -1) for a in (gla_out, hgrn_out, mem_out)]
    y, y_bf = _outproj_ln(acts, w_out.astype(BF16), x2d, ln_g.reshape(1, D), ln_b.reshape(1, D), alpha, emit_bf16)
    return y.reshape(B, T, D), y_bf


def _odd_layer(x, x_bf, emit_bf16, mem_bf, w_in, conv_w, conv_b, w_q, w_k, w_v, w_if, b_if, mh_norm_g, skip,
               mem_w_k, mem_w_v, w_out, ln_g, ln_b, alpha):
    B, T, D = x.shape
    width = conv_w.shape[1]
    heads = b_if.shape[0] // 2
    mw = mem_w_k.shape[1]
    assert w_in.shape[1] == 2 * width + 2 * mw

    x2d = x.reshape(B * T, D)
    if x_bf is None:
        x_bf = x2d.astype(BF16)
    h_bf = _matmul(x_bf, w_in.astype(BF16), BF16).reshape(B, T, -1)

    xc, q, k, kt, v, gates = _mlstm_pre_call(h_bf, conv_w, conv_b, w_q, w_k, w_v, w_if, b_if, width=width)
    gates_rows = jnp.transpose(gates[:, :, :2 * heads].reshape(B, T, 2, heads), (0, 3, 2, 1))
    mlstm_out = _mlstm_call(q, k, kt, v, gates, gates_rows, xc, h_bf,
                            mh_norm_g.reshape(1, width), skip.reshape(1, width), heads=heads, z_col=width)
    k_mem, v_mem = _mem_kv(mem_bf, mem_w_k, mem_w_v, B)
    mem_out = _memattn_call(h_bf, k_mem, v_mem, q_col=2 * width, g_col=2 * width + mw)

    acts = [a.reshape(B * T, -1) for a in (mlstm_out, mem_out)]
    y, y_bf = _outproj_ln(acts, w_out.astype(BF16), x2d, ln_g.reshape(1, D), ln_b.reshape(1, D), alpha, emit_bf16)
    return y.reshape(B, T, D), y_bf


def kernel(x, mem, hgrn_lb_logits, ev_w_in, ev_gla_w_a2, ev_gla_b_a, ev_gla_norm_g, ev_hgrn_norm_g, ev_mem_w_k, ev_mem_w_v, ev_w_out, ev_ln_g, ev_ln_b, od_w_in, od_conv_w, od_conv_b, od_w_q, od_w_k, od_w_v, od_w_if, od_b_if, od_mh_norm_g, od_skip, od_mem_w_k, od_mem_w_v, od_w_out, od_ln_g, od_ln_b):
    depth = ev_w_in.shape[0] + od_w_in.shape[0]
    alpha = (2 * depth) ** 0.25
    B, ml, D = mem.shape
    mem_bf = mem.reshape(B * ml, D).astype(BF16)
    lb_all = jnp.cumsum(jax.nn.softmax(hgrn_lb_logits.astype(F32), axis=0), axis=0)
    x_bf = None
    for layer in range(depth):
        i = layer // 2
        emit_bf16 = layer + 1 < depth
        if layer % 2 == 0:
            x, x_bf = _even_layer(x, x_bf, emit_bf16, mem_bf, lb_all[layer], ev_w_in, i, ev_gla_w_a2[i],
                                  ev_gla_b_a[i], ev_gla_norm_g[i], ev_hgrn_norm_g[i], ev_mem_w_k[i],
                                  ev_mem_w_v[i], ev_w_out[i], ev_ln_g[i], ev_ln_b[i], alpha)
        else:
            x, x_bf = _odd_layer(x, x_bf, emit_bf16, mem_bf, od_w_in[i], od_conv_w[i], od_conv_b[i], od_w_q[i],
                                 od_w_k[i], od_w_v[i], od_w_if[i], od_b_if[i], od_mh_norm_g[i], od_skip[i],
                                 od_mem_w_k[i], od_mem_w_v[i], od_w_out[i], od_ln_g[i], od_ln_b[i], alpha)
    return x
```

```python
import functools

import jax
import jax.numpy as jnp
import numpy as np
from jax import lax
from jax.experimental import pallas as pl
from jax.experimental.pallas import tpu as pltpu

F32 = jnp.float32
BF16 = jnp.bfloat16

GLA_HEADS = 4
GLA_TAU = 16.0
HGRN_EXPAND = 128
MEM_HEADS = 4

LIN_CHUNK = 128
MLSTM_CHUNK = 256

V7X_VMEM_LIMIT_BYTES = 56 * 1024 * 1024
NEG_BIG = -1e30
LOG2_E = 1.4426950408889634


def _sigmoid(x):
    return 1.0 / (1.0 + jnp.exp(-x))


def _silu(x):
    return x * _sigmoid(x)


def _log_sigmoid(x):
    return jnp.minimum(x, 0.0) - jnp.log(1.0 + jnp.exp(-jnp.abs(x)))


def _dot(a, b):
    return jnp.dot(a, b, preferred_element_type=F32)


def _dot_nt(a, b):
    return lax.dot_general(a, b, (((1,), (1,)), ((), ())), preferred_element_type=F32)


def _dot_tn(a, b):
    return lax.dot_general(a, b, (((0,), (0,)), ((), ())), preferred_element_type=F32)


def _params(*sem):
    return pltpu.CompilerParams(dimension_semantics=sem, vmem_limit_bytes=V7X_VMEM_LIMIT_BYTES)


def _mm_kernel(a_ref, w_ref, o_ref):
    o_ref[...] = _dot(a_ref[...], w_ref[...]).astype(o_ref.dtype)


def _matmul(a, w, out_dtype, tm=1024, tn=1024, cols=None):
    m, k = a.shape
    cols = [(0, w.shape[1])] if cols is None else list(cols)
    tm = min(tm, m)
    while any(v % tn for rng in cols for v in rng):
        tn //= 2
    assert m % tm == 0 and tn % 128 == 0, (m, cols, tm, tn)
    n_out = sum(n for _, n in cols)

    def w_block(i, j):
        blk, done = None, 0
        for first, n in cols:
            here = first // tn + (j - done)
            blk = here if blk is None else jnp.where(j >= done, here, blk)
            done += n // tn
        return 0, blk

    return pl.pallas_call(
        _mm_kernel,
        grid=(m // tm, n_out // tn),
        in_specs=[pl.BlockSpec((tm, k), lambda i, j: (i, 0)),
                  pl.BlockSpec((k, tn), w_block)],
        out_specs=pl.BlockSpec((tm, tn), lambda i, j: (i, j)),
        out_shape=jax.ShapeDtypeStruct((m, n_out), out_dtype),
        compiler_params=_params("parallel", "arbitrary"),
        name="proj_matmul",
    )(a, w)


def _window_cast_kernel(x_ref, o_ref):
    o_ref[...] = x_ref[0].T.astype(o_ref.dtype)


def _window_cast(w_stack, layer, col0, ncols, tr=512, tkb=512):
    wt = jnp.swapaxes(w_stack, 1, 2)
    _, n, k = wt.shape
    tkb = min(tkb, k)
    while ncols % tr:
        tr //= 2
    assert col0 % 8 == 0 and tr % 128 == 0 and k % tkb == 0 and col0 + ncols <= n, (col0, ncols, tr)
    return pl.pallas_call(
        _window_cast_kernel,
        grid=(ncols // tr, k // tkb),
        in_specs=[pl.BlockSpec((pl.Element(1), pl.Element(tr), pl.Element(tkb)),
                               lambda i, j: (layer, pl.multiple_of(col0 + i * tr, 8), pl.multiple_of(j * tkb, 128)))],
        out_specs=pl.BlockSpec((tkb, tr), lambda i, j: (j, i)),
        out_shape=jax.ShapeDtypeStruct((k, ncols), BF16),
        compiler_params=_params("parallel", "parallel"),
        name="weight_window_cast",
    )(wt)


def _lin_constants(L):
    i = np.arange(L)[:, None]
    t = np.arange(L)[None, :]
    sums = [t <= i]
    masks = [i == t]
    m = L // 2
    while m >= 1:
        r = (i // (2 * m)) * 2 * m + m - 1
        second = (i & m) != 0
        sums.append(np.where(second, (t > r) & (t <= i), (t > i) & (t <= r)))
        masks.append((i // (2 * m) == t // (2 * m)) & second & ((t & m) == 0))
        m //= 2
    sums.append(t > i)
    s = np.concatenate(sums, axis=0).astype(np.float32)
    return (jnp.asarray(np.concatenate([s, s], axis=1), BF16),
            jnp.asarray(np.stack(masks).astype(np.float32)))


def _lin_attn_chunk(q, k, v_bf, g, st_ref, sums_ref, mask_ref, nh):
    L, wk = q.shape
    dk = wk // nh
    dv = v_bf.shape[1] // nh
    n_lvl = mask_ref.shape[0] - 1
    g2 = g * LOG2_E
    hi = g2.astype(BF16)
    lo = (g2 - hi.astype(F32)).astype(BF16)
    p = jnp.exp2(_dot(sums_ref[...], jnp.concatenate([hi, lo], axis=0)))
    p_last = p[L - 1:L, :]
    qd = (q * p[0:L]).astype(BF16)
    kd = (k * p[(n_lvl + 1) * L:(n_lvl + 2) * L]).astype(BF16)
    q_bf = q.astype(BF16)
    k_bf = k.astype(BF16)
    rowi = lax.broadcasted_iota(jnp.int32, (L, 1), 0)
    xs = []
    for lvl in range(n_lvl):
        second = (rowi & (L >> (lvl + 1))) != 0
        xs.append((jnp.where(second, q, k) * p[(lvl + 1) * L:(lvl + 2) * L]).astype(BF16))
    outs = []
    for h in range(nh):
        ck = slice(h * dk, (h + 1) * dk)
        cv = slice(h * dv, (h + 1) * dv)
        a = _dot_nt(q_bf[:, ck], k_bf[:, ck]) * mask_ref[0]
        for lvl in range(n_lvl):
            x = xs[lvl][:, ck]
            a = a + _dot_nt(x, x) * mask_ref[lvl + 1]
        st = st_ref[h]
        outs.append(_dot_nt(qd[:, ck], st.astype(BF16)) + _dot(a.astype(BF16), v_bf[:, cv]))
        st_ref[h] = st * p_last[:, ck] + _dot_tn(v_bf[:, cv], kd[:, ck])
    return outs


def _rms_gate(o, norm_g, gate, eps=1e-6):
    ms = jnp.mean(o * o, axis=-1, keepdims=True)
    return o * lax.rsqrt(ms + eps) * norm_g * _silu(gate)


def _gla_kernel(q_ref, k_ref, v_ref, gg_ref, ga_ref, wa_ref, ba_ref, ng_ref, sums_ref, mask_ref, o_ref,
                st_ref, *, n_chunks, q_scale):
    @pl.when(pl.program_id(2) == 0)
    def _():
        st_ref[...] = jnp.zeros_like(st_ref)

    def chunk(c, carry):
        rows = pl.ds(pl.multiple_of(c * LIN_CHUNK, LIN_CHUNK), LIN_CHUNK)
        q = q_ref[rows, :].astype(F32) * q_scale
        k = k_ref[rows, :].astype(F32)
        a_pre = _dot(ga_ref[rows, :].astype(BF16), wa_ref[...]) + ba_ref[...]
        g = _log_sigmoid(a_pre) * (1.0 / GLA_TAU)
        (o,) = _lin_attn_chunk(q, k, v_ref[rows, :], g, st_ref, sums_ref, mask_ref, 1)
        o_ref[rows, :] = _rms_gate(o, ng_ref[...], gg_ref[rows, :].astype(F32)).astype(o_ref.dtype)
        return carry

    lax.fori_loop(0, n_chunks, chunk, 0, unroll=4)


def _hgrn_kernel(q_ref, f_ref, v_ref, hg_ref, lb_ref, ng_ref, sums_ref, mask_ref, o_ref,
                 st_ref, *, n_chunks, nh):
    dv = ng_ref.shape[1]

    @pl.when(pl.program_id(2) == 0)
    def _():
        st_ref[...] = jnp.zeros_like(st_ref)

    def chunk(c, carry):
        rows = pl.ds(pl.multiple_of(c * LIN_CHUNK, LIN_CHUNK), LIN_CHUNK)
        lb = lb_ref[...]
        f = lb + (1.0 - lb) * _sigmoid(f_ref[rows, :])
        q = _silu(q_ref[rows, :].astype(F32))
        outs = _lin_attn_chunk(q, 1.0 - f, v_ref[rows, :], jnp.log(f), st_ref, sums_ref, mask_ref, nh)
        gate = hg_ref[rows, :].astype(F32)
        for h, o in enumerate(outs):
            cv = slice(h * dv, (h + 1) * dv)
            o_ref[rows, cv] = _rms_gate(o, ng_ref[...], gate[:, cv]).astype(o_ref.dtype)
        return carry

    lax.fori_loop(0, n_chunks, chunk, 0, unroll=4)


def _col_spec(tb, width, col0):
    assert col0 % width == 0, (col0, width)
    base = col0 // width
    return pl.BlockSpec((None, tb, width), lambda b, h, t: (b, t, base + h))


def _const_specs(*arrays):
    return [pl.BlockSpec(a.shape, lambda b, h, t, nd=a.ndim: (0,) * nd) for a in arrays]


def _gla_call(h_bf, h_f, wa_pad, b_a, norm_g, *, offs, ga_col, dk, dv, tb):
    B, T, _ = h_bf.shape
    tb = min(tb, T)
    ga_w = wa_pad.shape[0]
    assert ga_col % ga_w == 0
    consts = _lin_constants(LIN_CHUNK)
    kern = functools.partial(_gla_kernel, n_chunks=tb // LIN_CHUNK, q_scale=dk ** -0.5)
    return pl.pallas_call(
        kern,
        grid=(B, GLA_HEADS, T // tb),
        in_specs=[_col_spec(tb, dk, offs["gq"]), _col_spec(tb, dk, offs["gk"]),
                  _col_spec(tb, dv, offs["gv"]), _col_spec(tb, dv, offs["gg"]),
                  pl.BlockSpec((None, tb, ga_w), lambda b, h, t: (b, t, ga_col // ga_w)),
                  pl.BlockSpec((ga_w, dk), lambda b, h, t: (0, h)),
                  pl.BlockSpec((1, dk), lambda b, h, t: (0, h)),
                  pl.BlockSpec((1, dv), lambda b, h, t: (0, 0))] + _const_specs(*consts),
        out_specs=pl.BlockSpec((None, tb, dv), lambda b, h, t: (b, t, h)),
        out_shape=jax.ShapeDtypeStruct((B, T, GLA_HEADS * dv), BF16),
        scratch_shapes=[pltpu.VMEM((1, dv, dk), F32)],
        compiler_params=_params("parallel", "parallel", "arbitrary"),
        name="gla_chunks",
    )(h_bf, h_bf, h_bf, h_bf, h_f, wa_pad, b_a, norm_g, *consts)


def _hgrn_call(h_bf, h_f, lb, norm_g, *, offs, heads, dk, dv, tb, nh=4):
    B, T, _ = h_bf.shape
    tb = min(tb, T)
    nh = min(nh, heads)
    assert heads % nh == 0
    consts = _lin_constants(LIN_CHUNK)
    kern = functools.partial(_hgrn_kernel, n_chunks=tb // LIN_CHUNK, nh=nh)
    return pl.pallas_call(
        kern,
        grid=(B, heads // nh, T // tb),
        in_specs=[_col_spec(tb, nh * dk, offs["hq"]),
                  pl.BlockSpec((None, tb, nh * dk), lambda b, h, t: (b, t, h)),
                  _col_spec(tb, nh * dv, offs["hi"]), _col_spec(tb, nh * dv, offs["hg"]),
                  pl.BlockSpec((1, nh * dk), lambda b, h, t: (0, h)),
                  pl.BlockSpec((1, dv), lambda b, h, t: (0, 0))] + _const_specs(*consts),
        out_specs=pl.BlockSpec((None, tb, nh * dv), lambda b, h, t: (b, t, h)),
        out_shape=jax.ShapeDtypeStruct((B, T, heads * dv), BF16),
        scratch_shapes=[pltpu.VMEM((nh, dv, dk), F32)],
        compiler_params=_params("parallel", "parallel", "arbitrary"),
        name="hgrn_chunks",
    )(h_bf, h_f, h_bf, h_bf, lb, norm_g, *consts)


def _memattn_kernel(q_ref, g_ref, k_ref, v_ref, o_ref, *, dh):
    scale = dh ** -0.5
    for h in range(MEM_HEADS):
        cs = slice(h * dh, (h + 1) * dh)
        s = _dot_nt(q_ref[:, cs], k_ref[:, cs]) * scale
        s = s - jnp.max(s, axis=-1, keepdims=True)
        p = jnp.exp(s)
        p = p / jnp.sum(p, axis=-1, keepdims=True)
        o = _dot(p.astype(BF16), v_ref[:, cs])
        o_ref[:, cs] = (o * _silu(g_ref[:, cs].astype(F32))).astype(o_ref.dtype)


def _memattn_call(h_bf, k_mem, v_mem, *, q_col, g_col, tm=512):
    B, T, _ = h_bf.shape
    _, ml, mw = k_mem.shape
    tm = min(tm, T)
    assert q_col % mw == 0 and g_col % mw == 0
    return pl.pallas_call(
        functools.partial(_memattn_kernel, dh=mw // MEM_HEADS),
        grid=(B, T // tm),
        in_specs=[pl.BlockSpec((None, tm, mw), lambda b, t: (b, t, q_col // mw)),
                  pl.BlockSpec((None, tm, mw), lambda b, t: (b, t, g_col // mw)),
                  pl.BlockSpec((None, ml, mw), lambda b, t: (b, 0, 0)),
                  pl.BlockSpec((None, ml, mw), lambda b, t: (b, 0, 0))],
        out_specs=pl.BlockSpec((None, tm, mw), lambda b, t: (b, t, 0)),
        out_shape=jax.ShapeDtypeStruct((B, T, mw), BF16),
        compiler_params=_params("parallel", "parallel"),
        name="mem_attention",
    )(h_bf, h_bf, k_mem, v_mem)


def _outproj_kernel(*refs, segs, nk, n_blocks, alpha, eps, emit_bf16):
    ns = len(segs)
    a_refs = refs[:ns]
    if emit_bf16:
        w_ref, x_hbm, g_ref, b_ref, o_hbm, obf_hbm, acc, x_sem, o_sem, bf_buf, bf_sem = refs[ns:]
    else:
        w_ref, x_hbm, g_ref, b_ref, o_hbm, acc, x_sem, o_sem = refs[ns:]
    i = pl.program_id(0)
    k = pl.program_id(1)
    _, tm, d = acc.shape
    rb, cb = min(128, tm), min(512, d)
    n_rb = tm // rb
    slot = i % 2
    o_ref = acc.at[slot]

    def rows_of(blk):
        return pl.ds(pl.multiple_of(blk * tm, tm), tm)

    def x_copy(blk, s):
        return pltpu.make_async_copy(x_hbm.at[rows_of(blk), :], acc.at[s], x_sem.at[s])

    def o_copy(blk, s):
        return pltpu.make_async_copy(acc.at[s], o_hbm.at[rows_of(blk), :], o_sem.at[s])

    def bf_copy(r, bslot):
        row0 = pl.multiple_of(i * tm + r * rb, rb)
        return pltpu.make_async_copy(bf_buf.at[bslot], obf_hbm.at[pl.ds(row0, rb), :], bf_sem.at[bslot])

    @pl.when((i == 0) & (k == 0))
    def _():
        x_copy(0, 0).start()

    @pl.when(k == 0)
    def _():
        x_copy(i, slot).wait()
        a = a_refs[0][...]
        for c in range(0, d, cb):
            o_ref[:, c:c + cb] = alpha * o_ref[:, c:c + cb] + _dot(a, w_ref[:, c:c + cb])

    for a_ref, (start, n) in zip(a_refs, segs):
        @pl.when((k >= max(start, 1)) & (k < start + n))
        def _(a_ref=a_ref):
            a = a_ref[...]
            for c in range(0, d, cb):
                o_ref[:, c:c + cb] += _dot(a, w_ref[:, c:c + cb])

    @pl.when(k == nk - 1)
    def _():
        @pl.when(i >= 1)
        def _():
            o_copy(i - 1, 1 - slot).wait()

        @pl.when(i + 1 < n_blocks)
        def _():
            x_copy(i + 1, 1 - slot).start()

        def norm(r, carry):
            rows = pl.ds(pl.multiple_of(r * rb, rb), rb)
            y = o_ref[rows, :]
            mu = jnp.mean(y, axis=-1, keepdims=True)
            yc = y - mu
            var = jnp.mean(yc * yc, axis=-1, keepdims=True)
            out = yc * lax.rsqrt(var + eps) * g_ref[...] + b_ref[...]
            o_ref[rows, :] = out
            if emit_bf16:
                n_slot = bf_buf.shape[0]
                bslot = r % n_slot

                @pl.when(r >= n_slot)
                def _():
                    bf_copy(r - n_slot, bslot).wait()

                bf_buf[bslot] = out.astype(bf_buf.dtype)
                bf_copy(r, bslot).start()
            return carry

        lax.fori_loop(0, n_rb, norm, 0)
        o_copy(i, slot).start()
        if emit_bf16:
            for r in range(max(n_rb - bf_buf.shape[0], 0), n_rb):
                bf_copy(r, r % bf_buf.shape[0]).wait()

        @pl.when(i == n_blocks - 1)
        def _():
            o_copy(i, slot).wait()


def _outproj_ln(acts, w_bf, x2d, ln_g, ln_b, alpha, emit_bf16, tm=1024, tk=512):
    m, d = x2d.shape
    tm = min(tm, m)
    segs, start = [], 0
    for a in acts:
        assert a.shape[0] == m and a.shape[1] % tk == 0
        segs.append((start, a.shape[1] // tk))
        start += a.shape[1] // tk
    nk = start
    assert nk * tk == w_bf.shape[0]

    def a_spec(s0, n):
        return pl.BlockSpec((tm, tk), lambda i, k: (i, jnp.clip(k - s0, 0, n - 1)))

    kern = functools.partial(_outproj_kernel, segs=tuple(segs), nk=nk, n_blocks=m // tm, alpha=alpha, eps=1e-5,
                             emit_bf16=emit_bf16)
    rb = min(128, tm)
    out_specs = [pl.BlockSpec(memory_space=pl.ANY)]
    out_shape = [jax.ShapeDtypeStruct((m, d), F32)]
    scratch = [pltpu.VMEM((2, tm, d), F32), pltpu.SemaphoreType.DMA((2,)), pltpu.SemaphoreType.DMA((2,))]
    if emit_bf16:
        out_specs.append(pl.BlockSpec(memory_space=pl.ANY))
        out_shape.append(jax.ShapeDtypeStruct((m, d), BF16))
        n_slot = 4
        scratch += [pltpu.VMEM((n_slot, rb, d), BF16), pltpu.SemaphoreType.DMA((n_slot,))]
    outs = pl.pallas_call(
        kern,
        grid=(m // tm, nk),
        in_specs=[a_spec(s0, n) for s0, n in segs] + [
            pl.BlockSpec((tk, d), lambda i, k: (k, 0)),
            pl.BlockSpec(memory_space=pl.ANY),
            pl.BlockSpec((1, d), lambda i, k: (0, 0)),
            pl.BlockSpec((1, d), lambda i, k: (0, 0))],
        out_specs=out_specs,
        out_shape=out_shape,
        scratch_shapes=scratch,
        compiler_params=_params("arbitrary", "arbitrary"),
        name="outproj_layernorm",
    )(*acts, w_bf, x2d, ln_g, ln_b)
    return (outs[0], outs[1]) if emit_bf16 else (outs[0], None)


def _mlstm_pre_kernel(x_ref, halo_ref, cw_ref, cb_ref, wq_ref, wk_ref, wkt_ref, wv_ref,
                      iq_ref, ik_ref, iv_ref, bif_ref,
                      xc_ref, q_ref, k_ref, kt_ref, v_ref, g_ref, *, n_sub, bw, n_taps):
    t = pl.program_id(1)
    c = pl.program_id(2)
    tm = x_ref.shape[0]

    @pl.when(c == 0)
    def _():
        g_ref[...] = jnp.broadcast_to(bif_ref[...], g_ref.shape)

    gates = jnp.zeros(g_ref.shape, F32)
    for j in range(n_sub):
        cs = slice(j * bw, (j + 1) * bw)
        x_bf = x_ref[:, cs]
        x32 = x_bf.astype(F32)
        halo = jnp.where(t > 0, halo_ref[:, cs].astype(F32), 0.0)
        ext = jnp.concatenate([halo, x32], axis=0)
        acc = jnp.broadcast_to(cb_ref[:, cs], (tm, bw))
        for tap in range(n_taps):
            sh = 8 - (n_taps - 1) + tap
            acc = acc + ext[sh:sh + tm, :] * cw_ref[tap:tap + 1, cs]
        xc_bf = _silu(acc).astype(BF16)
        xc_ref[:, cs] = xc_bf
        q_bf = _dot(xc_bf, wq_ref[j]).astype(BF16)
        k_bf = _dot(xc_bf, wk_ref[j]).astype(BF16)
        v_bf = _dot(x_bf, wv_ref[j]).astype(BF16)
        q_ref[:, cs] = q_bf
        k_ref[:, cs] = k_bf
        v_ref[:, cs] = v_bf
        kt_ref[cs, :] = _dot_nt(wkt_ref[j], xc_bf).astype(BF16)
        gates = gates + _dot(q_bf, iq_ref[cs, :]) + _dot(k_bf, ik_ref[cs, :]) + _dot(v_bf, iv_ref[cs, :])
    g_ref[...] += gates


def _block_diag(w, bw):
    nb, bi, bo = w.shape
    assert bi == bo and bw % bi == 0 and (nb * bi) % bw == 0
    per = bw // bi
    w4 = w.reshape(nb // per, per, bi, bo)
    eye = jnp.eye(per, dtype=w.dtype)
    return jnp.einsum("cnio,nm->cnimo", w4, eye).reshape(nb // per, bw, bw)


def _mlstm_pre_call(h_bf, conv_w, conv_b, w_q, w_k, w_v, w_if, b_if, *, width, tm=512, tc=1024, bw=256):
    B, T, _ = h_bf.shape
    tm, tc = min(tm, T), min(tc, width)
    n_taps = conv_w.shape[0]
    assert tm % 8 == 0 and n_taps <= 8 and width % tc == 0 and tc % bw == 0
    n_sub = tc // bw
    wq = _block_diag(w_q, bw).astype(BF16)
    wk = _block_diag(w_k, bw)
    wkt = jnp.swapaxes(wk, 1, 2).astype(BF16)
    wk = wk.astype(BF16)
    wv = _block_diag(w_v, bw).astype(BF16)
    n_gate = w_if.shape[1]
    wif = jnp.pad(w_if, ((0, 0), (0, 128 - n_gate))).astype(BF16)
    bif = jnp.pad(b_if.astype(F32), (0, 128 - n_gate)).reshape(1, 128)
    hb = tm // 8
    tile_spec = pl.BlockSpec((n_sub, bw, bw), lambda b, t, c: (c, 0, 0))
    act_spec = pl.BlockSpec((None, tm, tc), lambda b, t, c: (b, t, c))
    kern = functools.partial(_mlstm_pre_kernel, n_sub=n_sub, bw=bw, n_taps=n_taps)
    shp = jax.ShapeDtypeStruct((B, T, width), BF16)
    return pl.pallas_call(
        kern,
        grid=(B, T // tm, width // tc),
        in_specs=[act_spec,
                  pl.BlockSpec((None, 8, tc), lambda b, t, c: (b, jnp.maximum(t * hb - 1, 0), c)),
                  pl.BlockSpec((n_taps, tc), lambda b, t, c: (0, c)),
                  pl.BlockSpec((1, tc), lambda b, t, c: (0, c)),
                  tile_spec, tile_spec, tile_spec, tile_spec,
                  pl.BlockSpec((tc, 128), lambda b, t, c: (c, 0)),
                  pl.BlockSpec((tc, 128), lambda b, t, c: (width // tc + c, 0)),
                  pl.BlockSpec((tc, 128), lambda b, t, c: (2 * (width // tc) + c, 0)),
                  pl.BlockSpec((1, 128), lambda b, t, c: (0, 0))],
        out_specs=[act_spec, act_spec, act_spec,
                   pl.BlockSpec((None, tc, tm), lambda b, t, c: (b, c, t)),
                   act_spec,
                   pl.BlockSpec((None, tm, 128), lambda b, t, c: (b, t, 0))],
        out_shape=[shp, shp, shp, jax.ShapeDtypeStruct((B, width, T), BF16), shp,
                   jax.ShapeDtypeStruct((B, T, 128), F32)],
        compiler_params=_params("parallel", "parallel", "arbitrary"),
        name="mlstm_frontend",
    )(h_bf, h_bf, conv_w, conv_b.reshape(1, width), wq, wk, wkt, wv, wif, wif, wif, bif)


def _mlstm_kernel(q_ref, k_ref, kt_ref, v_ref, gc_ref, gr_ref, xc_ref, z_ref, ng_ref, sk_ref, o_ref,
                  c_ref, hh_ref, n_ref, m_ref, *, n_chunks, heads, k_scale, dvb):
    h = pl.program_id(1)
    L = MLSTM_CHUNK
    dk, dv = c_ref.shape

    @pl.when(pl.program_id(2) == 0)
    def _():
        c_ref[...] = jnp.zeros_like(c_ref)
        n_ref[...] = jnp.zeros_like(n_ref)
        m_ref[...] = jnp.zeros_like(m_ref)

    row = lax.broadcasted_iota(jnp.int32, (L, L), 0)
    col = lax.broadcasted_iota(jnp.int32, (L, L), 1)
    causal = row >= col
    glane = lax.broadcasted_iota(jnp.int32, (L, gc_ref.shape[1]), 1)

    for c in range(n_chunks):
        rs = slice(c * L, (c + 1) * L)
        qc = q_ref[rs, :]
        kc = k_ref[rs, :]
        vc = v_ref[rs, :]
        gcol = gc_ref[rs, :]
        i_col = jnp.sum(jnp.where(glane == h, gcol, 0.0), axis=1, keepdims=True)
        f_col = jnp.sum(jnp.where(glane == heads + h, gcol, 0.0), axis=1, keepdims=True)
        i_row = gr_ref[0:1, rs]
        f_row = gr_ref[1:2, rs]
        lf_col = _log_sigmoid(f_col)
        lf_row = _log_sigmoid(f_row)
        b_col = jnp.sum(jnp.where(causal, lf_row, 0.0), axis=1, keepdims=True)
        b_row = jnp.sum(jnp.where(row <= col, lf_col, 0.0), axis=0, keepdims=True)
        b_last = b_col[L - 1:L, :]
        m_prev = m_ref[0:1, 0:1]

        d = jnp.where(causal, b_col - b_row + i_row, NEG_BIG)
        inter_log = b_col + m_prev
        m_i = jnp.maximum(inter_log, jnp.max(d, axis=1, keepdims=True))
        w_intra = jnp.exp(d - m_i)
        w_inter = jnp.exp(inter_log - m_i) * k_scale
        scores = _dot_nt(qc, kc) * k_scale * w_intra
        n_row = n_ref[...]
        den = (jnp.sum(scores, axis=1, keepdims=True)
               + w_inter * jnp.sum(qc.astype(F32) * n_row, axis=1, keepdims=True))
        inv = 1.0 / jnp.maximum(jnp.abs(den), jnp.exp(-m_i))
        scores_bf = scores.astype(BF16)

        log_wj_row = b_last - b_row + i_row
        log_wj_col = b_last - b_col + i_col
        m_new = jnp.maximum(b_last + m_prev, jnp.max(log_wj_row, axis=1, keepdims=True))
        wj_row = jnp.exp(log_wj_row - m_new)
        wj_col = jnp.exp(log_wj_col - m_new)
        dec = jnp.exp(b_last + m_prev - m_new)
        ktw = (kt_ref[:, rs].astype(F32) * wj_row).astype(BF16)

        for j in range(dv // dvb):
            cs = slice(j * dvb, (j + 1) * dvb)
            c_old = c_ref[:, cs]
            num = _dot(scores_bf, vc[:, cs]) + w_inter * _dot(qc, c_old.astype(BF16))
            hh_ref[:, cs] = num * inv
            c_ref[:, cs] = c_old * dec + _dot(ktw, vc[:, cs])
        n_ref[...] = n_row * dec + jnp.sum(kc.astype(F32) * wj_col, axis=0, keepdims=True)
        m_ref[...] = jnp.broadcast_to(m_new, m_ref.shape)

        hh = hh_ref[...]
        mu = jnp.mean(hh, axis=-1, keepdims=True)
        hc = hh - mu
        var = jnp.mean(hc * hc, axis=-1, keepdims=True)
        hn = hc * lax.rsqrt(var + 1e-6) * ng_ref[...]
        out = (hn + sk_ref[...] * xc_ref[rs, :].astype(F32)) * _silu(z_ref[rs, :].astype(F32))
        o_ref[rs, :] = out.astype(o_ref.dtype)


def _mlstm_call(q, k, kt, v, gates, gates_rows, xc, h_bf, norm_g, skip, *, heads, z_col, tb=MLSTM_CHUNK):
    B, T, width = q.shape
    dh = width // heads
    tb = min(tb, T)
    assert tb % MLSTM_CHUNK == 0 and z_col % dh == 0
    dvb = min(512, dh)
    hs = pl.BlockSpec((None, tb, dh), lambda b, h, t: (b, t, h))
    kern = functools.partial(_mlstm_kernel, n_chunks=tb // MLSTM_CHUNK, heads=heads,
                             k_scale=dh ** -0.5, dvb=dvb)
    return pl.pallas_call(
        kern,
        grid=(B, heads, T // tb),
        in_specs=[hs, hs,
                  pl.BlockSpec((None, dh, tb), lambda b, h, t: (b, h, t)),
                  hs,
                  pl.BlockSpec((None, tb, gates.shape[2]), lambda b, h, t: (b, t, 0)),
                  pl.BlockSpec((None, None, 2, tb), lambda b, h, t: (b, h, 0, t)),
                  hs,
                  pl.BlockSpec((None, tb, dh), lambda b, h, t: (b, t, z_col // dh + h)),
                  pl.BlockSpec((1, dh), lambda b, h, t: (0, h)),
                  pl.BlockSpec((1, dh), lambda b, h, t: (0, h))],
        out_specs=hs,
        out_shape=jax.ShapeDtypeStruct((B, T, width), BF16),
        scratch_shapes=[pltpu.VMEM((dh, dh), F32), pltpu.VMEM((MLSTM_CHUNK, dh), F32),
                        pltpu.VMEM((1, dh), F32), pltpu.VMEM((8, 128), F32)],
        compiler_params=_params("parallel", "parallel", "arbitrary"),
        name="mlstm_chunks",
    )(q, k, kt, v, gates, gates_rows, xc, h_bf, norm_g, skip)


def _mem_kv(mem_bf, w_k, w_v, B):
    mw = w_k.shape[1]
    k = _matmul(mem_bf, w_k.astype(BF16), BF16).reshape(B, -1, mw)
    v = _matmul(mem_bf, w_v.astype(BF16), BF16).reshape(B, -1, mw)
    return k, v


def _even_layer(x, x_bf, emit_bf16, mem_bf, lb, w_in_stack, li, gla_w_a2, gla_b_a, gla_norm_g, hgrn_norm_g,
                mem_w_k, mem_w_v, w_out, ln_g, ln_b, alpha):
    B, T, D = x.shape
    lowrank, kw = gla_w_a2.shape
    gdv = gla_norm_g.shape[0]
    gw = GLA_HEADS * gdv
    gdk = kw // GLA_HEADS
    fw = lb.shape[0]
    hheads = fw // HGRN_EXPAND
    hdv = hgrn_norm_g.shape[0]
    hw = hheads * hdv
    mw = mem_w_k.shape[1]
    sizes = (kw, kw, gw, gw, lowrank, fw, fw, hw, hw, mw, mw)
    assert sum(sizes) == w_in_stack.shape[2]
    bounds = [0]
    for s in sizes:
        bounds.append(bounds[-1] + s)
    ga_w = 256
    w_a = _window_cast(w_in_stack, li, 0, bounds[4])
    w_b = _window_cast(w_in_stack, li, bounds[5], bounds[-1] - bounds[5])
    w_ga = jnp.pad(w_in_stack[li, :, bounds[4]:bounds[5]], ((0, 0), (0, ga_w - lowrank))).astype(BF16)
    offs_a = {"gq": 0, "gk": kw, "gv": 2 * kw, "gg": 2 * kw + gw}
    offs_b = {"hq": 0, "hi": fw, "hg": fw + hw, "mq": fw + 2 * hw, "mg": fw + 2 * hw + mw}

    x2d = x.reshape(B * T, D)
    if x_bf is None:
        x_bf = x2d.astype(BF16)
    h_a = _matmul(x_bf, w_a, BF16)
    h_b = _matmul(x_bf, w_b, BF16, cols=[(0, fw), (2 * fw, bounds[-1] - bounds[5] - 2 * fw)])
    h_hf = _matmul(x_bf, w_b, F32, cols=[(fw, fw)])
    h_ga = _matmul(x_bf, w_ga, F32)
    h_a, h_b, h_hf, h_ga = (h.reshape(B, T, -1) for h in (h_a, h_b, h_hf, h_ga))

    wa_pad = jnp.pad(gla_w_a2, ((0, ga_w - lowrank), (0, 0))).astype(BF16)
    gla_out = _gla_call(h_a, h_ga, wa_pad, gla_b_a.reshape(1, kw), gla_norm_g.reshape(1, gdv),
                        offs=offs_a, ga_col=0, dk=gdk, dv=gdv, tb=512)
    hgrn_out = _hgrn_call(h_b, h_hf, lb.reshape(1, fw), hgrn_norm_g.reshape(1, hdv),
                          offs=offs_b, heads=hheads, dk=HGRN_EXPAND, dv=hdv, tb=512)
    k_mem, v_mem = _mem_kv(mem_bf, mem_w_k, mem_w_v, B)
    mem_out = _memattn_call(h_b, k_mem, v_mem, q_col=offs_b["mq"], g_col=offs_b["mg"])

    acts = [a.reshape(B * T, -1) for a in (gla_out, hgrn_out, mem_out)]
    y, y_bf = _outproj_ln(acts, w_out.astype(BF16), x2d, ln_g.reshape(1, D), ln_b.reshape(1, D), alpha, emit_bf16)
    return y.reshape(B, T, D), y_bf


def _odd_layer(x, x_bf, emit_bf16, mem_bf, w_in, conv_w, conv_b, w_q, w_k, w_v, w_if, b_if, mh_norm_g, skip,
               mem_w_k, mem_w_v, w_out, ln_g, ln_b, alpha):
    B, T, D = x.shape
    width = conv_w.shape[1]
    heads = b_if.shape[0] // 2
    mw = mem_w_k.shape[1]
    assert w_in.shape[1] == 2 * width + 2 * mw

    x2d = x.reshape(B * T, D)
    if x_bf is None:
        x_bf = x2d.astype(BF16)
    h_bf = _matmul(x_bf, w_in.astype(BF16), BF16).reshape(B, T, -1)

    xc, q, k, kt, v, gates = _mlstm_pre_call(h_bf, conv_w, conv_b, w_q, w_k, w_v, w_if, b_if, width=width)
    gates_rows = jnp.transpose(gates[:, :, :2 * heads].reshape(B, T, 2, heads), (0, 3, 2, 1))
    mlstm_out = _mlstm_call(q, k, kt, v, gates, gates_rows, xc, h_bf,
                            mh_norm_g.reshape(1, width), skip.reshape(1, width), heads=heads, z_col=width)
    k_mem, v_mem = _mem_kv(mem_bf, mem_w_k, mem_w_v, B)
    mem_out = _memattn_call(h_bf, k_mem, v_mem, q_col=2 * width, g_col=2 * width + mw)

    acts = [a.reshape(B * T, -1) for a in (mlstm_out, mem_out)]
    y, y_bf = _outproj_ln(acts, w_out.astype(BF16), x2d, ln_g.reshape(1, D), ln_b.reshape(1, D), alpha, emit_bf16)
    return y.reshape(B, T, D), y_bf


def kernel(x, mem, hgrn_lb_logits, ev_w_in, ev_gla_w_a2, ev_gla_b_a, ev_gla_norm_g, ev_hgrn_norm_g, ev_mem_w_k, ev_mem_w_v, ev_w_out, ev_ln_g, ev_ln_b, od_w_in, od_conv_w, od_conv_b, od_w_q, od_w_k, od_w_v, od_w_if, od_b_if, od_mh_norm_g, od_skip, od_mem_w_k, od_mem_w_v, od_w_out, od_ln_g, od_ln_b):
    depth = ev_w_in.shape[0] + od_w_in.shape[0]
    alpha = (2 * depth) ** 0.25
    B, ml, D = mem.shape
    mem_bf = mem.reshape(B * ml, D).astype(BF16)
    lb_all = jnp.cumsum(jax.nn.softmax(hgrn_lb_logits.astype(F32), axis=0), axis=0)
    x_bf = None
    for layer in range(depth):
        i = layer // 2
        emit_bf16 = layer + 1 < depth
        if layer % 2 == 0:
            x, x_bf = _even_layer(x, x_bf, emit_bf16, mem_bf, lb_all[layer], ev_w_in, i, ev_gla_w_a2[i],
                                  ev_gla_b_a[i], ev_gla_norm_g[i], ev_hgrn_norm_g[i], ev_mem_w_k[i],
                                  ev_mem_w_v[i], ev_w_out[i], ev_ln_g[i], ev_ln_b[i], alpha)
        else:
            x, x_bf = _odd_layer(x, x_bf, emit_bf16, mem_bf, od_w_in[i], od_conv_w[i], od_conv_b[i], od_w_q[i],
                                 od_w_k[i], od_w_v[i], od_w_if[i], od_b_if[i], od_mh_norm_g[i], od_skip[i],
                                 od_mem_w_k[i], od_mem_w_v[i], od_w_out[i], od_ln_g[i], od_ln_b[i], alpha)
    return x
```

```python
import functools

import jax
import jax.numpy as jnp
import numpy as np
from jax import lax
from jax.experimental import pallas as pl
from jax.experimental.pallas import tpu as pltpu

F32 = jnp.float32
BF16 = jnp.bfloat16

GLA_HEADS = 4
GLA_TAU = 16.0
HGRN_EXPAND = 128
MEM_HEADS = 4

LIN_CHUNK = 128
MLSTM_CHUNK = 256

V7X_VMEM_LIMIT_BYTES = 56 * 1024 * 1024
V7X_MXU_WIDTH = 256
NEG_BIG = -1e30
LOG2_E = 1.4426950408889634


def _sigmoid(x):
    return 1.0 / (1.0 + jnp.exp(-x))


def _silu(x):
    return x * _sigmoid(x)


def _log_sigmoid(x):
    return jnp.minimum(x, 0.0) - jnp.log(1.0 + jnp.exp(-jnp.abs(x)))


def _dot(a, b):
    return jnp.dot(a, b, preferred_element_type=F32)


def _dot_nt(a, b):
    return lax.dot_general(a, b, (((1,), (1,)), ((), ())), preferred_element_type=F32)


def _dot_tn(a, b):
    return lax.dot_general(a, b, (((0,), (0,)), ((), ())), preferred_element_type=F32)


def _params(*sem):
    return pltpu.CompilerParams(dimension_semantics=sem, vmem_limit_bytes=V7X_VMEM_LIMIT_BYTES)


def _mm_kernel(a_ref, w_ref, o_ref):
    o_ref[...] = _dot(a_ref[...], w_ref[...]).astype(o_ref.dtype)


def _matmul(a, w, out_dtype, tm=1024, tn=1024, cols=None):
    m, k = a.shape
    cols = [(0, w.shape[1])] if cols is None else list(cols)
    tm = min(tm, m)
    while any(v % tn for rng in cols for v in rng):
        tn //= 2
    assert m % tm == 0 and tn % 128 == 0, (m, cols, tm, tn)
    n_out = sum(n for _, n in cols)

    def w_block(i, j):
        blk, done = None, 0
        for first, n in cols:
            here = first // tn + (j - done)
            blk = here if blk is None else jnp.where(j >= done, here, blk)
            done += n // tn
        return 0, blk

    return pl.pallas_call(
        _mm_kernel,
        grid=(m // tm, n_out // tn),
        in_specs=[pl.BlockSpec((tm, k), lambda i, j: (i, 0)),
                  pl.BlockSpec((k, tn), w_block)],
        out_specs=pl.BlockSpec((tm, tn), lambda i, j: (i, j)),
        out_shape=jax.ShapeDtypeStruct((m, n_out), out_dtype),
        compiler_params=_params("parallel", "arbitrary"),
        name="proj_matmul",
    )(a, w)


def _window_cast_kernel(x_ref, o_ref):
    o_ref[...] = x_ref[0].T.astype(o_ref.dtype)


def _window_cast(w_stack, layer, col0, ncols, tr=1024, tkb=1024):
    wt = jnp.swapaxes(w_stack, 1, 2)
    _, n, k = wt.shape
    tkb = min(tkb, k)
    while ncols % tr:
        tr //= 2
    assert col0 % 8 == 0 and tr % 128 == 0 and k % tkb == 0 and col0 + ncols <= n, (col0, ncols, tr)
    return pl.pallas_call(
        _window_cast_kernel,
        grid=(ncols // tr, k // tkb),
        in_specs=[pl.BlockSpec((pl.Element(1), pl.Element(tr), pl.Element(tkb)),
                               lambda i, j: (layer, pl.multiple_of(col0 + i * tr, 8), pl.multiple_of(j * tkb, 128)))],
        out_specs=pl.BlockSpec((tkb, tr), lambda i, j: (j, i)),
        out_shape=jax.ShapeDtypeStruct((k, ncols), BF16),
        compiler_params=_params("parallel", "parallel"),
        name="weight_window_cast",
    )(wt)


def _lin_constants(L):
    i = np.arange(L)[:, None]
    t = np.arange(L)[None, :]
    sums = [t <= i]
    masks = [i == t]
    m = L // 2
    while m >= 1:
        r = (i // (2 * m)) * 2 * m + m - 1
        second = (i & m) != 0
        sums.append(np.where(second, (t > r) & (t <= i), (t > i) & (t <= r)))
        masks.append((i // (2 * m) == t // (2 * m)) & second & ((t & m) == 0))
        m //= 2
    sums.append(t > i)
    s = np.concatenate(sums, axis=0).astype(np.float32)
    return (jnp.asarray(np.concatenate([s, s], axis=1), BF16),
            jnp.asarray(np.stack(masks).astype(np.float32)))


def _block_rows(x, grp, d):
    if grp == 1:
        return x
    lane = lax.broadcasted_iota(jnp.int32, x.shape, 1)
    zero = jnp.zeros_like(x)
    return jnp.concatenate([jnp.where((lane >= r * d) & (lane < (r + 1) * d), x, zero) for r in range(grp)], axis=0)


def _lin_attn_chunk(q, k, v_bf, g, st_ref, sums_ref, mask_ref, nh):
    L, wk = q.shape
    dk = wk // nh
    dv = v_bf.shape[1] // nh
    grp = st_ref.shape[1] // dv
    n_lvl = mask_ref.shape[0] - 1
    g2 = g * LOG2_E
    hi = g2.astype(BF16)
    lo = (g2 - hi.astype(F32)).astype(BF16)
    p = jnp.exp2(_dot(sums_ref[...], jnp.concatenate([hi, lo], axis=0)))
    p_last = p[L - 1:L, :]
    qd = (q * p[0:L]).astype(BF16)
    kd = (k * p[(n_lvl + 1) * L:(n_lvl + 2) * L]).astype(BF16)
    q_bf = q.astype(BF16)
    k_bf = k.astype(BF16)
    rowi = lax.broadcasted_iota(jnp.int32, (L, 1), 0)
    xs = []
    for lvl in range(n_lvl):
        second = (rowi & (L >> (lvl + 1))) != 0
        xs.append((jnp.where(second, q, k) * p[(lvl + 1) * L:(lvl + 2) * L]).astype(BF16))
    def mask(l):
        m = mask_ref[l]
        return m if grp == 1 else jnp.concatenate([m] * grp, axis=1)

    outs = []
    for h in range(nh // grp):
        ck = slice(h * grp * dk, (h + 1) * grp * dk)
        cv = slice(h * grp * dv, (h + 1) * grp * dv)
        a = _dot_nt(q_bf[:, ck], _block_rows(k_bf[:, ck], grp, dk)) * mask(0)
        for lvl in range(n_lvl):
            x = xs[lvl][:, ck]
            a = a + _dot_nt(x, _block_rows(x, grp, dk)) * mask(lvl + 1)
        st = st_ref[h]
        v_blk = _block_rows(v_bf[:, cv], grp, dv)
        o = _dot_nt(qd[:, ck], st.astype(BF16)) + _dot(a.astype(BF16), v_blk)
        st_ref[h] = st * p_last[:, ck] + _dot_tn(v_blk, _block_rows(kd[:, ck], grp, dk))
        outs.extend(o[:, r * dv:(r + 1) * dv] for r in range(grp))
    return outs


def _rms_gate(o, norm_g, gate, eps=1e-6):
    ms = jnp.mean(o * o, axis=-1, keepdims=True)
    return o * lax.rsqrt(ms + eps) * norm_g * _silu(gate)


def _gla_kernel(q_ref, k_ref, v_ref, gg_ref, ga_ref, wa_ref, ba_ref, ng_ref, sums_ref, mask_ref, o_ref,
                st_ref, *, n_chunks, q_scale):
    @pl.when(pl.program_id(2) == 0)
    def _():
        st_ref[...] = jnp.zeros_like(st_ref)

    def chunk(c, carry):
        rows = pl.ds(pl.multiple_of(c * LIN_CHUNK, LIN_CHUNK), LIN_CHUNK)
        q = q_ref[rows, :].astype(F32) * q_scale
        k = k_ref[rows, :].astype(F32)
        a_pre = _dot(ga_ref[rows, :].astype(BF16), wa_ref[...]) + ba_ref[...]
        g = _log_sigmoid(a_pre) * (1.0 / GLA_TAU)
        (o,) = _lin_attn_chunk(q, k, v_ref[rows, :], g, st_ref, sums_ref, mask_ref, 1)
        o_ref[rows, :] = _rms_gate(o, ng_ref[...], gg_ref[rows, :].astype(F32)).astype(o_ref.dtype)
        return carry

    lax.fori_loop(0, n_chunks, chunk, 0, unroll=4)


def _hgrn_kernel(q_ref, f_ref, v_ref, hg_ref, lb_ref, ng_ref, sums_ref, mask_ref, o_ref,
                 st_ref, *, n_chunks, nh):
    dv = ng_ref.shape[1]

    @pl.when(pl.program_id(2) == 0)
    def _():
        st_ref[...] = jnp.zeros_like(st_ref)

    def chunk(c, carry):
        rows = pl.ds(pl.multiple_of(c * LIN_CHUNK, LIN_CHUNK), LIN_CHUNK)
        lb = lb_ref[...]
        f = lb + (1.0 - lb) * _sigmoid(f_ref[rows, :])
        q = _silu(q_ref[rows, :].astype(F32))
        outs = _lin_attn_chunk(q, 1.0 - f, v_ref[rows, :], jnp.log(f), st_ref, sums_ref, mask_ref, nh)
        gate = hg_ref[rows, :].astype(F32)
        for h, o in enumerate(outs):
            cv = slice(h * dv, (h + 1) * dv)
            o_ref[rows, cv] = _rms_gate(o, ng_ref[...], gate[:, cv]).astype(o_ref.dtype)
        return carry

    lax.fori_loop(0, n_chunks, chunk, 0, unroll=4)


def _col_spec(tb, width, col0):
    assert col0 % width == 0, (col0, width)
    base = col0 // width
    return pl.BlockSpec((None, tb, width), lambda b, h, t: (b, t, base + h))


def _const_specs(*arrays):
    return [pl.BlockSpec(a.shape, lambda b, h, t, nd=a.ndim: (0,) * nd) for a in arrays]


def _gla_call(h_bf, h_f, wa_pad, b_a, norm_g, *, offs, ga_col, dk, dv, tb):
    B, T, _ = h_bf.shape
    tb = min(tb, T)
    ga_w = wa_pad.shape[0]
    assert ga_col % ga_w == 0
    consts = _lin_constants(LIN_CHUNK)
    kern = functools.partial(_gla_kernel, n_chunks=tb // LIN_CHUNK, q_scale=dk ** -0.5)
    return pl.pallas_call(
        kern,
        grid=(B, GLA_HEADS, T // tb),
        in_specs=[_col_spec(tb, dk, offs["gq"]), _col_spec(tb, dk, offs["gk"]),
                  _col_spec(tb, dv, offs["gv"]), _col_spec(tb, dv, offs["gg"]),
                  pl.BlockSpec((None, tb, ga_w), lambda b, h, t: (b, t, ga_col // ga_w)),
                  pl.BlockSpec((ga_w, dk), lambda b, h, t: (0, h)),
                  pl.BlockSpec((1, dk), lambda b, h, t: (0, h)),
                  pl.BlockSpec((1, dv), lambda b, h, t: (0, 0))] + _const_specs(*consts),
        out_specs=pl.BlockSpec((None, tb, dv), lambda b, h, t: (b, t, h)),
        out_shape=jax.ShapeDtypeStruct((B, T, GLA_HEADS * dv), BF16),
        scratch_shapes=[pltpu.VMEM((1, dv, dk), F32)],
        compiler_params=_params("parallel", "parallel", "arbitrary"),
        name="gla_chunks",
    )(h_bf, h_bf, h_bf, h_bf, h_f, wa_pad, b_a, norm_g, *consts)


def _hgrn_call(h_bf, h_f, lb, norm_g, *, offs, heads, dk, dv, tb, nh=4):
    B, T, _ = h_bf.shape
    tb = min(tb, T)
    nh = min(nh, heads)
    assert heads % nh == 0
    grp = 2 if (nh % 2 == 0 and dk == dv and 2 * dk == V7X_MXU_WIDTH) else 1
    consts = _lin_constants(LIN_CHUNK)
    kern = functools.partial(_hgrn_kernel, n_chunks=tb // LIN_CHUNK, nh=nh)
    return pl.pallas_call(
        kern,
        grid=(B, heads // nh, T // tb),
        in_specs=[_col_spec(tb, nh * dk, offs["hq"]),
                  pl.BlockSpec((None, tb, nh * dk), lambda b, h, t: (b, t, h)),
                  _col_spec(tb, nh * dv, offs["hi"]), _col_spec(tb, nh * dv, offs["hg"]),
                  pl.BlockSpec((1, nh * dk), lambda b, h, t: (0, h)),
                  pl.BlockSpec((1, dv), lambda b, h, t: (0, 0))] + _const_specs(*consts),
        out_specs=pl.BlockSpec((None, tb, nh * dv), lambda b, h, t: (b, t, h)),
        out_shape=jax.ShapeDtypeStruct((B, T, heads * dv), BF16),
        scratch_shapes=[pltpu.VMEM((nh // grp, grp * dv, grp * dk), F32)],
        compiler_params=_params("parallel", "parallel", "arbitrary"),
        name="hgrn_chunks",
    )(h_bf, h_f, h_bf, h_bf, lb, norm_g, *consts)


def _memattn_kernel(q_ref, g_ref, k_ref, v_ref, o_ref, *, dh):
    scale = dh ** -0.5
    for h in range(MEM_HEADS):
        cs = slice(h * dh, (h + 1) * dh)
        s = _dot_nt(q_ref[:, cs], k_ref[:, cs]) * scale
        s = s - jnp.max(s, axis=-1, keepdims=True)
        p = jnp.exp(s)
        p = p / jnp.sum(p, axis=-1, keepdims=True)
        o = _dot(p.astype(BF16), v_ref[:, cs])
        o_ref[:, cs] = (o * _silu(g_ref[:, cs].astype(F32))).astype(o_ref.dtype)


def _memattn_call(h_bf, k_mem, v_mem, *, q_col, g_col, tm=512):
    B, T, _ = h_bf.shape
    _, ml, mw = k_mem.shape
    tm = min(tm, T)
    assert q_col % mw == 0 and g_col % mw == 0
    return pl.pallas_call(
        functools.partial(_memattn_kernel, dh=mw // MEM_HEADS),
        grid=(B, T // tm),
        in_specs=[pl.BlockSpec((None, tm, mw), lambda b, t: (b, t, q_col // mw)),
                  pl.BlockSpec((None, tm, mw), lambda b, t: (b, t, g_col // mw)),
                  pl.BlockSpec((None, ml, mw), lambda b, t: (b, 0, 0)),
                  pl.BlockSpec((None, ml, mw), lambda b, t: (b, 0, 0))],
        out_specs=pl.BlockSpec((None, tm, mw), lambda b, t: (b, t, 0)),
        out_shape=jax.ShapeDtypeStruct((B, T, mw), BF16),
        compiler_params=_params("parallel", "parallel"),
        name="mem_attention",
    )(h_bf, h_bf, k_mem, v_mem)


def _outproj_kernel(*refs, segs, nk, n_blocks, alpha, eps, emit_bf16):
    ns = len(segs)
    a_refs = refs[:ns]
    if emit_bf16:
        w_ref, x_hbm, g_ref, b_ref, o_hbm, obf_hbm, acc, x_sem, o_sem, bf_buf, bf_sem = refs[ns:]
    else:
        w_ref, x_hbm, g_ref, b_ref, o_hbm, acc, x_sem, o_sem = refs[ns:]
    i = pl.program_id(0)
    k = pl.program_id(1)
    _, tm, d = acc.shape
    rb, cb = min(128, tm), min(512, d)
    n_rb = tm // rb
    slot = i % 2
    o_ref = acc.at[slot]

    def rows_of(blk):
        return pl.ds(pl.multiple_of(blk * tm, tm), tm)

    def x_copy(blk, s):
        return pltpu.make_async_copy(x_hbm.at[rows_of(blk), :], acc.at[s], x_sem.at[s])

    def o_copy(blk, s):
        return pltpu.make_async_copy(acc.at[s], o_hbm.at[rows_of(blk), :], o_sem.at[s])

    def bf_copy(r, bslot):
        row0 = pl.multiple_of(i * tm + r * rb, rb)
        return pltpu.make_async_copy(bf_buf.at[bslot], obf_hbm.at[pl.ds(row0, rb), :], bf_sem.at[bslot])

    @pl.when((i == 0) & (k == 0))
    def _():
        x_copy(0, 0).start()

    @pl.when(k == 0)
    def _():
        x_copy(i, slot).wait()
        a = a_refs[0][...]
        for c in range(0, d, cb):
            o_ref[:, c:c + cb] = alpha * o_ref[:, c:c + cb] + _dot(a, w_ref[:, c:c + cb])

    for a_ref, (start, n) in zip(a_refs, segs):
        @pl.when((k >= max(start, 1)) & (k < start + n))
        def _(a_ref=a_ref):
            a = a_ref[...]
            for c in range(0, d, cb):
                o_ref[:, c:c + cb] += _dot(a, w_ref[:, c:c + cb])

    @pl.when(k == nk - 1)
    def _():
        @pl.when(i >= 1)
        def _():
            o_copy(i - 1, 1 - slot).wait()

        @pl.when(i + 1 < n_blocks)
        def _():
            x_copy(i + 1, 1 - slot).start()

        def norm(r, carry):
            rows = pl.ds(pl.multiple_of(r * rb, rb), rb)
            y = o_ref[rows, :]
            mu = jnp.mean(y, axis=-1, keepdims=True)
            yc = y - mu
            var = jnp.mean(yc * yc, axis=-1, keepdims=True)
            out = yc * lax.rsqrt(var + eps) * g_ref[...] + b_ref[...]
            o_ref[rows, :] = out
            if emit_bf16:
                n_slot = bf_buf.shape[0]
                bslot = r % n_slot

                @pl.when(r >= n_slot)
                def _():
                    bf_copy(r - n_slot, bslot).wait()

                bf_buf[bslot] = out.astype(bf_buf.dtype)
                bf_copy(r, bslot).start()
            return carry

        lax.fori_loop(0, n_rb, norm, 0)
        o_copy(i, slot).start()
        if emit_bf16:
            for r in range(max(n_rb - bf_buf.shape[0], 0), n_rb):
                bf_copy(r, r % bf_buf.shape[0]).wait()

        @pl.when(i == n_blocks - 1)
        def _():
            o_copy(i, slot).wait()


def _outproj_ln(acts, w_bf, x2d, ln_g, ln_b, alpha, emit_bf16, tm=1024, tk=512):
    m, d = x2d.shape
    tm = min(tm, m)
    segs, start = [], 0
    for a in acts:
        assert a.shape[0] == m and a.shape[1] % tk == 0
        segs.append((start, a.shape[1] // tk))
        start += a.shape[1] // tk
    nk = start
    assert nk * tk == w_bf.shape[0]

    def a_spec(s0, n):
        return pl.BlockSpec((tm, tk), lambda i, k: (i, jnp.clip(k - s0, 0, n - 1)))

    kern = functools.partial(_outproj_kernel, segs=tuple(segs), nk=nk, n_blocks=m // tm, alpha=alpha, eps=1e-5,
                             emit_bf16=emit_bf16)
    rb = min(128, tm)
    out_specs = [pl.BlockSpec(memory_space=pl.ANY)]
    out_shape = [jax.ShapeDtypeStruct((m, d), F32)]
    scratch = [pltpu.VMEM((2, tm, d), F32), pltpu.SemaphoreType.DMA((2,)), pltpu.SemaphoreType.DMA((2,))]
    if emit_bf16:
        out_specs.append(pl.BlockSpec(memory_space=pl.ANY))
        out_shape.append(jax.ShapeDtypeStruct((m, d), BF16))
        n_slot = 4
        scratch += [pltpu.VMEM((n_slot, rb, d), BF16), pltpu.SemaphoreType.DMA((n_slot,))]
    outs = pl.pallas_call(
        kern,
        grid=(m // tm, nk),
        in_specs=[a_spec(s0, n) for s0, n in segs] + [
            pl.BlockSpec((tk, d), lambda i, k: (k, 0)),
            pl.BlockSpec(memory_space=pl.ANY),
            pl.BlockSpec((1, d), lambda i, k: (0, 0)),
            pl.BlockSpec((1, d), lambda i, k: (0, 0))],
        out_specs=out_specs,
        out_shape=out_shape,
        scratch_shapes=scratch,
        compiler_params=_params("arbitrary", "arbitrary"),
        name="outproj_layernorm",
    )(*acts, w_bf, x2d, ln_g, ln_b)
    return (outs[0], outs[1]) if emit_bf16 else (outs[0], None)


def _mlstm_pre_kernel(x_ref, halo_ref, cw_ref, cb_ref, wq_ref, wk_ref, wkt_ref, wv_ref,
                      iq_ref, ik_ref, iv_ref, bif_ref,
                      xc_ref, q_ref, k_ref, kt_ref, v_ref, g_ref, *, n_sub, bw, n_taps):
    t = pl.program_id(1)
    c = pl.program_id(2)
    tm = x_ref.shape[0]

    @pl.when(c == 0)
    def _():
        g_ref[...] = jnp.broadcast_to(bif_ref[...], g_ref.shape)

    gates = jnp.zeros(g_ref.shape, F32)
    for j in range(n_sub):
        cs = slice(j * bw, (j + 1) * bw)
        x_bf = x_ref[:, cs]
        x32 = x_bf.astype(F32)
        halo = jnp.where(t > 0, halo_ref[:, cs].astype(F32), 0.0)
        ext = jnp.concatenate([halo, x32], axis=0)
        acc = jnp.broadcast_to(cb_ref[:, cs], (tm, bw))
        for tap in range(n_taps):
            sh = 8 - (n_taps - 1) + tap
            acc = acc + ext[sh:sh + tm, :] * cw_ref[tap:tap + 1, cs]
        xc_bf = _silu(acc).astype(BF16)
        xc_ref[:, cs] = xc_bf
        q_bf = _dot(xc_bf, wq_ref[j]).astype(BF16)
        k_bf = _dot(xc_bf, wk_ref[j]).astype(BF16)
        v_bf = _dot(x_bf, wv_ref[j]).astype(BF16)
        q_ref[:, cs] = q_bf
        k_ref[:, cs] = k_bf
        v_ref[:, cs] = v_bf
        kt_ref[cs, :] = _dot_nt(wkt_ref[j], xc_bf).astype(BF16)
        gates = gates + _dot(q_bf, iq_ref[cs, :]) + _dot(k_bf, ik_ref[cs, :]) + _dot(v_bf, iv_ref[cs, :])
    g_ref[...] += gates


def _block_diag(w, bw):
    nb, bi, bo = w.shape
    assert bi == bo and bw % bi == 0 and (nb * bi) % bw == 0
    per = bw // bi
    w4 = w.reshape(nb // per, per, bi, bo)
    eye = jnp.eye(per, dtype=w.dtype)
    return jnp.einsum("cnio,nm->cnimo", w4, eye).reshape(nb // per, bw, bw)


def _mlstm_pre_call(h_bf, conv_w, conv_b, w_q, w_k, w_v, w_if, b_if, *, width, tm=512, tc=1024, bw=256):
    B, T, _ = h_bf.shape
    tm, tc = min(tm, T), min(tc, width)
    n_taps = conv_w.shape[0]
    assert tm % 8 == 0 and n_taps <= 8 and width % tc == 0 and tc % bw == 0
    n_sub = tc // bw
    wq = _block_diag(w_q, bw).astype(BF16)
    wk = _block_diag(w_k, bw)
    wkt = jnp.swapaxes(wk, 1, 2).astype(BF16)
    wk = wk.astype(BF16)
    wv = _block_diag(w_v, bw).astype(BF16)
    n_gate = w_if.shape[1]
    wif = jnp.pad(w_if, ((0, 0), (0, 128 - n_gate))).astype(BF16)
    bif = jnp.pad(b_if.astype(F32), (0, 128 - n_gate)).reshape(1, 128)
    hb = tm // 8
    tile_spec = pl.BlockSpec((n_sub, bw, bw), lambda b, t, c: (c, 0, 0))
    act_spec = pl.BlockSpec((None, tm, tc), lambda b, t, c: (b, t, c))
    kern = functools.partial(_mlstm_pre_kernel, n_sub=n_sub, bw=bw, n_taps=n_taps)
    shp = jax.ShapeDtypeStruct((B, T, width), BF16)
    return pl.pallas_call(
        kern,
        grid=(B, T // tm, width // tc),
        in_specs=[act_spec,
                  pl.BlockSpec((None, 8, tc), lambda b, t, c: (b, jnp.maximum(t * hb - 1, 0), c)),
                  pl.BlockSpec((n_taps, tc), lambda b, t, c: (0, c)),
                  pl.BlockSpec((1, tc), lambda b, t, c: (0, c)),
                  tile_spec, tile_spec, tile_spec, tile_spec,
                  pl.BlockSpec((tc, 128), lambda b, t, c: (c, 0)),
                  pl.BlockSpec((tc, 128), lambda b, t, c: (width // tc + c, 0)),
                  pl.BlockSpec((tc, 128), lambda b, t, c: (2 * (width // tc) + c, 0)),
                  pl.BlockSpec((1, 128), lambda b, t, c: (0, 0))],
        out_specs=[act_spec, act_spec, act_spec,
                   pl.BlockSpec((None, tc, tm), lambda b, t, c: (b, c, t)),
                   act_spec,
                   pl.BlockSpec((None, tm, 128), lambda b, t, c: (b, t, 0))],
        out_shape=[shp, shp, shp, jax.ShapeDtypeStruct((B, width, T), BF16), shp,
                   jax.ShapeDtypeStruct((B, T, 128), F32)],
        compiler_params=_params("parallel", "parallel", "arbitrary"),
        name="mlstm_frontend",
    )(h_bf, h_bf, conv_w, conv_b.reshape(1, width), wq, wk, wkt, wv, wif, wif, wif, bif)


def _mlstm_kernel(q_ref, k_ref, kt_ref, v_ref, gc_ref, gr_ref, xc_ref, z_ref, ng_ref, sk_ref, o_ref,
                  c_ref, hh_ref, n_ref, m_ref, *, n_chunks, heads, k_scale, dvb):
    h = pl.program_id(1)
    L = MLSTM_CHUNK
    dk, dv = c_ref.shape

    @pl.when(pl.program_id(2) == 0)
    def _():
        c_ref[...] = jnp.zeros_like(c_ref)
        n_ref[...] = jnp.zeros_like(n_ref)
        m_ref[...] = jnp.zeros_like(m_ref)

    row = lax.broadcasted_iota(jnp.int32, (L, L), 0)
    col = lax.broadcasted_iota(jnp.int32, (L, L), 1)
    causal = row >= col
    glane = lax.broadcasted_iota(jnp.int32, (L, gc_ref.shape[1]), 1)

    for c in range(n_chunks):
        rs = slice(c * L, (c + 1) * L)
        qc = q_ref[rs, :]
        kc = k_ref[rs, :]
        vc = v_ref[rs, :]
        gcol = gc_ref[rs, :]
        i_col = jnp.sum(jnp.where(glane == h, gcol, 0.0), axis=1, keepdims=True)
        f_col = jnp.sum(jnp.where(glane == heads + h, gcol, 0.0), axis=1, keepdims=True)
        i_row = gr_ref[0:1, rs]
        f_row = gr_ref[1:2, rs]
        lf_col = _log_sigmoid(f_col)
        lf_row = _log_sigmoid(f_row)
        b_col = jnp.sum(jnp.where(causal, lf_row, 0.0), axis=1, keepdims=True)
        b_row = jnp.sum(jnp.where(row <= col, lf_col, 0.0), axis=0, keepdims=True)
        b_last = b_col[L - 1:L, :]
        m_prev = m_ref[0:1, 0:1]

        d = jnp.where(causal, b_col - b_row + i_row, NEG_BIG)
        inter_log = b_col + m_prev
        m_i = jnp.maximum(inter_log, jnp.max(d, axis=1, keepdims=True))
        w_intra = jnp.exp(d - m_i)
        w_inter = jnp.exp(inter_log - m_i) * k_scale
        scores = _dot_nt(qc, kc) * k_scale * w_intra
        n_row = n_ref[...]
        den = (jnp.sum(scores, axis=1, keepdims=True)
               + w_inter * jnp.sum(qc.astype(F32) * n_row, axis=1, keepdims=True))
        inv = 1.0 / jnp.maximum(jnp.abs(den), jnp.exp(-m_i))
        scores_bf = scores.astype(BF16)

        log_wj_row = b_last - b_row + i_row
        log_wj_col = b_last - b_col + i_col
        m_new = jnp.maximum(b_last + m_prev, jnp.max(log_wj_row, axis=1, keepdims=True))
        wj_row = jnp.exp(log_wj_row - m_new)
        wj_col = jnp.exp(log_wj_col - m_new)
        dec = jnp.exp(b_last + m_prev - m_new)
        ktw = (kt_ref[:, rs].astype(F32) * wj_row).astype(BF16)

        for j in range(dv // dvb):
            cs = slice(j * dvb, (j + 1) * dvb)
            c_old = c_ref[:, cs]
            num = _dot(scores_bf, vc[:, cs]) + w_inter * _dot(qc, c_old.astype(BF16))
            hh_ref[:, cs] = num * inv
            c_ref[:, cs] = c_old * dec + _dot(ktw, vc[:, cs])
        n_ref[...] = n_row * dec + jnp.sum(kc.astype(F32) * wj_col, axis=0, keepdims=True)
        m_ref[...] = jnp.broadcast_to(m_new, m_ref.shape)

        hh = hh_ref[...]
        mu = jnp.mean(hh, axis=-1, keepdims=True)
        hc = hh - mu
        var = jnp.mean(hc * hc, axis=-1, keepdims=True)
        hn = hc * lax.rsqrt(var + 1e-6) * ng_ref[...]
        out = (hn + sk_ref[...] * xc_ref[rs, :].astype(F32)) * _silu(z_ref[rs, :].astype(F32))
        o_ref[rs, :] = out.astype(o_ref.dtype)


def _mlstm_call(q, k, kt, v, gates, gates_rows, xc, h_bf, norm_g, skip, *, heads, z_col, tb=MLSTM_CHUNK):
    B, T, width = q.shape
    dh = width // heads
    tb = min(tb, T)
    assert tb % MLSTM_CHUNK == 0 and z_col % dh == 0
    dvb = min(512, dh)
    hs = pl.BlockSpec((None, tb, dh), lambda b, h, t: (b, t, h))
    kern = functools.partial(_mlstm_kernel, n_chunks=tb // MLSTM_CHUNK, heads=heads,
                             k_scale=dh ** -0.5, dvb=dvb)
    return pl.pallas_call(
        kern,
        grid=(B, heads, T // tb),
        in_specs=[hs, hs,
                  pl.BlockSpec((None, dh, tb), lambda b, h, t: (b, h, t)),
                  hs,
                  pl.BlockSpec((None, tb, gates.shape[2]), lambda b, h, t: (b, t, 0)),
                  pl.BlockSpec((None, None, 2, tb), lambda b, h, t: (b, h, 0, t)),
                  hs,
                  pl.BlockSpec((None, tb, dh), lambda b, h, t: (b, t, z_col // dh + h)),
                  pl.BlockSpec((1, dh), lambda b, h, t: (0, h)),
                  pl.BlockSpec((1, dh), lambda b, h, t: (0, h))],
        out_specs=hs,
        out_shape=jax.ShapeDtypeStruct((B, T, width), BF16),
        scratch_shapes=[pltpu.VMEM((dh, dh), F32), pltpu.VMEM((MLSTM_CHUNK, dh), F32),
                        pltpu.VMEM((1, dh), F32), pltpu.VMEM((8, 128), F32)],
        compiler_params=_params("parallel", "parallel", "arbitrary"),
        name="mlstm_chunks",
    )(q, k, kt, v, gates, gates_rows, xc, h_bf, norm_g, skip)


def _mem_kv(mem_bf, w_k, w_v, B):
    mw = w_k.shape[1]
    k = _matmul(mem_bf, w_k.astype(BF16), BF16).reshape(B, -1, mw)
    v = _matmul(mem_bf, w_v.astype(BF16), BF16).reshape(B, -1, mw)
    return k, v


def _even_layer(x, x_bf, emit_bf16, mem_bf, lb, w_in_stack, li, gla_w_a2, gla_b_a, gla_norm_g, hgrn_norm_g,
                mem_w_k, mem_w_v, w_out, ln_g, ln_b, alpha):
    B, T, D = x.shape
    lowrank, kw = gla_w_a2.shape
    gdv = gla_norm_g.shape[0]
    gw = GLA_HEADS * gdv
    gdk = kw // GLA_HEADS
    fw = lb.shape[0]
    hheads = fw // HGRN_EXPAND
    hdv = hgrn_norm_g.shape[0]
    hw = hheads * hdv
    mw = mem_w_k.shape[1]
    sizes = (kw, kw, gw, gw, lowrank, fw, fw, hw, hw, mw, mw)
    assert sum(sizes) == w_in_stack.shape[2]
    bounds = [0]
    for s in sizes:
        bounds.append(bounds[-1] + s)
    ga_w = 256
    w_a = _window_cast(w_in_stack, li, 0, bounds[4])
    w_b = _window_cast(w_in_stack, li, bounds[5], bounds[-1] - bounds[5])
    w_ga = jnp.pad(w_in_stack[li, :, bounds[4]:bounds[5]], ((0, 0), (0, ga_w - lowrank))).astype(BF16)
    offs_a = {"gq": 0, "gk": kw, "gv": 2 * kw, "gg": 2 * kw + gw}
    offs_b = {"hq": 0, "hi": fw, "hg": fw + hw, "mq": fw + 2 * hw, "mg": fw + 2 * hw + mw}

    x2d = x.reshape(B * T, D)
    if x_bf is None:
        x_bf = x2d.astype(BF16)
    h_a = _matmul(x_bf, w_a, BF16)
    h_b = _matmul(x_bf, w_b, BF16, cols=[(0, fw), (2 * fw, bounds[-1] - bounds[5] - 2 * fw)])
    h_hf = _matmul(x_bf, w_b, F32, cols=[(fw, fw)])
    h_ga = _matmul(x_bf, w_ga, F32)
    h_a, h_b, h_hf, h_ga = (h.reshape(B, T, -1) for h in (h_a, h_b, h_hf, h_ga))

    wa_pad = jnp.pad(gla_w_a2, ((0, ga_w - lowrank), (0, 0))).astype(BF16)
    gla_out = _gla_call(h_a, h_ga, wa_pad, gla_b_a.reshape(1, kw), gla_norm_g.reshape(1, gdv),
                        offs=offs_a, ga_col=0, dk=gdk, dv=gdv, tb=512)
    hgrn_out = _hgrn_call(h_b, h_hf, lb.reshape(1, fw), hgrn_norm_g.reshape(1, hdv),
                          offs=offs_b, heads=hheads, dk=HGRN_EXPAND, dv=hdv, tb=512)
    k_mem, v_mem = _mem_kv(mem_bf, mem_w_k, mem_w_v, B)
    mem_out = _memattn_call(h_b, k_mem, v_mem, q_col=offs_b["mq"], g_col=offs_b["mg"])

    acts = [a.reshape(B * T, -1) for a in (gla_out, hgrn_out, mem_out)]
    y, y_bf = _outproj_ln(acts, w_out.astype(BF16), x2d, ln_g.reshape(1, D), ln_b.reshape(1, D), alpha, emit_bf16)
    return y.reshape(B, T, D), y_bf


def _odd_layer(x, x_bf, emit_bf16, mem_bf, w_in, conv_w, conv_b, w_q, w_k, w_v, w_if, b_if, mh_norm_g, skip,
               mem_w_k, mem_w_v, w_out, ln_g, ln_b, alpha):
    B, T, D = x.shape
    width = conv_w.shape[1]
    heads = b_if.shape[0] // 2
    mw = mem_w_k.shape[1]
    assert w_in.shape[1] == 2 * width + 2 * mw

    x2d = x.reshape(B * T, D)
    if x_bf is None:
        x_bf = x2d.astype(BF16)
    h_bf = _matmul(x_bf, w_in.astype(BF16), BF16).reshape(B, T, -1)

    xc, q, k, kt, v, gates = _mlstm_pre_call(h_bf, conv_w, conv_b, w_q, w_k, w_v, w_if, b_if, width=width)
    gates_rows = jnp.transpose(gates[:, :, :2 * heads].reshape(B, T, 2, heads), (0, 3, 2, 1))
    mlstm_out = _mlstm_call(q, k, kt, v, gates, gates_rows, xc, h_bf,
                            mh_norm_g.reshape(1, width), skip.reshape(1, width), heads=heads, z_col=width)
    k_mem, v_mem = _mem_kv(mem_bf, mem_w_k, mem_w_v, B)
    mem_out = _memattn_call(h_bf, k_mem, v_mem, q_col=2 * width, g_col=2 * width + mw)

    acts = [a.reshape(B * T, -1) for a in (mlstm_out, mem_out)]
    y, y_bf = _outproj_ln(acts, w_out.astype(BF16), x2d, ln_g.reshape(1, D), ln_b.reshape(1, D), alpha, emit_bf16)
    return y.reshape(B, T, D), y_bf


def kernel(x, mem, hgrn_lb_logits, ev_w_in, ev_gla_w_a2, ev_gla_b_a, ev_gla_norm_g, ev_hgrn_norm_g, ev_mem_w_k, ev_mem_w_v, ev_w_out, ev_ln_g, ev_ln_b, od_w_in, od_conv_w, od_conv_b, od_w_q, od_w_k, od_w_v, od_w_if, od_b_if, od_mh_norm_g, od_skip, od_mem_w_k, od_mem_w_v, od_w_out, od_ln_g, od_ln_b):
    depth = ev_w_in.shape[0] + od_w_in.shape[0]
    alpha = (2 * depth) ** 0.25
    B, ml, D = mem.shape
    mem_bf = mem.reshape(B * ml, D).astype(BF16)
    lb_all = jnp.cumsum(jax.nn.softmax(hgrn_lb_logits.astype(F32), axis=0), axis=0)
    x_bf = None
    for layer in range(depth):
        i = layer // 2
        emit_bf16 = layer + 1 < depth
        if layer % 2 == 0:
            x, x_bf = _even_layer(x, x_bf, emit_bf16, mem_bf, lb_all[layer], ev_w_in, i, ev_gla_w_a2[i],
                                  ev_gla_b_a[i], ev_gla_norm_g[i], ev_hgrn_norm_g[i], ev_mem_w_k[i],
                                  ev_mem_w_v[i], ev_w_out[i], ev_ln_g[i], ev_ln_b[i], alpha)
        else:
            x, x_bf = _odd_layer(x, x_bf, emit_bf16, mem_bf, od_w_in[i], od_conv_w[i], od_conv_b[i], od_w_q[i],
                                 od_w_k[i], od_w_v[i], od_w_if[i], od_b_if[i], od_mh_norm_g[i], od_skip[i],
                                 od_mem_w_k[i], od_mem_w_v[i], od_w_out[i], od_ln_g[i], od_ln_b[i], alpha)
    return x
```

```python
import functools

import jax
import jax.numpy as jnp
import numpy as np
from jax import lax
from jax.experimental import pallas as pl
from jax.experimental.pallas import tpu as pltpu

F32 = jnp.float32
BF16 = jnp.bfloat16

GLA_HEADS = 4
GLA_TAU = 16.0
HGRN_EXPAND = 128
MEM_HEADS = 4

LIN_CHUNK = 128
MLSTM_CHUNK = 256

V7X_VMEM_LIMIT_BYTES = 56 * 1024 * 1024
V7X_MXU_WIDTH = 256
NEG_BIG = -1e30
LOG2_E = 1.4426950408889634


def _sigmoid(x):
    return 1.0 / (1.0 + jnp.exp(-x))


def _silu(x):
    return x * _sigmoid(x)


def _log_sigmoid(x):
    return jnp.minimum(x, 0.0) - jnp.log(1.0 + jnp.exp(-jnp.abs(x)))


def _dot(a, b):
    return jnp.dot(a, b, preferred_element_type=F32)


def _dot_nt(a, b):
    return lax.dot_general(a, b, (((1,), (1,)), ((), ())), preferred_element_type=F32)


def _dot_tn(a, b):
    return lax.dot_general(a, b, (((0,), (0,)), ((), ())), preferred_element_type=F32)


def _params(*sem):
    return pltpu.CompilerParams(dimension_semantics=sem, vmem_limit_bytes=V7X_VMEM_LIMIT_BYTES)


def _mm_kernel(a_ref, w_ref, o_ref):
    o_ref[...] = _dot(a_ref[...], w_ref[...]).astype(o_ref.dtype)


def _matmul(a, w, out_dtype, tm=1024, tn=1024, cols=None):
    m, k = a.shape
    cols = [(0, w.shape[1])] if cols is None else list(cols)
    tm = min(tm, m)
    while any(v % tn for rng in cols for v in rng):
        tn //= 2
    assert m % tm == 0 and tn % 128 == 0, (m, cols, tm, tn)
    n_out = sum(n for _, n in cols)

    def w_block(i, j):
        blk, done = None, 0
        for first, n in cols:
            here = first // tn + (j - done)
            blk = here if blk is None else jnp.where(j >= done, here, blk)
            done += n // tn
        return 0, blk

    return pl.pallas_call(
        _mm_kernel,
        grid=(m // tm, n_out // tn),
        in_specs=[pl.BlockSpec((tm, k), lambda i, j: (i, 0)),
                  pl.BlockSpec((k, tn), w_block)],
        out_specs=pl.BlockSpec((tm, tn), lambda i, j: (i, j)),
        out_shape=jax.ShapeDtypeStruct((m, n_out), out_dtype),
        compiler_params=_params("parallel", "arbitrary"),
        name="proj_matmul",
    )(a, w)


def _mm_cast_kernel(a_ref, w_ref, o_ref, abf_ref):
    a_bf = a_ref[...].astype(abf_ref.dtype)
    abf_ref[...] = a_bf
    o_ref[...] = _dot(a_bf, w_ref[...]).astype(o_ref.dtype)


def _matmul_cast(a, w, out_dtype, tm=512):
    m, k = a.shape
    n = w.shape[1]
    tm = min(tm, m)
    assert m % tm == 0 and n % 128 == 0
    return pl.pallas_call(
        _mm_cast_kernel,
        grid=(m // tm,),
        in_specs=[pl.BlockSpec((tm, k), lambda i: (i, 0)), pl.BlockSpec((k, n), lambda i: (0, 0))],
        out_specs=[pl.BlockSpec((tm, n), lambda i: (i, 0)), pl.BlockSpec((tm, k), lambda i: (i, 0))],
        out_shape=[jax.ShapeDtypeStruct((m, n), out_dtype), jax.ShapeDtypeStruct((m, k), BF16)],
        compiler_params=_params("parallel"),
        name="proj_matmul_cast",
    )(a, w)


def _window_cast_kernel(x_ref, o_ref):
    o_ref[...] = x_ref[0].T.astype(o_ref.dtype)


def _window_cast(w_stack, layer, col0, ncols, tr=1024, tkb=1024):
    wt = jnp.swapaxes(w_stack, 1, 2)
    _, n, k = wt.shape
    tkb = min(tkb, k)
    while ncols % tr:
        tr //= 2
    assert col0 % 8 == 0 and tr % 128 == 0 and k % tkb == 0 and col0 + ncols <= n, (col0, ncols, tr)
    return pl.pallas_call(
        _window_cast_kernel,
        grid=(ncols // tr, k // tkb),
        in_specs=[pl.BlockSpec((pl.Element(1), pl.Element(tr), pl.Element(tkb)),
                               lambda i, j: (layer, pl.multiple_of(col0 + i * tr, 8), pl.multiple_of(j * tkb, 128)))],
        out_specs=pl.BlockSpec((tkb, tr), lambda i, j: (j, i)),
        out_shape=jax.ShapeDtypeStruct((k, ncols), BF16),
        compiler_params=_params("parallel", "parallel"),
        name="weight_window_cast",
    )(wt)


def _lin_constants(L):
    i = np.arange(L)[:, None]
    t = np.arange(L)[None, :]
    sums = [t <= i]
    masks = [i == t]
    m = L // 2
    while m >= 1:
        r = (i // (2 * m)) * 2 * m + m - 1
        second = (i & m) != 0
        sums.append(np.where(second, (t > r) & (t <= i), (t > i) & (t <= r)))
        masks.append((i // (2 * m) == t // (2 * m)) & second & ((t & m) == 0))
        m //= 2
    sums.append(t > i)
    s = np.concatenate(sums, axis=0).astype(np.float32)
    return (jnp.asarray(np.concatenate([s, s], axis=1), BF16),
            jnp.asarray(np.stack(masks).astype(np.float32)))


def _block_rows(x, grp, d):
    if grp == 1:
        return x
    lane = lax.broadcasted_iota(jnp.int32, x.shape, 1)
    zero = jnp.zeros_like(x)
    return jnp.concatenate([jnp.where((lane >= r * d) & (lane < (r + 1) * d), x, zero) for r in range(grp)], axis=0)


def _lin_attn_chunk(q, k, v_bf, g, st_ref, sums_ref, mask_ref, nh):
    L, wk = q.shape
    dk = wk // nh
    dv = v_bf.shape[1] // nh
    grp = st_ref.shape[1] // dv
    n_lvl = mask_ref.shape[0] - 1
    g2 = g * LOG2_E
    hi = g2.astype(BF16)
    lo = (g2 - hi.astype(F32)).astype(BF16)
    p = jnp.exp2(_dot(sums_ref[...], jnp.concatenate([hi, lo], axis=0)))
    p_last = p[L - 1:L, :]
    qd = (q * p[0:L]).astype(BF16)
    kd = (k * p[(n_lvl + 1) * L:(n_lvl + 2) * L]).astype(BF16)
    q_bf = q.astype(BF16)
    k_bf = k.astype(BF16)
    rowi = lax.broadcasted_iota(jnp.int32, (L, 1), 0)
    xs = []
    for lvl in range(n_lvl):
        second = (rowi & (L >> (lvl + 1))) != 0
        xs.append((jnp.where(second, q, k) * p[(lvl + 1) * L:(lvl + 2) * L]).astype(BF16))
    def mask(l):
        m = mask_ref[l]
        return m if grp == 1 else jnp.concatenate([m] * grp, axis=1)

    outs = []
    for h in range(nh // grp):
        ck = slice(h * grp * dk, (h + 1) * grp * dk)
        cv = slice(h * grp * dv, (h + 1) * grp * dv)
        a = _dot_nt(q_bf[:, ck], _block_rows(k_bf[:, ck], grp, dk)) * mask(0)
        for lvl in range(n_lvl):
            x = xs[lvl][:, ck]
            a = a + _dot_nt(x, _block_rows(x, grp, dk)) * mask(lvl + 1)
        st = st_ref[h]
        v_blk = _block_rows(v_bf[:, cv], grp, dv)
        o = _dot_nt(qd[:, ck], st.astype(BF16)) + _dot(a.astype(BF16), v_blk)
        st_ref[h] = st * p_last[:, ck] + _dot_tn(v_blk, _block_rows(kd[:, ck], grp, dk))
        outs.extend(o[:, r * dv:(r + 1) * dv] for r in range(grp))
    return outs


def _rms_gate(o, norm_g, gate, eps=1e-6):
    ms = jnp.mean(o * o, axis=-1, keepdims=True)
    return o * lax.rsqrt(ms + eps) * norm_g * _silu(gate)


def _gla_kernel(q_ref, k_ref, v_ref, gg_ref, ga_ref, wa_ref, ba_ref, ng_ref, sums_ref, mask_ref, o_ref,
                st_ref, *, n_chunks, q_scale):
    @pl.when(pl.program_id(2) == 0)
    def _():
        st_ref[...] = jnp.zeros_like(st_ref)

    def chunk(c, carry):
        rows = pl.ds(pl.multiple_of(c * LIN_CHUNK, LIN_CHUNK), LIN_CHUNK)
        q = q_ref[rows, :].astype(F32) * q_scale
        k = k_ref[rows, :].astype(F32)
        a_pre = _dot(ga_ref[rows, :].astype(BF16), wa_ref[...]) + ba_ref[...]
        g = _log_sigmoid(a_pre) * (1.0 / GLA_TAU)
        (o,) = _lin_attn_chunk(q, k, v_ref[rows, :], g, st_ref, sums_ref, mask_ref, 1)
        o_ref[rows, :] = _rms_gate(o, ng_ref[...], gg_ref[rows, :].astype(F32)).astype(o_ref.dtype)
        return carry

    lax.fori_loop(0, n_chunks, chunk, 0, unroll=4)


def _hgrn_kernel(q_ref, f_ref, v_ref, hg_ref, lb_ref, ng_ref, sums_ref, mask_ref, o_ref,
                 st_ref, *, n_chunks, nh):
    dv = ng_ref.shape[1]

    @pl.when(pl.program_id(2) == 0)
    def _():
        st_ref[...] = jnp.zeros_like(st_ref)

    def chunk(c, carry):
        rows = pl.ds(pl.multiple_of(c * LIN_CHUNK, LIN_CHUNK), LIN_CHUNK)
        lb = lb_ref[...]
        f = lb + (1.0 - lb) * _sigmoid(f_ref[rows, :])
        q = _silu(q_ref[rows, :].astype(F32))
        outs = _lin_attn_chunk(q, 1.0 - f, v_ref[rows, :], jnp.log(f), st_ref, sums_ref, mask_ref, nh)
        gate = hg_ref[rows, :].astype(F32)
        for h, o in enumerate(outs):
            cv = slice(h * dv, (h + 1) * dv)
            o_ref[rows, cv] = _rms_gate(o, ng_ref[...], gate[:, cv]).astype(o_ref.dtype)
        return carry

    lax.fori_loop(0, n_chunks, chunk, 0, unroll=4)


def _col_spec(tb, width, col0):
    assert col0 % width == 0, (col0, width)
    base = col0 // width
    return pl.BlockSpec((None, tb, width), lambda b, h, t: (b, t, base + h))


def _const_specs(*arrays):
    return [pl.BlockSpec(a.shape, lambda b, h, t, nd=a.ndim: (0,) * nd) for a in arrays]


def _gla_call(h_bf, h_f, wa_pad, b_a, norm_g, *, offs, ga_col, dk, dv, tb):
    B, T, _ = h_bf.shape
    tb = min(tb, T)
    ga_w = wa_pad.shape[0]
    assert ga_col % ga_w == 0
    consts = _lin_constants(LIN_CHUNK)
    kern = functools.partial(_gla_kernel, n_chunks=tb // LIN_CHUNK, q_scale=dk ** -0.5)
    return pl.pallas_call(
        kern,
        grid=(B, GLA_HEADS, T // tb),
        in_specs=[_col_spec(tb, dk, offs["gq"]), _col_spec(tb, dk, offs["gk"]),
                  _col_spec(tb, dv, offs["gv"]), _col_spec(tb, dv, offs["gg"]),
                  pl.BlockSpec((None, tb, ga_w), lambda b, h, t: (b, t, ga_col // ga_w)),
                  pl.BlockSpec((ga_w, dk), lambda b, h, t: (0, h)),
                  pl.BlockSpec((1, dk), lambda b, h, t: (0, h)),
                  pl.BlockSpec((1, dv), lambda b, h, t: (0, 0))] + _const_specs(*consts),
        out_specs=pl.BlockSpec((None, tb, dv), lambda b, h, t: (b, t, h)),
        out_shape=jax.ShapeDtypeStruct((B, T, GLA_HEADS * dv), BF16),
        scratch_shapes=[pltpu.VMEM((1, dv, dk), F32)],
        compiler_params=_params("parallel", "parallel", "arbitrary"),
        name="gla_chunks",
    )(h_bf, h_bf, h_bf, h_bf, h_f, wa_pad, b_a, norm_g, *consts)


def _hgrn_call(h_bf, h_f, lb, norm_g, *, offs, heads, dk, dv, tb, nh=4):
    B, T, _ = h_bf.shape
    tb = min(tb, T)
    nh = min(nh, heads)
    assert heads % nh == 0
    grp = 2 if (nh % 2 == 0 and dk == dv and 2 * dk == V7X_MXU_WIDTH) else 1
    consts = _lin_constants(LIN_CHUNK)
    kern = functools.partial(_hgrn_kernel, n_chunks=tb // LIN_CHUNK, nh=nh)
    return pl.pallas_call(
        kern,
        grid=(B, heads // nh, T // tb),
        in_specs=[_col_spec(tb, nh * dk, offs["hq"]),
                  pl.BlockSpec((None, tb, nh * dk), lambda b, h, t: (b, t, h)),
                  _col_spec(tb, nh * dv, offs["hi"]), _col_spec(tb, nh * dv, offs["hg"]),
                  pl.BlockSpec((1, nh * dk), lambda b, h, t: (0, h)),
                  pl.BlockSpec((1, dv), lambda b, h, t: (0, 0))] + _const_specs(*consts),
        out_specs=pl.BlockSpec((None, tb, nh * dv), lambda b, h, t: (b, t, h)),
        out_shape=jax.ShapeDtypeStruct((B, T, heads * dv), BF16),
        scratch_shapes=[pltpu.VMEM((nh // grp, grp * dv, grp * dk), F32)],
        compiler_params=_params("parallel", "parallel", "arbitrary"),
        name="hgrn_chunks",
    )(h_bf, h_f, h_bf, h_bf, lb, norm_g, *consts)


def _memattn_kernel(q_ref, g_ref, k_ref, v_ref, o_ref, *, dh):
    scale = dh ** -0.5
    for h in range(MEM_HEADS):
        cs = slice(h * dh, (h + 1) * dh)
        s = _dot_nt(q_ref[:, cs], k_ref[:, cs]) * scale
        s = s - jnp.max(s, axis=-1, keepdims=True)
        p = jnp.exp(s)
        p = p / jnp.sum(p, axis=-1, keepdims=True)
        o = _dot(p.astype(BF16), v_ref[:, cs])
        o_ref[:, cs] = (o * _silu(g_ref[:, cs].astype(F32))).astype(o_ref.dtype)


def _memattn_call(h_bf, k_mem, v_mem, *, q_col, g_col, tm=512):
    B, T, _ = h_bf.shape
    _, ml, mw = k_mem.shape
    tm = min(tm, T)
    assert q_col % mw == 0 and g_col % mw == 0
    return pl.pallas_call(
        functools.partial(_memattn_kernel, dh=mw // MEM_HEADS),
        grid=(B, T // tm),
        in_specs=[pl.BlockSpec((None, tm, mw), lambda b, t: (b, t, q_col // mw)),
                  pl.BlockSpec((None, tm, mw), lambda b, t: (b, t, g_col // mw)),
                  pl.BlockSpec((None, ml, mw), lambda b, t: (b, 0, 0)),
                  pl.BlockSpec((None, ml, mw), lambda b, t: (b, 0, 0))],
        out_specs=pl.BlockSpec((None, tm, mw), lambda b, t: (b, t, 0)),
        out_shape=jax.ShapeDtypeStruct((B, T, mw), BF16),
        compiler_params=_params("parallel", "parallel"),
        name="mem_attention",
    )(h_bf, h_bf, k_mem, v_mem)


def _outproj_kernel(*refs, segs, nk, n_blocks, alpha, eps, emit_bf16):
    ns = len(segs)
    a_refs = refs[:ns]
    if emit_bf16:
        w_ref, x_hbm, g_ref, b_ref, o_hbm, obf_hbm, acc, x_sem, o_sem, bf_buf, bf_sem = refs[ns:]
    else:
        w_ref, x_hbm, g_ref, b_ref, o_hbm, acc, x_sem, o_sem = refs[ns:]
    i = pl.program_id(0)
    k = pl.program_id(1)
    _, tm, d = acc.shape
    rb, cb = min(128, tm), min(512, d)
    n_rb = tm // rb
    slot = i % 2
    o_ref = acc.at[slot]

    def rows_of(blk):
        return pl.ds(pl.multiple_of(blk * tm, tm), tm)

    def x_copy(blk, s):
        return pltpu.make_async_copy(x_hbm.at[rows_of(blk), :], acc.at[s], x_sem.at[s])

    def o_copy(blk, s):
        return pltpu.make_async_copy(acc.at[s], o_hbm.at[rows_of(blk), :], o_sem.at[s])

    def bf_copy(r, bslot):
        row0 = pl.multiple_of(i * tm + r * rb, rb)
        return pltpu.make_async_copy(bf_buf.at[bslot], obf_hbm.at[pl.ds(row0, rb), :], bf_sem.at[bslot])

    @pl.when((i == 0) & (k == 0))
    def _():
        x_copy(0, 0).start()

    @pl.when(k == 0)
    def _():
        x_copy(i, slot).wait()
        a = a_refs[0][...]
        for c in range(0, d, cb):
            o_ref[:, c:c + cb] = alpha * o_ref[:, c:c + cb] + _dot(a, w_ref[:, c:c + cb])

    for a_ref, (start, n) in zip(a_refs, segs):
        @pl.when((k >= max(start, 1)) & (k < start + n))
        def _(a_ref=a_ref):
            a = a_ref[...]
            for c in range(0, d, cb):
                o_ref[:, c:c + cb] += _dot(a, w_ref[:, c:c + cb])

    @pl.when(k == nk - 1)
    def _():
        @pl.when(i >= 1)
        def _():
            o_copy(i - 1, 1 - slot).wait()

        @pl.when(i + 1 < n_blocks)
        def _():
            x_copy(i + 1, 1 - slot).start()

        def norm(r, carry):
            rows = pl.ds(pl.multiple_of(r * rb, rb), rb)
            y = o_ref[rows, :]
            mu = jnp.mean(y, axis=-1, keepdims=True)
            yc = y - mu
            var = jnp.mean(yc * yc, axis=-1, keepdims=True)
            out = yc * lax.rsqrt(var + eps) * g_ref[...] + b_ref[...]
            o_ref[rows, :] = out
            if emit_bf16:
                n_slot = bf_buf.shape[0]
                bslot = r % n_slot

                @pl.when(r >= n_slot)
                def _():
                    bf_copy(r - n_slot, bslot).wait()

                bf_buf[bslot] = out.astype(bf_buf.dtype)
                bf_copy(r, bslot).start()
            return carry

        lax.fori_loop(0, n_rb, norm, 0)
        o_copy(i, slot).start()
        if emit_bf16:
            for r in range(max(n_rb - bf_buf.shape[0], 0), n_rb):
                bf_copy(r, r % bf_buf.shape[0]).wait()

        @pl.when(i == n_blocks - 1)
        def _():
            o_copy(i, slot).wait()


def _outproj_ln(acts, w_bf, x2d, ln_g, ln_b, alpha, emit_bf16, tm=1024, tk=512):
    m, d = x2d.shape
    tm = min(tm, m)
    segs, start = [], 0
    for a in acts:
        assert a.shape[0] == m and a.shape[1] % tk == 0
        segs.append((start, a.shape[1] // tk))
        start += a.shape[1] // tk
    nk = start
    assert nk * tk == w_bf.shape[0]

    def a_spec(s0, n):
        return pl.BlockSpec((tm, tk), lambda i, k: (i, jnp.clip(k - s0, 0, n - 1)))

    kern = functools.partial(_outproj_kernel, segs=tuple(segs), nk=nk, n_blocks=m // tm, alpha=alpha, eps=1e-5,
                             emit_bf16=emit_bf16)
    rb = min(128, tm)
    out_specs = [pl.BlockSpec(memory_space=pl.ANY)]
    out_shape = [jax.ShapeDtypeStruct((m, d), F32)]
    scratch = [pltpu.VMEM((2, tm, d), F32), pltpu.SemaphoreType.DMA((2,)), pltpu.SemaphoreType.DMA((2,))]
    if emit_bf16:
        out_specs.append(pl.BlockSpec(memory_space=pl.ANY))
        out_shape.append(jax.ShapeDtypeStruct((m, d), BF16))
        n_slot = 4
        scratch += [pltpu.VMEM((n_slot, rb, d), BF16), pltpu.SemaphoreType.DMA((n_slot,))]
    outs = pl.pallas_call(
        kern,
        grid=(m // tm, nk),
        in_specs=[a_spec(s0, n) for s0, n in segs] + [
            pl.BlockSpec((tk, d), lambda i, k: (k, 0)),
            pl.BlockSpec(memory_space=pl.ANY),
            pl.BlockSpec((1, d), lambda i, k: (0, 0)),
            pl.BlockSpec((1, d), lambda i, k: (0, 0))],
        out_specs=out_specs,
        out_shape=out_shape,
        scratch_shapes=scratch,
        compiler_params=_params("arbitrary", "arbitrary"),
        name="outproj_layernorm",
    )(*acts, w_bf, x2d, ln_g, ln_b)
    return (outs[0], outs[1]) if emit_bf16 else (outs[0], None)


def _mlstm_pre_kernel(x_ref, halo_ref, cw_ref, cb_ref, wq_ref, wk_ref, wkt_ref, wv_ref,
                      iq_ref, ik_ref, iv_ref, bif_ref,
                      xc_ref, q_ref, k_ref, kt_ref, v_ref, g_ref, *, n_sub, bw, n_taps):
    t = pl.program_id(1)
    c = pl.program_id(2)
    tm = x_ref.shape[0]

    @pl.when(c == 0)
    def _():
        g_ref[...] = jnp.broadcast_to(bif_ref[...], g_ref.shape)

    gates = jnp.zeros(g_ref.shape, F32)
    for j in range(n_sub):
        cs = slice(j * bw, (j + 1) * bw)
        x_bf = x_ref[:, cs]
        x32 = x_bf.astype(F32)
        halo = jnp.where(t > 0, halo_ref[:, cs].astype(F32), 0.0)
        ext = jnp.concatenate([halo, x32], axis=0)
        acc = jnp.broadcast_to(cb_ref[:, cs], (tm, bw))
        for tap in range(n_taps):
            sh = 8 - (n_taps - 1) + tap
            acc = acc + ext[sh:sh + tm, :] * cw_ref[tap:tap + 1, cs]
        xc_bf = _silu(acc).astype(BF16)
        xc_ref[:, cs] = xc_bf
        q_bf = _dot(xc_bf, wq_ref[j]).astype(BF16)
        k_bf = _dot(xc_bf, wk_ref[j]).astype(BF16)
        v_bf = _dot(x_bf, wv_ref[j]).astype(BF16)
        q_ref[:, cs] = q_bf
        k_ref[:, cs] = k_bf
        v_ref[:, cs] = v_bf
        kt_ref[cs, :] = _dot_nt(wkt_ref[j], xc_bf).astype(BF16)
        gates = gates + _dot(q_bf, iq_ref[cs, :]) + _dot(k_bf, ik_ref[cs, :]) + _dot(v_bf, iv_ref[cs, :])
    g_ref[...] += gates


def _block_diag(w, bw):
    nb, bi, bo = w.shape
    assert bi == bo and bw % bi == 0 and (nb * bi) % bw == 0
    per = bw // bi
    w4 = w.reshape(nb // per, per, bi, bo)
    eye = jnp.eye(per, dtype=w.dtype)
    return jnp.einsum("cnio,nm->cnimo", w4, eye).reshape(nb // per, bw, bw)


def _mlstm_pre_call(h_bf, conv_w, conv_b, w_q, w_k, w_v, w_if, b_if, *, width, tm=512, tc=1024, bw=256):
    B, T, _ = h_bf.shape
    tm, tc = min(tm, T), min(tc, width)
    n_taps = conv_w.shape[0]
    assert tm % 8 == 0 and n_taps <= 8 and width % tc == 0 and tc % bw == 0
    n_sub = tc // bw
    wq = _block_diag(w_q, bw).astype(BF16)
    wk = _block_diag(w_k, bw)
    wkt = jnp.swapaxes(wk, 1, 2).astype(BF16)
    wk = wk.astype(BF16)
    wv = _block_diag(w_v, bw).astype(BF16)
    n_gate = w_if.shape[1]
    wif = jnp.pad(w_if, ((0, 0), (0, 128 - n_gate))).astype(BF16)
    bif = jnp.pad(b_if.astype(F32), (0, 128 - n_gate)).reshape(1, 128)
    hb = tm // 8
    tile_spec = pl.BlockSpec((n_sub, bw, bw), lambda b, t, c: (c, 0, 0))
    act_spec = pl.BlockSpec((None, tm, tc), lambda b, t, c: (b, t, c))
    kern = functools.partial(_mlstm_pre_kernel, n_sub=n_sub, bw=bw, n_taps=n_taps)
    shp = jax.ShapeDtypeStruct((B, T, width), BF16)
    return pl.pallas_call(
        kern,
        grid=(B, T // tm, width // tc),
        in_specs=[act_spec,
                  pl.BlockSpec((None, 8, tc), lambda b, t, c: (b, jnp.maximum(t * hb - 1, 0), c)),
                  pl.BlockSpec((n_taps, tc), lambda b, t, c: (0, c)),
                  pl.BlockSpec((1, tc), lambda b, t, c: (0, c)),
                  tile_spec, tile_spec, tile_spec, tile_spec,
                  pl.BlockSpec((tc, 128), lambda b, t, c: (c, 0)),
                  pl.BlockSpec((tc, 128), lambda b, t, c: (width // tc + c, 0)),
                  pl.BlockSpec((tc, 128), lambda b, t, c: (2 * (width // tc) + c, 0)),
                  pl.BlockSpec((1, 128), lambda b, t, c: (0, 0))],
        out_specs=[act_spec, act_spec, act_spec,
                   pl.BlockSpec((None, tc, tm), lambda b, t, c: (b, c, t)),
                   act_spec,
                   pl.BlockSpec((None, tm, 128), lambda b, t, c: (b, t, 0))],
        out_shape=[shp, shp, shp, jax.ShapeDtypeStruct((B, width, T), BF16), shp,
                   jax.ShapeDtypeStruct((B, T, 128), F32)],
        compiler_params=_params("parallel", "parallel", "arbitrary"),
        name="mlstm_frontend",
    )(h_bf, h_bf, conv_w, conv_b.reshape(1, width), wq, wk, wkt, wv, wif, wif, wif, bif)


def _mlstm_kernel(q_ref, k_ref, kt_ref, v_ref, gc_ref, gr_ref, xc_ref, z_ref, ng_ref, sk_ref, o_ref,
                  c_ref, hh_ref, n_ref, m_ref, *, n_chunks, heads, k_scale, dvb):
    h = pl.program_id(1)
    L = MLSTM_CHUNK
    dk, dv = c_ref.shape

    @pl.when(pl.program_id(2) == 0)
    def _():
        c_ref[...] = jnp.zeros_like(c_ref)
        n_ref[...] = jnp.zeros_like(n_ref)
        m_ref[...] = jnp.zeros_like(m_ref)

    row = lax.broadcasted_iota(jnp.int32, (L, L), 0)
    col = lax.broadcasted_iota(jnp.int32, (L, L), 1)
    causal = row >= col
    glane = lax.broadcasted_iota(jnp.int32, (L, gc_ref.shape[1]), 1)

    for c in range(n_chunks):
        rs = slice(c * L, (c + 1) * L)
        qc = q_ref[rs, :]
        kc = k_ref[rs, :]
        vc = v_ref[rs, :]
        gcol = gc_ref[rs, :]
        i_col = jnp.sum(jnp.where(glane == h, gcol, 0.0), axis=1, keepdims=True)
        f_col = jnp.sum(jnp.where(glane == heads + h, gcol, 0.0), axis=1, keepdims=True)
        i_row = gr_ref[0:1, rs]
        f_row = gr_ref[1:2, rs]
        lf_col = _log_sigmoid(f_col)
        lf_row = _log_sigmoid(f_row)
        b_col = jnp.sum(jnp.where(causal, lf_row, 0.0), axis=1, keepdims=True)
        b_row = jnp.sum(jnp.where(row <= col, lf_col, 0.0), axis=0, keepdims=True)
        b_last = b_col[L - 1:L, :]
        m_prev = m_ref[0:1, 0:1]

        d = jnp.where(causal, b_col - b_row + i_row, NEG_BIG)
        inter_log = b_col + m_prev
        m_i = jnp.maximum(inter_log, jnp.max(d, axis=1, keepdims=True))
        w_intra = jnp.exp(d - m_i)
        w_inter = jnp.exp(inter_log - m_i) * k_scale
        scores = _dot_nt(qc, kc) * k_scale * w_intra
        n_row = n_ref[...]
        den = (jnp.sum(scores, axis=1, keepdims=True)
               + w_inter * jnp.sum(qc.astype(F32) * n_row, axis=1, keepdims=True))
        inv = 1.0 / jnp.maximum(jnp.abs(den), jnp.exp(-m_i))
        scores_bf = scores.astype(BF16)

        log_wj_row = b_last - b_row + i_row
        log_wj_col = b_last - b_col + i_col
        m_new = jnp.maximum(b_last + m_prev, jnp.max(log_wj_row, axis=1, keepdims=True))
        wj_row = jnp.exp(log_wj_row - m_new)
        wj_col = jnp.exp(log_wj_col - m_new)
        dec = jnp.exp(b_last + m_prev - m_new)
        ktw = (kt_ref[:, rs].astype(F32) * wj_row).astype(BF16)

        for j in range(dv // dvb):
            cs = slice(j * dvb, (j + 1) * dvb)
            c_old = c_ref[:, cs]
            num = _dot(scores_bf, vc[:, cs]) + w_inter * _dot(qc, c_old.astype(BF16))
            hh_ref[:, cs] = num * inv
            c_ref[:, cs] = c_old * dec + _dot(ktw, vc[:, cs])
        n_ref[...] = n_row * dec + jnp.sum(kc.astype(F32) * wj_col, axis=0, keepdims=True)
        m_ref[...] = jnp.broadcast_to(m_new, m_ref.shape)

        hh = hh_ref[...]
        mu = jnp.mean(hh, axis=-1, keepdims=True)
        hc = hh - mu
        var = jnp.mean(hc * hc, axis=-1, keepdims=True)
        hn = hc * lax.rsqrt(var + 1e-6) * ng_ref[...]
        out = (hn + sk_ref[...] * xc_ref[rs, :].astype(F32)) * _silu(z_ref[rs, :].astype(F32))
        o_ref[rs, :] = out.astype(o_ref.dtype)


def _mlstm_call(q, k, kt, v, gates, gates_rows, xc, h_bf, norm_g, skip, *, heads, z_col, tb=MLSTM_CHUNK):
    B, T, width = q.shape
    dh = width // heads
    tb = min(tb, T)
    assert tb % MLSTM_CHUNK == 0 and z_col % dh == 0
    dvb = min(512, dh)
    hs = pl.BlockSpec((None, tb, dh), lambda b, h, t: (b, t, h))
    kern = functools.partial(_mlstm_kernel, n_chunks=tb // MLSTM_CHUNK, heads=heads,
                             k_scale=dh ** -0.5, dvb=dvb)
    return pl.pallas_call(
        kern,
        grid=(B, heads, T // tb),
        in_specs=[hs, hs,
                  pl.BlockSpec((None, dh, tb), lambda b, h, t: (b, h, t)),
                  hs,
                  pl.BlockSpec((None, tb, gates.shape[2]), lambda b, h, t: (b, t, 0)),
                  pl.BlockSpec((None, None, 2, tb), lambda b, h, t: (b, h, 0, t)),
                  hs,
                  pl.BlockSpec((None, tb, dh), lambda b, h, t: (b, t, z_col // dh + h)),
                  pl.BlockSpec((1, dh), lambda b, h, t: (0, h)),
                  pl.BlockSpec((1, dh), lambda b, h, t: (0, h))],
        out_specs=hs,
        out_shape=jax.ShapeDtypeStruct((B, T, width), BF16),
        scratch_shapes=[pltpu.VMEM((dh, dh), F32), pltpu.VMEM((MLSTM_CHUNK, dh), F32),
                        pltpu.VMEM((1, dh), F32), pltpu.VMEM((8, 128), F32)],
        compiler_params=_params("parallel", "parallel", "arbitrary"),
        name="mlstm_chunks",
    )(q, k, kt, v, gates, gates_rows, xc, h_bf, norm_g, skip)


def _mem_kv(mem_bf, w_k, w_v, B):
    mw = w_k.shape[1]
    k = _matmul(mem_bf, w_k.astype(BF16), BF16).reshape(B, -1, mw)
    v = _matmul(mem_bf, w_v.astype(BF16), BF16).reshape(B, -1, mw)
    return k, v


def _even_layer(x, x_bf, emit_bf16, mem_bf, lb, w_in_stack, li, gla_w_a2, gla_b_a, gla_norm_g, hgrn_norm_g,
                mem_w_k, mem_w_v, w_out, ln_g, ln_b, alpha):
    B, T, D = x.shape
    lowrank, kw = gla_w_a2.shape
    gdv = gla_norm_g.shape[0]
    gw = GLA_HEADS * gdv
    gdk = kw // GLA_HEADS
    fw = lb.shape[0]
    hheads = fw // HGRN_EXPAND
    hdv = hgrn_norm_g.shape[0]
    hw = hheads * hdv
    mw = mem_w_k.shape[1]
    sizes = (kw, kw, gw, gw, lowrank, fw, fw, hw, hw, mw, mw)
    assert sum(sizes) == w_in_stack.shape[2]
    bounds = [0]
    for s in sizes:
        bounds.append(bounds[-1] + s)
    ga_w = 256
    w_a = _window_cast(w_in_stack, li, 0, bounds[4])
    w_b = _window_cast(w_in_stack, li, bounds[5], bounds[-1] - bounds[5])
    w_ga = jnp.pad(w_in_stack[li, :, bounds[4]:bounds[5]], ((0, 0), (0, ga_w - lowrank))).astype(BF16)
    offs_a = {"gq": 0, "gk": kw, "gv": 2 * kw, "gg": 2 * kw + gw}
    offs_b = {"hq": 0, "hi": fw, "hg": fw + hw, "mq": fw + 2 * hw, "mg": fw + 2 * hw + mw}

    x2d = x.reshape(B * T, D)
    if x_bf is None:
        h_ga, x_bf = _matmul_cast(x2d, w_ga, F32)
    else:
        h_ga = _matmul(x_bf, w_ga, F32)
    h_a = _matmul(x_bf, w_a, BF16)
    h_b = _matmul(x_bf, w_b, BF16, cols=[(0, fw), (2 * fw, bounds[-1] - bounds[5] - 2 * fw)])
    h_hf = _matmul(x_bf, w_b, F32, cols=[(fw, fw)])
    h_a, h_b, h_hf, h_ga = (h.reshape(B, T, -1) for h in (h_a, h_b, h_hf, h_ga))

    wa_pad = jnp.pad(gla_w_a2, ((0, ga_w - lowrank), (0, 0))).astype(BF16)
    gla_out = _gla_call(h_a, h_ga, wa_pad, gla_b_a.reshape(1, kw), gla_norm_g.reshape(1, gdv),
                        offs=offs_a, ga_col=0, dk=gdk, dv=gdv, tb=512)
    hgrn_out = _hgrn_call(h_b, h_hf, lb.reshape(1, fw), hgrn_norm_g.reshape(1, hdv),
                          offs=offs_b, heads=hheads, dk=HGRN_EXPAND, dv=hdv, tb=512)
    k_mem, v_mem = _mem_kv(mem_bf, mem_w_k, mem_w_v, B)
    mem_out = _memattn_call(h_b, k_mem, v_mem, q_col=offs_b["mq"], g_col=offs_b["mg"])

    acts = [a.reshape(B * T, -1) for a in (gla_out, hgrn_out, mem_out)]
    y, y_bf = _outproj_ln(acts, w_out.astype(BF16), x2d, ln_g.reshape(1, D), ln_b.reshape(1, D), alpha, emit_bf16)
    return y.reshape(B, T, D), y_bf


def _odd_layer(x, x_bf, emit_bf16, mem_bf, w_in, conv_w, conv_b, w_q, w_k, w_v, w_if, b_if, mh_norm_g, skip,
               mem_w_k, mem_w_v, w_out, ln_g, ln_b, alpha):
    B, T, D = x.shape
    width = conv_w.shape[1]
    heads = b_if.shape[0] // 2
    mw = mem_w_k.shape[1]
    assert w_in.shape[1] == 2 * width + 2 * mw

    x2d = x.reshape(B * T, D)
    if x_bf is None:
        x_bf = x2d.astype(BF16)
    h_bf = _matmul(x_bf, w_in.astype(BF16), BF16, tm=2048, tn=512).reshape(B, T, -1)

    xc, q, k, kt, v, gates = _mlstm_pre_call(h_bf, conv_w, conv_b, w_q, w_k, w_v, w_if, b_if, width=width)
    gates_rows = jnp.transpose(gates[:, :, :2 * heads].reshape(B, T, 2, heads), (0, 3, 2, 1))
    mlstm_out = _mlstm_call(q, k, kt, v, gates, gates_rows, xc, h_bf,
                            mh_norm_g.reshape(1, width), skip.reshape(1, width), heads=heads, z_col=width)
    k_mem, v_mem = _mem_kv(mem_bf, mem_w_k, mem_w_v, B)
    mem_out = _memattn_call(h_bf, k_mem, v_mem, q_col=2 * width, g_col=2 * width + mw)

    acts = [a.reshape(B * T, -1) for a in (mlstm_out, mem_out)]
    y, y_bf = _outproj_ln(acts, w_out.astype(BF16), x2d, ln_g.reshape(1, D), ln_b.reshape(1, D), alpha, emit_bf16)
    return y.reshape(B, T, D), y_bf


def kernel(x, mem, hgrn_lb_logits, ev_w_in, ev_gla_w_a2, ev_gla_b_a, ev_gla_norm_g, ev_hgrn_norm_g, ev_mem_w_k, ev_mem_w_v, ev_w_out, ev_ln_g, ev_ln_b, od_w_in, od_conv_w, od_conv_b, od_w_q, od_w_k, od_w_v, od_w_if, od_b_if, od_mh_norm_g, od_skip, od_mem_w_k, od_mem_w_v, od_w_out, od_ln_g, od_ln_b):
    depth = ev_w_in.shape[0] + od_w_in.shape[0]
    alpha = (2 * depth) ** 0.25
    B, ml, D = mem.shape
    mem_bf = mem.reshape(B * ml, D).astype(BF16)
    lb_all = jnp.cumsum(jax.nn.softmax(hgrn_lb_logits.astype(F32), axis=0), axis=0)
    x_bf = None
    for layer in range(depth):
        i = layer // 2
        emit_bf16 = layer + 1 < depth
        if layer % 2 == 0:
            x, x_bf = _even_layer(x, x_bf, emit_bf16, mem_bf, lb_all[layer], ev_w_in, i, ev_gla_w_a2[i],
                                  ev_gla_b_a[i], ev_gla_norm_g[i], ev_hgrn_norm_g[i], ev_mem_w_k[i],
                                  ev_mem_w_v[i], ev_w_out[i], ev_ln_g[i], ev_ln_b[i], alpha)
        else:
            x, x_bf = _odd_layer(x, x_bf, emit_bf16, mem_bf, od_w_in[i], od_conv_w[i], od_conv_b[i], od_w_q[i],
                                 od_w_k[i], od_w_v[i], od_w_if[i], od_b_if[i], od_mh_norm_g[i], od_skip[i],
                                 od_mem_w_k[i], od_mem_w_v[i], od_w_out[i], od_ln_g[i], od_ln_b[i], alpha)
    return x
```

```python
import functools

import jax
import jax.numpy as jnp
import numpy as np
from jax import lax
from jax.experimental import pallas as pl
from jax.experimental.pallas import tpu as pltpu

F32 = jnp.float32
BF16 = jnp.bfloat16

GLA_HEADS = 4
GLA_TAU = 16.0
HGRN_EXPAND = 128
MEM_HEADS = 4

LIN_CHUNK = 128
MLSTM_CHUNK = 256

V7X_VMEM_LIMIT_BYTES = 56 * 1024 * 1024
V7X_MXU_WIDTH = 256
NEG_BIG = -1e30
LOG2_E = 1.4426950408889634


def _sigmoid(x):
    return 1.0 / (1.0 + jnp.exp(-x))


def _silu(x):
    return x * _sigmoid(x)


def _log_sigmoid(x):
    return jnp.minimum(x, 0.0) - jnp.log(1.0 + jnp.exp(-jnp.abs(x)))


def _dot(a, b):
    return jnp.dot(a, b, preferred_element_type=F32)


def _dot_nt(a, b):
    return lax.dot_general(a, b, (((1,), (1,)), ((), ())), preferred_element_type=F32)


def _dot_tn(a, b):
    return lax.dot_general(a, b, (((0,), (0,)), ((), ())), preferred_element_type=F32)


def _params(*sem):
    return pltpu.CompilerParams(dimension_semantics=sem, vmem_limit_bytes=V7X_VMEM_LIMIT_BYTES)


def _mm_kernel(a_ref, w_ref, o_ref):
    o_ref[...] = _dot(a_ref[...], w_ref[...]).astype(o_ref.dtype)


def _matmul(a, w, out_dtype, tm=1024, tn=1024, cols=None):
    m, k = a.shape
    cols = [(0, w.shape[1])] if cols is None else list(cols)
    tm = min(tm, m)
    while any(v % tn for rng in cols for v in rng):
        tn //= 2
    assert m % tm == 0 and tn % 128 == 0, (m, cols, tm, tn)
    n_out = sum(n for _, n in cols)

    def w_block(i, j):
        blk, done = None, 0
        for first, n in cols:
            here = first // tn + (j - done)
            blk = here if blk is None else jnp.where(j >= done, here, blk)
            done += n // tn
        return 0, blk

    return pl.pallas_call(
        _mm_kernel,
        grid=(m // tm, n_out // tn),
        in_specs=[pl.BlockSpec((tm, k), lambda i, j: (i, 0)),
                  pl.BlockSpec((k, tn), w_block)],
        out_specs=pl.BlockSpec((tm, tn), lambda i, j: (i, j)),
        out_shape=jax.ShapeDtypeStruct((m, n_out), out_dtype),
        compiler_params=_params("parallel", "arbitrary"),
        name="proj_matmul",
    )(a, w)


def _mm_cast_kernel(a_ref, w_ref, o_ref, abf_ref):
    a_bf = a_ref[...].astype(abf_ref.dtype)
    abf_ref[...] = a_bf
    o_ref[...] = _dot(a_bf, w_ref[...]).astype(o_ref.dtype)


def _matmul_cast(a, w, out_dtype, tm=512):
    m, k = a.shape
    n = w.shape[1]
    tm = min(tm, m)
    assert m % tm == 0 and n % 128 == 0
    return pl.pallas_call(
        _mm_cast_kernel,
        grid=(m // tm,),
        in_specs=[pl.BlockSpec((tm, k), lambda i: (i, 0)), pl.BlockSpec((k, n), lambda i: (0, 0))],
        out_specs=[pl.BlockSpec((tm, n), lambda i: (i, 0)), pl.BlockSpec((tm, k), lambda i: (i, 0))],
        out_shape=[jax.ShapeDtypeStruct((m, n), out_dtype), jax.ShapeDtypeStruct((m, k), BF16)],
        compiler_params=_params("parallel"),
        name="proj_matmul_cast",
    )(a, w)


def _window_cast_kernel(x_ref, o_ref):
    o_ref[...] = x_ref[0].T.astype(o_ref.dtype)


def _window_cast(w_stack, layer, col0, ncols, tr=1024, tkb=1024):
    wt = jnp.swapaxes(w_stack, 1, 2)
    _, n, k = wt.shape
    tkb = min(tkb, k)
    while ncols % tr:
        tr //= 2
    assert col0 % 8 == 0 and tr % 128 == 0 and k % tkb == 0 and col0 + ncols <= n, (col0, ncols, tr)
    return pl.pallas_call(
        _window_cast_kernel,
        grid=(ncols // tr, k // tkb),
        in_specs=[pl.BlockSpec((pl.Element(1), pl.Element(tr), pl.Element(tkb)),
                               lambda i, j: (layer, pl.multiple_of(col0 + i * tr, 8), pl.multiple_of(j * tkb, 128)))],
        out_specs=pl.BlockSpec((tkb, tr), lambda i, j: (j, i)),
        out_shape=jax.ShapeDtypeStruct((k, ncols), BF16),
        compiler_params=_params("parallel", "parallel"),
        name="weight_window_cast",
    )(wt)


def _lin_constants(L):
    i = np.arange(L)[:, None]
    t = np.arange(L)[None, :]
    sums = [t <= i]
    masks = [i == t]
    m = L // 2
    while m >= 1:
        r = (i // (2 * m)) * 2 * m + m - 1
        second = (i & m) != 0
        sums.append(np.where(second, (t > r) & (t <= i), (t > i) & (t <= r)))
        masks.append((i // (2 * m) == t // (2 * m)) & second & ((t & m) == 0))
        m //= 2
    sums.append(t > i)
    s = np.concatenate(sums, axis=0).astype(np.float32)
    return (jnp.asarray(np.concatenate([s, s], axis=1), BF16),
            jnp.asarray(np.stack(masks).astype(np.float32)))


def _block_rows(x, grp, d):
    if grp == 1:
        return x
    lane = lax.broadcasted_iota(jnp.int32, x.shape, 1)
    zero = jnp.zeros_like(x)
    return jnp.concatenate([jnp.where((lane >= r * d) & (lane < (r + 1) * d), x, zero) for r in range(grp)], axis=0)


def _lin_attn_chunk(q, k, v_bf, g, st_ref, sums_ref, mask_ref, nh):
    L, wk = q.shape
    dk = wk // nh
    dv = v_bf.shape[1] // nh
    grp = st_ref.shape[1] // dv
    n_lvl = mask_ref.shape[0] - 1
    g2 = g * LOG2_E
    hi = g2.astype(BF16)
    lo = (g2 - hi.astype(F32)).astype(BF16)
    p = jnp.exp2(_dot(sums_ref[...], jnp.concatenate([hi, lo], axis=0)))
    p_last = p[L - 1:L, :]
    qd = (q * p[0:L]).astype(BF16)
    kd = (k * p[(n_lvl + 1) * L:(n_lvl + 2) * L]).astype(BF16)
    q_bf = q.astype(BF16)
    k_bf = k.astype(BF16)
    rowi = lax.broadcasted_iota(jnp.int32, (L, 1), 0)
    xs = []
    for lvl in range(n_lvl):
        second = (rowi & (L >> (lvl + 1))) != 0
        xs.append((jnp.where(second, q, k) * p[(lvl + 1) * L:(lvl + 2) * L]).astype(BF16))
    def mask(l):
        m = mask_ref[l]
        return m if grp == 1 else jnp.concatenate([m] * grp, axis=1)

    outs = []
    for h in range(nh // grp):
        ck = slice(h * grp * dk, (h + 1) * grp * dk)
        cv = slice(h * grp * dv, (h + 1) * grp * dv)
        a = _dot_nt(q_bf[:, ck], _block_rows(k_bf[:, ck], grp, dk)) * mask(0)
        for lvl in range(n_lvl):
            x = xs[lvl][:, ck]
            a = a + _dot_nt(x, _block_rows(x, grp, dk)) * mask(lvl + 1)
        st = st_ref[h]
        v_blk = _block_rows(v_bf[:, cv], grp, dv)
        o = _dot_nt(qd[:, ck], st.astype(BF16)) + _dot(a.astype(BF16), v_blk)
        st_ref[h] = st * p_last[:, ck] + _dot_tn(v_blk, _block_rows(kd[:, ck], grp, dk))
        outs.extend(o[:, r * dv:(r + 1) * dv] for r in range(grp))
    return outs


def _rms_gate(o, norm_g, gate, eps=1e-6):
    ms = jnp.mean(o * o, axis=-1, keepdims=True)
    return o * lax.rsqrt(ms + eps) * norm_g * _silu(gate)


def _gla_kernel(q_ref, k_ref, v_ref, gg_ref, ga_ref, wa_ref, ba_ref, ng_ref, sums_ref, mask_ref, o_ref,
                st_ref, *, n_chunks, q_scale):
    @pl.when(pl.program_id(2) == 0)
    def _():
        st_ref[...] = jnp.zeros_like(st_ref)

    def chunk(c, carry):
        rows = pl.ds(pl.multiple_of(c * LIN_CHUNK, LIN_CHUNK), LIN_CHUNK)
        q = q_ref[rows, :].astype(F32) * q_scale
        k = k_ref[rows, :].astype(F32)
        a_pre = _dot(ga_ref[rows, :].astype(BF16), wa_ref[...]) + ba_ref[...]
        g = _log_sigmoid(a_pre) * (1.0 / GLA_TAU)
        (o,) = _lin_attn_chunk(q, k, v_ref[rows, :], g, st_ref, sums_ref, mask_ref, 1)
        o_ref[rows, :] = _rms_gate(o, ng_ref[...], gg_ref[rows, :].astype(F32)).astype(o_ref.dtype)
        return carry

    lax.fori_loop(0, n_chunks, chunk, 0, unroll=4)


def _hgrn_kernel(q_ref, f_ref, v_ref, hg_ref, lb_ref, ng_ref, sums_ref, mask_ref, o_ref,
                 st_ref, *, n_chunks, nh):
    dv = ng_ref.shape[1]

    @pl.when(pl.program_id(2) == 0)
    def _():
        st_ref[...] = jnp.zeros_like(st_ref)

    def chunk(c, carry):
        rows = pl.ds(pl.multiple_of(c * LIN_CHUNK, LIN_CHUNK), LIN_CHUNK)
        lb = lb_ref[...]
        f = lb + (1.0 - lb) * _sigmoid(f_ref[rows, :])
        q = _silu(q_ref[rows, :].astype(F32))
        outs = _lin_attn_chunk(q, 1.0 - f, v_ref[rows, :], jnp.log(f), st_ref, sums_ref, mask_ref, nh)
        gate = hg_ref[rows, :].astype(F32)
        for h, o in enumerate(outs):
            cv = slice(h * dv, (h + 1) * dv)
            o_ref[rows, cv] = _rms_gate(o, ng_ref[...], gate[:, cv]).astype(o_ref.dtype)
        return carry

    lax.fori_loop(0, n_chunks, chunk, 0, unroll=4)


def _col_spec(tb, width, col0):
    assert col0 % width == 0, (col0, width)
    base = col0 // width
    return pl.BlockSpec((None, tb, width), lambda b, h, t: (b, t, base + h))


def _const_specs(*arrays):
    return [pl.BlockSpec(a.shape, lambda b, h, t, nd=a.ndim: (0,) * nd) for a in arrays]


def _gla_call(h_bf, h_f, wa_pad, b_a, norm_g, *, offs, ga_col, dk, dv, tb):
    B, T, _ = h_bf.shape
    tb = min(tb, T)
    ga_w = wa_pad.shape[0]
    assert ga_col % ga_w == 0
    consts = _lin_constants(LIN_CHUNK)
    kern = functools.partial(_gla_kernel, n_chunks=tb // LIN_CHUNK, q_scale=dk ** -0.5)
    return pl.pallas_call(
        kern,
        grid=(B, GLA_HEADS, T // tb),
        in_specs=[_col_spec(tb, dk, offs["gq"]), _col_spec(tb, dk, offs["gk"]),
                  _col_spec(tb, dv, offs["gv"]), _col_spec(tb, dv, offs["gg"]),
                  pl.BlockSpec((None, tb, ga_w), lambda b, h, t: (b, t, ga_col // ga_w)),
                  pl.BlockSpec((ga_w, dk), lambda b, h, t: (0, h)),
                  pl.BlockSpec((1, dk), lambda b, h, t: (0, h)),
                  pl.BlockSpec((1, dv), lambda b, h, t: (0, 0))] + _const_specs(*consts),
        out_specs=pl.BlockSpec((None, tb, dv), lambda b, h, t: (b, t, h)),
        out_shape=jax.ShapeDtypeStruct((B, T, GLA_HEADS * dv), BF16),
        scratch_shapes=[pltpu.VMEM((1, dv, dk), F32)],
        compiler_params=_params("parallel", "parallel", "arbitrary"),
        name="gla_chunks",
    )(h_bf, h_bf, h_bf, h_bf, h_f, wa_pad, b_a, norm_g, *consts)


def _hgrn_call(h_bf, h_f, lb, norm_g, *, offs, heads, dk, dv, tb, nh=4):
    B, T, _ = h_bf.shape
    tb = min(tb, T)
    nh = min(nh, heads)
    assert heads % nh == 0
    grp = 2 if (nh % 2 == 0 and dk == dv and 2 * dk == V7X_MXU_WIDTH) else 1
    consts = _lin_constants(LIN_CHUNK)
    kern = functools.partial(_hgrn_kernel, n_chunks=tb // LIN_CHUNK, nh=nh)
    return pl.pallas_call(
        kern,
        grid=(B, heads // nh, T // tb),
        in_specs=[_col_spec(tb, nh * dk, offs["hq"]),
                  pl.BlockSpec((None, tb, nh * dk), lambda b, h, t: (b, t, h)),
                  _col_spec(tb, nh * dv, offs["hi"]), _col_spec(tb, nh * dv, offs["hg"]),
                  pl.BlockSpec((1, nh * dk), lambda b, h, t: (0, h)),
                  pl.BlockSpec((1, dv), lambda b, h, t: (0, 0))] + _const_specs(*consts),
        out_specs=pl.BlockSpec((None, tb, nh * dv), lambda b, h, t: (b, t, h)),
        out_shape=jax.ShapeDtypeStruct((B, T, heads * dv), BF16),
        scratch_shapes=[pltpu.VMEM((nh // grp, grp * dv, grp * dk), F32)],
        compiler_params=_params("parallel", "parallel", "arbitrary"),
        name="hgrn_chunks",
    )(h_bf, h_f, h_bf, h_bf, lb, norm_g, *consts)


def _memattn_kernel(q_ref, g_ref, k_ref, v_ref, o_ref, *, dh):
    scale = dh ** -0.5
    for h in range(MEM_HEADS):
        cs = slice(h * dh, (h + 1) * dh)
        s = _dot_nt(q_ref[:, cs], k_ref[:, cs]) * scale
        s = s - jnp.max(s, axis=-1, keepdims=True)
        p = jnp.exp(s)
        p = p / jnp.sum(p, axis=-1, keepdims=True)
        o = _dot(p.astype(BF16), v_ref[:, cs])
        o_ref[:, cs] = (o * _silu(g_ref[:, cs].astype(F32))).astype(o_ref.dtype)


def _memattn_call(h_bf, k_mem, v_mem, *, q_col, g_col, tm=512):
    B, T, _ = h_bf.shape
    _, ml, mw = k_mem.shape
    tm = min(tm, T)
    assert q_col % mw == 0 and g_col % mw == 0
    return pl.pallas_call(
        functools.partial(_memattn_kernel, dh=mw // MEM_HEADS),
        grid=(B, T // tm),
        in_specs=[pl.BlockSpec((None, tm, mw), lambda b, t: (b, t, q_col // mw)),
                  pl.BlockSpec((None, tm, mw), lambda b, t: (b, t, g_col // mw)),
                  pl.BlockSpec((None, ml, mw), lambda b, t: (b, 0, 0)),
                  pl.BlockSpec((None, ml, mw), lambda b, t: (b, 0, 0))],
        out_specs=pl.BlockSpec((None, tm, mw), lambda b, t: (b, t, 0)),
        out_shape=jax.ShapeDtypeStruct((B, T, mw), BF16),
        compiler_params=_params("parallel", "parallel"),
        name="mem_attention",
    )(h_bf, h_bf, k_mem, v_mem)


def _outproj_kernel(*refs, segs, nk, n_blocks, alpha, eps, emit_bf16):
    ns = len(segs)
    a_refs = refs[:ns]
    if emit_bf16:
        w_ref, x_hbm, g_ref, b_ref, o_hbm, obf_hbm, acc, x_sem, o_sem, bf_buf, bf_sem = refs[ns:]
    else:
        w_ref, x_hbm, g_ref, b_ref, o_hbm, acc, x_sem, o_sem = refs[ns:]
    i = pl.program_id(0)
    k = pl.program_id(1)
    _, tm, d = acc.shape
    rb, cb = min(128, tm), min(512, d)
    n_rb = tm // rb
    slot = i % 2
    o_ref = acc.at[slot]

    def rows_of(blk):
        return pl.ds(pl.multiple_of(blk * tm, tm), tm)

    def x_copy(blk, s):
        return pltpu.make_async_copy(x_hbm.at[rows_of(blk), :], acc.at[s], x_sem.at[s])

    def o_copy(blk, s):
        return pltpu.make_async_copy(acc.at[s], o_hbm.at[rows_of(blk), :], o_sem.at[s])

    def bf_copy(r, bslot):
        row0 = pl.multiple_of(i * tm + r * rb, rb)
        return pltpu.make_async_copy(bf_buf.at[bslot], obf_hbm.at[pl.ds(row0, rb), :], bf_sem.at[bslot])

    @pl.when((i == 0) & (k == 0))
    def _():
        x_copy(0, 0).start()

    @pl.when(k == 0)
    def _():
        x_copy(i, slot).wait()
        a = a_refs[0][...]
        for c in range(0, d, cb):
            o_ref[:, c:c + cb] = alpha * o_ref[:, c:c + cb] + _dot(a, w_ref[:, c:c + cb])

    for a_ref, (start, n) in zip(a_refs, segs):
        @pl.when((k >= max(start, 1)) & (k < start + n))
        def _(a_ref=a_ref):
            a = a_ref[...]
            for c in range(0, d, cb):
                o_ref[:, c:c + cb] += _dot(a, w_ref[:, c:c + cb])

    @pl.when(k == nk - 1)
    def _():
        @pl.when(i >= 1)
        def _():
            o_copy(i - 1, 1 - slot).wait()

        @pl.when(i + 1 < n_blocks)
        def _():
            x_copy(i + 1, 1 - slot).start()

        def norm(r, carry):
            rows = pl.ds(pl.multiple_of(r * rb, rb), rb)
            y = o_ref[rows, :]
            mu = jnp.mean(y, axis=-1, keepdims=True)
            yc = y - mu
            var = jnp.mean(yc * yc, axis=-1, keepdims=True)
            out = yc * lax.rsqrt(var + eps) * g_ref[...] + b_ref[...]
            o_ref[rows, :] = out
            if emit_bf16:
                n_slot = bf_buf.shape[0]
                bslot = r % n_slot

                @pl.when(r >= n_slot)
                def _():
                    bf_copy(r - n_slot, bslot).wait()

                bf_buf[bslot] = out.astype(bf_buf.dtype)
                bf_copy(r, bslot).start()
            return carry

        lax.fori_loop(0, n_rb, norm, 0)
        o_copy(i, slot).start()
        if emit_bf16:
            for r in range(max(n_rb - bf_buf.shape[0], 0), n_rb):
                bf_copy(r, r % bf_buf.shape[0]).wait()

        @pl.when(i == n_blocks - 1)
        def _():
            o_copy(i, slot).wait()


def _outproj_ln(acts, w_bf, x2d, ln_g, ln_b, alpha, emit_bf16, tm=1024, tk=512):
    m, d = x2d.shape
    tm = min(tm, m)
    segs, start = [], 0
    for a in acts:
        assert a.shape[0] == m and a.shape[1] % tk == 0
        segs.append((start, a.shape[1] // tk))
        start += a.shape[1] // tk
    nk = start
    assert nk * tk == w_bf.shape[0]

    def a_spec(s0, n):
        return pl.BlockSpec((tm, tk), lambda i, k: (i, jnp.clip(k - s0, 0, n - 1)))

    kern = functools.partial(_outproj_kernel, segs=tuple(segs), nk=nk, n_blocks=m // tm, alpha=alpha, eps=1e-5,
                             emit_bf16=emit_bf16)
    rb = min(128, tm)
    out_specs = [pl.BlockSpec(memory_space=pl.ANY)]
    out_shape = [jax.ShapeDtypeStruct((m, d), F32)]
    scratch = [pltpu.VMEM((2, tm, d), F32), pltpu.SemaphoreType.DMA((2,)), pltpu.SemaphoreType.DMA((2,))]
    if emit_bf16:
        out_specs.append(pl.BlockSpec(memory_space=pl.ANY))
        out_shape.append(jax.ShapeDtypeStruct((m, d), BF16))
        n_slot = 4
        scratch += [pltpu.VMEM((n_slot, rb, d), BF16), pltpu.SemaphoreType.DMA((n_slot,))]
    outs = pl.pallas_call(
        kern,
        grid=(m // tm, nk),
        in_specs=[a_spec(s0, n) for s0, n in segs] + [
            pl.BlockSpec((tk, d), lambda i, k: (k, 0)),
            pl.BlockSpec(memory_space=pl.ANY),
            pl.BlockSpec((1, d), lambda i, k: (0, 0)),
            pl.BlockSpec((1, d), lambda i, k: (0, 0))],
        out_specs=out_specs,
        out_shape=out_shape,
        scratch_shapes=scratch,
        compiler_params=_params("arbitrary", "arbitrary"),
        name="outproj_layernorm",
    )(*acts, w_bf, x2d, ln_g, ln_b)
    return (outs[0], outs[1]) if emit_bf16 else (outs[0], None)


def _mlstm_pre_kernel(x_ref, halo_ref, cw_ref, cb_ref, wq_ref, wk_ref, wkt_ref, wv_ref,
                      iq_ref, ik_ref, iv_ref, bif_ref,
                      xc_ref, q_ref, k_ref, kt_ref, v_ref, g_ref, *, n_sub, bw, n_taps):
    t = pl.program_id(1)
    c = pl.program_id(2)
    tm = x_ref.shape[0]

    @pl.when(c == 0)
    def _():
        g_ref[...] = jnp.broadcast_to(bif_ref[...], g_ref.shape)

    gates = jnp.zeros(g_ref.shape, F32)
    for j in range(n_sub):
        cs = slice(j * bw, (j + 1) * bw)
        x_bf = x_ref[:, cs]
        x32 = x_bf.astype(F32)
        halo = jnp.where(t > 0, halo_ref[:, cs].astype(F32), 0.0)
        ext = jnp.concatenate([halo, x32], axis=0)
        acc = jnp.broadcast_to(cb_ref[:, cs], (tm, bw))
        for tap in reversed(range(n_taps)):
            sh = 8 - (n_taps - 1) + tap
            acc = acc + ext[sh:sh + tm, :] * cw_ref[tap:tap + 1, cs]
        xc_bf = _silu(acc).astype(BF16)
        xc_ref[:, cs] = xc_bf
        q_bf = _dot(xc_bf, wq_ref[j]).astype(BF16)
        k_bf = _dot(xc_bf, wk_ref[j]).astype(BF16)
        v_bf = _dot(x_bf, wv_ref[j]).astype(BF16)
        q_ref[:, cs] = q_bf
        k_ref[:, cs] = k_bf
        v_ref[:, cs] = v_bf
        kt_ref[cs, :] = _dot_nt(wkt_ref[j], xc_bf).astype(BF16)
        gates = gates + _dot(q_bf, iq_ref[cs, :]) + _dot(k_bf, ik_ref[cs, :]) + _dot(v_bf, iv_ref[cs, :])
    g_ref[...] += gates


def _block_diag(w, bw):
    nb, bi, bo = w.shape
    assert bi == bo and bw % bi == 0 and (nb * bi) % bw == 0
    per = bw // bi
    w4 = w.reshape(nb // per, per, bi, bo)
    eye = jnp.eye(per, dtype=w.dtype)
    return jnp.einsum("cnio,nm->cnimo", w4, eye).reshape(nb // per, bw, bw)


def _mlstm_pre_call(h_bf, conv_w, conv_b, w_q, w_k, w_v, w_if, b_if, *, width, tm=512, tc=1024, bw=256):
    B, T, _ = h_bf.shape
    tm, tc = min(tm, T), min(tc, width)
    n_taps = conv_w.shape[0]
    assert tm % 8 == 0 and n_taps <= 8 and width % tc == 0 and tc % bw == 0
    n_sub = tc // bw
    wq = _block_diag(w_q, bw).astype(BF16)
    wk = _block_diag(w_k, bw)
    wkt = jnp.swapaxes(wk, 1, 2).astype(BF16)
    wk = wk.astype(BF16)
    wv = _block_diag(w_v, bw).astype(BF16)
    n_gate = w_if.shape[1]
    wif = jnp.pad(w_if, ((0, 0), (0, 128 - n_gate))).astype(BF16)
    bif = jnp.pad(b_if.astype(F32), (0, 128 - n_gate)).reshape(1, 128)
    hb = tm // 8
    tile_spec = pl.BlockSpec((n_sub, bw, bw), lambda b, t, c: (c, 0, 0))
    act_spec = pl.BlockSpec((None, tm, tc), lambda b, t, c: (b, t, c))
    kern = functools.partial(_mlstm_pre_kernel, n_sub=n_sub, bw=bw, n_taps=n_taps)
    shp = jax.ShapeDtypeStruct((B, T, width), BF16)
    return pl.pallas_call(
        kern,
        grid=(B, T // tm, width // tc),
        in_specs=[act_spec,
                  pl.BlockSpec((None, 8, tc), lambda b, t, c: (b, jnp.maximum(t * hb - 1, 0), c)),
                  pl.BlockSpec((n_taps, tc), lambda b, t, c: (0, c)),
                  pl.BlockSpec((1, tc), lambda b, t, c: (0, c)),
                  tile_spec, tile_spec, tile_spec, tile_spec,
                  pl.BlockSpec((tc, 128), lambda b, t, c: (c, 0)),
                  pl.BlockSpec((tc, 128), lambda b, t, c: (width // tc + c, 0)),
                  pl.BlockSpec((tc, 128), lambda b, t, c: (2 * (width // tc) + c, 0)),
                  pl.BlockSpec((1, 128), lambda b, t, c: (0, 0))],
        out_specs=[act_spec, act_spec, act_spec,
                   pl.BlockSpec((None, tc, tm), lambda b, t, c: (b, c, t)),
                   act_spec,
                   pl.BlockSpec((None, tm, 128), lambda b, t, c: (b, t, 0))],
        out_shape=[shp, shp, shp, jax.ShapeDtypeStruct((B, width, T), BF16), shp,
                   jax.ShapeDtypeStruct((B, T, 128), F32)],
        compiler_params=_params("parallel", "parallel", "arbitrary"),
        name="mlstm_frontend",
    )(h_bf, h_bf, conv_w, conv_b.reshape(1, width), wq, wk, wkt, wv, wif, wif, wif, bif)


def _mlstm_kernel(q_ref, k_ref, kt_ref, v_ref, gc_ref, gr_ref, xc_ref, z_ref, ng_ref, sk_ref, o_ref,
                  c_ref, hh_ref, n_ref, m_ref, *, n_chunks, heads, k_scale, dvb):
    h = pl.program_id(1)
    L = MLSTM_CHUNK
    dk, dv = c_ref.shape

    @pl.when(pl.program_id(2) == 0)
    def _():
        c_ref[...] = jnp.zeros_like(c_ref)
        n_ref[...] = jnp.zeros_like(n_ref)
        m_ref[...] = jnp.zeros_like(m_ref)

    row = lax.broadcasted_iota(jnp.int32, (L, L), 0)
    col = lax.broadcasted_iota(jnp.int32, (L, L), 1)
    causal = row >= col
    glane = lax.broadcasted_iota(jnp.int32, (L, gc_ref.shape[1]), 1)

    for c in range(n_chunks):
        rs = slice(c * L, (c + 1) * L)
        qc = q_ref[rs, :]
        kc = k_ref[rs, :]
        vc = v_ref[rs, :]
        gcol = gc_ref[rs, :]
        i_col = jnp.sum(jnp.where(glane == h, gcol, 0.0), axis=1, keepdims=True)
        f_col = jnp.sum(jnp.where(glane == heads + h, gcol, 0.0), axis=1, keepdims=True)
        i_row = gr_ref[0:1, rs]
        f_row = gr_ref[1:2, rs]
        lf_col = _log_sigmoid(f_col)
        lf_row = _log_sigmoid(f_row)
        b_col = jnp.sum(jnp.where(causal, lf_row, 0.0), axis=1, keepdims=True)
        b_row = jnp.sum(jnp.where(row <= col, lf_col, 0.0), axis=0, keepdims=True)
        b_last = b_col[L - 1:L, :]
        m_prev = m_ref[0:1, 0:1]

        d = jnp.where(causal, b_col - b_row + i_row, NEG_BIG)
        inter_log = b_col + m_prev
        m_i = jnp.maximum(inter_log, jnp.max(d, axis=1, keepdims=True))
        w_intra = jnp.exp(d - m_i)
        w_inter = jnp.exp(inter_log - m_i) * k_scale
        scores = _dot_nt(qc, kc) * k_scale * w_intra
        n_row = n_ref[...]
        den = (jnp.sum(scores, axis=1, keepdims=True)
               + w_inter * jnp.sum(qc.astype(F32) * n_row, axis=1, keepdims=True))
        inv = 1.0 / jnp.maximum(jnp.abs(den), jnp.exp(-m_i))
        scores_bf = scores.astype(BF16)

        log_wj_row = b_last - b_row + i_row
        log_wj_col = b_last - b_col + i_col
        m_new = jnp.maximum(b_last + m_prev, jnp.max(log_wj_row, axis=1, keepdims=True))
        wj_row = jnp.exp(log_wj_row - m_new)
        wj_col = jnp.exp(log_wj_col - m_new)
        dec = jnp.exp(b_last + m_prev - m_new)
        ktw = (kt_ref[:, rs].astype(F32) * wj_row).astype(BF16)

        for j in range(dv // dvb):
            cs = slice(j * dvb, (j + 1) * dvb)
            c_old = c_ref[:, cs]
            num = _dot(scores_bf, vc[:, cs]) + w_inter * _dot(qc, c_old.astype(BF16))
            hh_ref[:, cs] = num * inv
            c_ref[:, cs] = c_old * dec + _dot(ktw, vc[:, cs])
        n_ref[...] = n_row * dec + jnp.sum(kc.astype(F32) * wj_col, axis=0, keepdims=True)
        m_ref[...] = jnp.broadcast_to(m_new, m_ref.shape)

        hh = hh_ref[...]
        mu = jnp.mean(hh, axis=-1, keepdims=True)
        hc = hh - mu
        var = jnp.mean(hc * hc, axis=-1, keepdims=True)
        hn = hc * lax.rsqrt(var + 1e-6) * ng_ref[...]
        out = (hn + sk_ref[...] * xc_ref[rs, :].astype(F32)) * _silu(z_ref[rs, :].astype(F32))
        o_ref[rs, :] = out.astype(o_ref.dtype)


def _mlstm_call(q, k, kt, v, gates, gates_rows, xc, h_bf, norm_g, skip, *, heads, z_col, tb=MLSTM_CHUNK):
    B, T, width = q.shape
    dh = width // heads
    tb = min(tb, T)
    assert tb % MLSTM_CHUNK == 0 and z_col % dh == 0
    dvb = min(512, dh)
    hs = pl.BlockSpec((None, tb, dh), lambda b, h, t: (b, t, h))
    kern = functools.partial(_mlstm_kernel, n_chunks=tb // MLSTM_CHUNK, heads=heads,
                             k_scale=dh ** -0.5, dvb=dvb)
    return pl.pallas_call(
        kern,
        grid=(B, heads, T // tb),
        in_specs=[hs, hs,
                  pl.BlockSpec((None, dh, tb), lambda b, h, t: (b, h, t)),
                  hs,
                  pl.BlockSpec((None, tb, gates.shape[2]), lambda b, h, t: (b, t, 0)),
                  pl.BlockSpec((None, None, 2, tb), lambda b, h, t: (b, h, 0, t)),
                  hs,
                  pl.BlockSpec((None, tb, dh), lambda b, h, t: (b, t, z_col // dh + h)),
                  pl.BlockSpec((1, dh), lambda b, h, t: (0, h)),
                  pl.BlockSpec((1, dh), lambda b, h, t: (0, h))],
        out_specs=hs,
        out_shape=jax.ShapeDtypeStruct((B, T, width), BF16),
        scratch_shapes=[pltpu.VMEM((dh, dh), F32), pltpu.VMEM((MLSTM_CHUNK, dh), F32),
                        pltpu.VMEM((1, dh), F32), pltpu.VMEM((8, 128), F32)],
        compiler_params=_params("parallel", "parallel", "arbitrary"),
        name="mlstm_chunks",
    )(q, k, kt, v, gates, gates_rows, xc, h_bf, norm_g, skip)


def _mem_kv(mem_bf, w_k, w_v, B):
    mw = w_k.shape[1]
    k = _matmul(mem_bf, w_k.astype(BF16), BF16).reshape(B, -1, mw)
    v = _matmul(mem_bf, w_v.astype(BF16), BF16).reshape(B, -1, mw)
    return k, v


def _even_layer(x, x_bf, emit_bf16, mem_bf, lb, w_in_stack, li, gla_w_a2, gla_b_a, gla_norm_g, hgrn_norm_g,
                mem_w_k, mem_w_v, w_out, ln_g, ln_b, alpha):
    B, T, D = x.shape
    lowrank, kw = gla_w_a2.shape
    gdv = gla_norm_g.shape[0]
    gw = GLA_HEADS * gdv
    gdk = kw // GLA_HEADS
    fw = lb.shape[0]
    hheads = fw // HGRN_EXPAND
    hdv = hgrn_norm_g.shape[0]
    hw = hheads * hdv
    mw = mem_w_k.shape[1]
    sizes = (kw, kw, gw, gw, lowrank, fw, fw, hw, hw, mw, mw)
    assert sum(sizes) == w_in_stack.shape[2]
    bounds = [0]
    for s in sizes:
        bounds.append(bounds[-1] + s)
    ga_w = 256
    w_a = _window_cast(w_in_stack, li, 0, bounds[4])
    w_b = _window_cast(w_in_stack, li, bounds[5], bounds[-1] - bounds[5])
    w_ga = jnp.pad(w_in_stack[li, :, bounds[4]:bounds[5]], ((0, 0), (0, ga_w - lowrank))).astype(BF16)
    offs_a = {"gq": 0, "gk": kw, "gv": 2 * kw, "gg": 2 * kw + gw}
    offs_b = {"hq": 0, "hi": fw, "hg": fw + hw, "mq": fw + 2 * hw, "mg": fw + 2 * hw + mw}

    x2d = x.reshape(B * T, D)
    if x_bf is None:
        h_ga, x_bf = _matmul_cast(x2d, w_ga, F32)
    else:
        h_ga = _matmul(x_bf, w_ga, F32)
    h_a = _matmul(x_bf, w_a, BF16)
    h_b = _matmul(x_bf, w_b, BF16, cols=[(0, fw), (2 * fw, bounds[-1] - bounds[5] - 2 * fw)])
    h_hf = _matmul(x_bf, w_b, F32, cols=[(fw, fw)])
    h_a, h_b, h_hf, h_ga = (h.reshape(B, T, -1) for h in (h_a, h_b, h_hf, h_ga))

    wa_pad = jnp.pad(gla_w_a2, ((0, ga_w - lowrank), (0, 0))).astype(BF16)
    gla_out = _gla_call(h_a, h_ga, wa_pad, gla_b_a.reshape(1, kw), gla_norm_g.reshape(1, gdv),
                        offs=offs_a, ga_col=0, dk=gdk, dv=gdv, tb=512)
    hgrn_out = _hgrn_call(h_b, h_hf, lb.reshape(1, fw), hgrn_norm_g.reshape(1, hdv),
                          offs=offs_b, heads=hheads, dk=HGRN_EXPAND, dv=hdv, tb=512)
    k_mem, v_mem = _mem_kv(mem_bf, mem_w_k, mem_w_v, B)
    mem_out = _memattn_call(h_b, k_mem, v_mem, q_col=offs_b["mq"], g_col=offs_b["mg"])

    acts = [a.reshape(B * T, -1) for a in (gla_out, hgrn_out, mem_out)]
    y, y_bf = _outproj_ln(acts, w_out.astype(BF16), x2d, ln_g.reshape(1, D), ln_b.reshape(1, D), alpha, emit_bf16)
    return y.reshape(B, T, D), y_bf


def _odd_layer(x, x_bf, emit_bf16, mem_bf, w_in, conv_w, conv_b, w_q, w_k, w_v, w_if, b_if, mh_norm_g, skip,
               mem_w_k, mem_w_v, w_out, ln_g, ln_b, alpha):
    B, T, D = x.shape
    width = conv_w.shape[1]
    heads = b_if.shape[0] // 2
    mw = mem_w_k.shape[1]
    assert w_in.shape[1] == 2 * width + 2 * mw

    x2d = x.reshape(B * T, D)
    if x_bf is None:
        x_bf = x2d.astype(BF16)
    h_bf = _matmul(x_bf, w_in.astype(BF16), BF16).reshape(B, T, -1)

    xc, q, k, kt, v, gates = _mlstm_pre_call(h_bf, conv_w, conv_b, w_q, w_k, w_v, w_if, b_if, width=width)
    gates_rows = jnp.transpose(gates[:, :, :2 * heads].reshape(B, T, 2, heads), (0, 3, 2, 1))
    mlstm_out = _mlstm_call(q, k, kt, v, gates, gates_rows, xc, h_bf,
                            mh_norm_g.reshape(1, width), skip.reshape(1, width), heads=heads, z_col=width)
    k_mem, v_mem = _mem_kv(mem_bf, mem_w_k, mem_w_v, B)
    mem_out = _memattn_call(h_bf, k_mem, v_mem, q_col=2 * width, g_col=2 * width + mw)

    acts = [a.reshape(B * T, -1) for a in (mlstm_out, mem_out)]
    y, y_bf = _outproj_ln(acts, w_out.astype(BF16), x2d, ln_g.reshape(1, D), ln_b.reshape(1, D), alpha, emit_bf16)
    return y.reshape(B, T, D), y_bf


def kernel(x, mem, hgrn_lb_logits, ev_w_in, ev_gla_w_a2, ev_gla_b_a, ev_gla_norm_g, ev_hgrn_norm_g, ev_mem_w_k, ev_mem_w_v, ev_w_out, ev_ln_g, ev_ln_b, od_w_in, od_conv_w, od_conv_b, od_w_q, od_w_k, od_w_v, od_w_if, od_b_if, od_mh_norm_g, od_skip, od_mem_w_k, od_mem_w_v, od_w_out, od_ln_g, od_ln_b):
    depth = ev_w_in.shape[0] + od_w_in.shape[0]
    alpha = (2 * depth) ** 0.25
    B, ml, D = mem.shape
    mem_bf = mem.reshape(B * ml, D).astype(BF16)
    lb_all = jnp.cumsum(jax.nn.softmax(hgrn_lb_logits.astype(F32), axis=0), axis=0)
    x_bf = None
    for layer in range(depth):
        i = layer // 2
        emit_bf16 = layer + 1 < depth
        if layer % 2 == 0:
            x, x_bf = _even_layer(x, x_bf, emit_bf16, mem_bf, lb_all[layer], ev_w_in, i, ev_gla_w_a2[i],
                                  ev_gla_b_a[i], ev_gla_norm_g[i], ev_hgrn_norm_g[i], ev_mem_w_k[i],
                                  ev_mem_w_v[i], ev_w_out[i], ev_ln_g[i], ev_ln_b[i], alpha)
        else:
            x, x_bf = _odd_layer(x, x_bf, emit_bf16, mem_bf, od_w_in[i], od_conv_w[i], od_conv_b[i], od_w_q[i],
                                 od_w_k[i], od_w_v[i], od_w_if[i], od_b_if[i], od_mh_norm_g[i], od_skip[i],
                                 od_mem_w_k[i], od_mem_w_v[i], od_w_out[i], od_ln_g[i], od_ln_b[i], alpha)
    return x
```

```python
import functools

import jax
import jax.numpy as jnp
import numpy as np
from jax import lax
from jax.experimental import pallas as pl
from jax.experimental.pallas import tpu as pltpu

F32 = jnp.float32
BF16 = jnp.bfloat16

GLA_HEADS = 4
GLA_TAU = 16.0
HGRN_EXPAND = 128
MEM_HEADS = 4

LIN_CHUNK = 128
MLSTM_CHUNK = 256

V7X_VMEM_LIMIT_BYTES = 56 * 1024 * 1024
V7X_MXU_WIDTH = 256
NEG_BIG = -1e30
LOG2_E = 1.4426950408889634


def _sigmoid(x):
    return 1.0 / (1.0 + jnp.exp(-x))


def _silu(x):
    return x * _sigmoid(x)


def _log_sigmoid(x):
    return jnp.minimum(x, 0.0) - jnp.log(1.0 + jnp.exp(-jnp.abs(x)))


def _dot(a, b):
    return jnp.dot(a, b, preferred_element_type=F32)


def _dot_nt(a, b):
    return lax.dot_general(a, b, (((1,), (1,)), ((), ())), preferred_element_type=F32)


def _dot_tn(a, b):
    return lax.dot_general(a, b, (((0,), (0,)), ((), ())), preferred_element_type=F32)


def _params(*sem):
    return pltpu.CompilerParams(dimension_semantics=sem, vmem_limit_bytes=V7X_VMEM_LIMIT_BYTES)


def _mm_kernel(a_ref, w_ref, o_ref):
    o_ref[...] = _dot(a_ref[...], w_ref[...]).astype(o_ref.dtype)


def _matmul(a, w, out_dtype, tm=1024, tn=1024, cols=None):
    m, k = a.shape
    cols = [(0, w.shape[1])] if cols is None else list(cols)
    tm = min(tm, m)
    while any(v % tn for rng in cols for v in rng):
        tn //= 2
    assert m % tm == 0 and tn % 128 == 0, (m, cols, tm, tn)
    n_out = sum(n for _, n in cols)

    def w_block(i, j):
        blk, done = None, 0
        for first, n in cols:
            here = first // tn + (j - done)
            blk = here if blk is None else jnp.where(j >= done, here, blk)
            done += n // tn
        return 0, blk

    return pl.pallas_call(
        _mm_kernel,
        grid=(m // tm, n_out // tn),
        in_specs=[pl.BlockSpec((tm, k), lambda i, j: (i, 0)),
                  pl.BlockSpec((k, tn), w_block)],
        out_specs=pl.BlockSpec((tm, tn), lambda i, j: (i, j)),
        out_shape=jax.ShapeDtypeStruct((m, n_out), out_dtype),
        compiler_params=_params("parallel", "arbitrary"),
        name="proj_matmul",
    )(a, w)


def _mm_cast_kernel(a_ref, w_ref, o_ref, abf_ref):
    a_bf = a_ref[...].astype(abf_ref.dtype)
    abf_ref[...] = a_bf
    o_ref[...] = _dot(a_bf, w_ref[...]).astype(o_ref.dtype)


def _matmul_cast(a, w, out_dtype, tm=512):
    m, k = a.shape
    n = w.shape[1]
    tm = min(tm, m)
    assert m % tm == 0 and n % 128 == 0
    return pl.pallas_call(
        _mm_cast_kernel,
        grid=(m // tm,),
        in_specs=[pl.BlockSpec((tm, k), lambda i: (i, 0)), pl.BlockSpec((k, n), lambda i: (0, 0))],
        out_specs=[pl.BlockSpec((tm, n), lambda i: (i, 0)), pl.BlockSpec((tm, k), lambda i: (i, 0))],
        out_shape=[jax.ShapeDtypeStruct((m, n), out_dtype), jax.ShapeDtypeStruct((m, k), BF16)],
        compiler_params=_params("parallel"),
        name="proj_matmul_cast",
    )(a, w)


def _window_cast_kernel(x_ref, o_ref):
    o_ref[...] = x_ref[0].T.astype(o_ref.dtype)


def _window_cast(w_stack, layer, col0, ncols, tr=1024, tkb=1024):
    wt = jnp.swapaxes(w_stack, 1, 2)
    _, n, k = wt.shape
    tkb = min(tkb, k)
    while ncols % tr:
        tr //= 2
    assert col0 % 8 == 0 and tr % 128 == 0 and k % tkb == 0 and col0 + ncols <= n, (col0, ncols, tr)
    return pl.pallas_call(
        _window_cast_kernel,
        grid=(ncols // tr, k // tkb),
        in_specs=[pl.BlockSpec((pl.Element(1), pl.Element(tr), pl.Element(tkb)),
                               lambda i, j: (layer, pl.multiple_of(col0 + i * tr, 8), pl.multiple_of(j * tkb, 128)))],
        out_specs=pl.BlockSpec((tkb, tr), lambda i, j: (j, i)),
        out_shape=jax.ShapeDtypeStruct((k, ncols), BF16),
        compiler_params=_params("parallel", "parallel"),
        name="weight_window_cast",
    )(wt)


def _lin_constants(L):
    i = np.arange(L)[:, None]
    t = np.arange(L)[None, :]
    sums = [t <= i]
    masks = [i == t]
    m = L // 2
    while m >= 1:
        r = (i // (2 * m)) * 2 * m + m - 1
        second = (i & m) != 0
        sums.append(np.where(second, (t > r) & (t <= i), (t > i) & (t <= r)))
        masks.append((i // (2 * m) == t // (2 * m)) & second & ((t & m) == 0))
        m //= 2
    sums.append(t > i)
    s = np.concatenate(sums, axis=0).astype(np.float32)
    return (jnp.asarray(np.concatenate([s, s], axis=1), BF16),
            jnp.asarray(np.stack(masks).astype(np.float32)))


def _block_rows(x, grp, d):
    if grp == 1:
        return x
    lane = lax.broadcasted_iota(jnp.int32, x.shape, 1)
    zero = jnp.zeros_like(x)
    return jnp.concatenate([jnp.where((lane >= r * d) & (lane < (r + 1) * d), x, zero) for r in range(grp)], axis=0)


def _lin_attn_chunk(q, k, v_bf, g, st_ref, sums_ref, mask_ref, nh):
    L, wk = q.shape
    dk = wk // nh
    dv = v_bf.shape[1] // nh
    grp = st_ref.shape[1] // dv
    n_lvl = mask_ref.shape[0] - 1
    g2 = g * LOG2_E
    hi = g2.astype(BF16)
    lo = (g2 - hi.astype(F32)).astype(BF16)
    p = jnp.exp2(_dot(sums_ref[...], jnp.concatenate([hi, lo], axis=0)))
    p_last = p[L - 1:L, :]
    qd = (q * p[0:L]).astype(BF16)
    kd = (k * p[(n_lvl + 1) * L:(n_lvl + 2) * L]).astype(BF16)
    q_bf = q.astype(BF16)
    k_bf = k.astype(BF16)
    rowi = lax.broadcasted_iota(jnp.int32, (L, 1), 0)
    xs = []
    for lvl in range(n_lvl):
        second = (rowi & (L >> (lvl + 1))) != 0
        xs.append((jnp.where(second, q, k) * p[(lvl + 1) * L:(lvl + 2) * L]).astype(BF16))
    def mask(l):
        m = mask_ref[l]
        return m if grp == 1 else jnp.concatenate([m] * grp, axis=1)

    outs = []
    for h in range(nh // grp):
        ck = slice(h * grp * dk, (h + 1) * grp * dk)
        cv = slice(h * grp * dv, (h + 1) * grp * dv)
        a = _dot_nt(q_bf[:, ck], _block_rows(k_bf[:, ck], grp, dk)) * mask(0)
        for lvl in range(n_lvl):
            x = xs[lvl][:, ck]
            a = a + _dot_nt(x, _block_rows(x, grp, dk)) * mask(lvl + 1)
        st = st_ref[h]
        v_blk = _block_rows(v_bf[:, cv], grp, dv)
        o = _dot_nt(qd[:, ck], st.astype(BF16)) + _dot(a.astype(BF16), v_blk)
        st_ref[h] = st * p_last[:, ck] + _dot_tn(v_blk, _block_rows(kd[:, ck], grp, dk))
        outs.extend(o[:, r * dv:(r + 1) * dv] for r in range(grp))
    return outs


def _rms_gate(o, norm_g, gate, eps=1e-6):
    ms = jnp.mean(o * o, axis=-1, keepdims=True)
    return o * lax.rsqrt(ms + eps) * norm_g * _silu(gate)


def _gla_kernel(q_ref, k_ref, v_ref, gg_ref, ga_ref, wa_ref, ba_ref, ng_ref, sums_ref, mask_ref, o_ref,
                st_ref, *, n_chunks, q_scale):
    @pl.when(pl.program_id(2) == 0)
    def _():
        st_ref[...] = jnp.zeros_like(st_ref)

    def chunk(c, carry):
        rows = pl.ds(pl.multiple_of(c * LIN_CHUNK, LIN_CHUNK), LIN_CHUNK)
        q = q_ref[rows, :].astype(F32) * q_scale
        k = k_ref[rows, :].astype(F32)
        a_pre = _dot(ga_ref[rows, :].astype(BF16), wa_ref[...]) + ba_ref[...]
        g = _log_sigmoid(a_pre) * (1.0 / GLA_TAU)
        (o,) = _lin_attn_chunk(q, k, v_ref[rows, :], g, st_ref, sums_ref, mask_ref, 1)
        o_ref[rows, :] = _rms_gate(o, ng_ref[...], gg_ref[rows, :].astype(F32)).astype(o_ref.dtype)
        return carry

    lax.fori_loop(0, n_chunks, chunk, 0, unroll=4)


def _hgrn_kernel(q_ref, f_ref, v_ref, hg_ref, lb_ref, ng_ref, sums_ref, mask_ref, o_ref,
                 st_ref, *, n_chunks, nh):
    dv = ng_ref.shape[1]

    @pl.when(pl.program_id(2) == 0)
    def _():
        st_ref[...] = jnp.zeros_like(st_ref)

    def chunk(c, carry):
        rows = pl.ds(pl.multiple_of(c * LIN_CHUNK, LIN_CHUNK), LIN_CHUNK)
        lb = lb_ref[...]
        f = lb + (1.0 - lb) * _sigmoid(f_ref[rows, :])
        q = _silu(q_ref[rows, :].astype(F32))
        outs = _lin_attn_chunk(q, 1.0 - f, v_ref[rows, :], jnp.log(f), st_ref, sums_ref, mask_ref, nh)
        gate = hg_ref[rows, :].astype(F32)
        for h, o in enumerate(outs):
            cv = slice(h * dv, (h + 1) * dv)
            o_ref[rows, cv] = _rms_gate(o, ng_ref[...], gate[:, cv]).astype(o_ref.dtype)
        return carry

    lax.fori_loop(0, n_chunks, chunk, 0, unroll=4)


def _col_spec(tb, width, col0):
    assert col0 % width == 0, (col0, width)
    base = col0 // width
    return pl.BlockSpec((None, tb, width), lambda b, h, t: (b, t, base + h))


def _const_specs(*arrays):
    return [pl.BlockSpec(a.shape, lambda b, h, t, nd=a.ndim: (0,) * nd) for a in arrays]


def _gla_call(h_bf, h_f, wa_pad, b_a, norm_g, *, offs, ga_col, dk, dv, tb):
    B, T, _ = h_bf.shape
    tb = min(tb, T)
    ga_w = wa_pad.shape[0]
    assert ga_col % ga_w == 0
    consts = _lin_constants(LIN_CHUNK)
    kern = functools.partial(_gla_kernel, n_chunks=tb // LIN_CHUNK, q_scale=dk ** -0.5)
    return pl.pallas_call(
        kern,
        grid=(B, GLA_HEADS, T // tb),
        in_specs=[_col_spec(tb, dk, offs["gq"]), _col_spec(tb, dk, offs["gk"]),
                  _col_spec(tb, dv, offs["gv"]), _col_spec(tb, dv, offs["gg"]),
                  pl.BlockSpec((None, tb, ga_w), lambda b, h, t: (b, t, ga_col // ga_w)),
                  pl.BlockSpec((ga_w, dk), lambda b, h, t: (0, h)),
                  pl.BlockSpec((1, dk), lambda b, h, t: (0, h)),
                  pl.BlockSpec((1, dv), lambda b, h, t: (0, 0))] + _const_specs(*consts),
        out_specs=pl.BlockSpec((None, tb, dv), lambda b, h, t: (b, t, h)),
        out_shape=jax.ShapeDtypeStruct((B, T, GLA_HEADS * dv), BF16),
        scratch_shapes=[pltpu.VMEM((1, dv, dk), F32)],
        compiler_params=_params("parallel", "parallel", "arbitrary"),
        name="gla_chunks",
    )(h_bf, h_bf, h_bf, h_bf, h_f, wa_pad, b_a, norm_g, *consts)


def _hgrn_call(h_bf, h_f, lb, norm_g, *, offs, heads, dk, dv, tb, nh=4):
    B, T, _ = h_bf.shape
    tb = min(tb, T)
    nh = min(nh, heads)
    assert heads % nh == 0
    grp = 2 if (nh % 2 == 0 and dk == dv and 2 * dk == V7X_MXU_WIDTH) else 1
    consts = _lin_constants(LIN_CHUNK)
    kern = functools.partial(_hgrn_kernel, n_chunks=tb // LIN_CHUNK, nh=nh)
    return pl.pallas_call(
        kern,
        grid=(B, heads // nh, T // tb),
        in_specs=[_col_spec(tb, nh * dk, offs["hq"]),
                  pl.BlockSpec((None, tb, nh * dk), lambda b, h, t: (b, t, h)),
                  _col_spec(tb, nh * dv, offs["hi"]), _col_spec(tb, nh * dv, offs["hg"]),
                  pl.BlockSpec((1, nh * dk), lambda b, h, t: (0, h)),
                  pl.BlockSpec((1, dv), lambda b, h, t: (0, 0))] + _const_specs(*consts),
        out_specs=pl.BlockSpec((None, tb, nh * dv), lambda b, h, t: (b, t, h)),
        out_shape=jax.ShapeDtypeStruct((B, T, heads * dv), BF16),
        scratch_shapes=[pltpu.VMEM((nh // grp, grp * dv, grp * dk), F32)],
        compiler_params=_params("parallel", "parallel", "arbitrary"),
        name="hgrn_chunks",
    )(h_bf, h_f, h_bf, h_bf, lb, norm_g, *consts)


def _memattn_kernel(q_ref, g_ref, k_ref, v_ref, o_ref, *, dh):
    scale = dh ** -0.5
    for h in range(MEM_HEADS):
        cs = slice(h * dh, (h + 1) * dh)
        s = _dot_nt(q_ref[:, cs], k_ref[:, cs]) * scale
        s = s - jnp.max(s, axis=-1, keepdims=True)
        p = jnp.exp(s)
        p = p / jnp.sum(p, axis=-1, keepdims=True)
        o = _dot(p.astype(BF16), v_ref[:, cs])
        o_ref[:, cs] = (o * _silu(g_ref[:, cs].astype(F32))).astype(o_ref.dtype)


def _memattn_call(h_bf, k_mem, v_mem, *, q_col, g_col, tm=512):
    B, T, _ = h_bf.shape
    _, ml, mw = k_mem.shape
    tm = min(tm, T)
    assert q_col % mw == 0 and g_col % mw == 0
    return pl.pallas_call(
        functools.partial(_memattn_kernel, dh=mw // MEM_HEADS),
        grid=(B, T // tm),
        in_specs=[pl.BlockSpec((None, tm, mw), lambda b, t: (b, t, q_col // mw)),
                  pl.BlockSpec((None, tm, mw), lambda b, t: (b, t, g_col // mw)),
                  pl.BlockSpec((None, ml, mw), lambda b, t: (b, 0, 0)),
                  pl.BlockSpec((None, ml, mw), lambda b, t: (b, 0, 0))],
        out_specs=pl.BlockSpec((None, tm, mw), lambda b, t: (b, t, 0)),
        out_shape=jax.ShapeDtypeStruct((B, T, mw), BF16),
        compiler_params=_params("parallel", "parallel"),
        name="mem_attention",
    )(h_bf, h_bf, k_mem, v_mem)


def _outproj_kernel(*refs, segs, nk, n_blocks, alpha, eps, emit_bf16):
    ns = len(segs)
    a_refs = refs[:ns]
    if emit_bf16:
        w_ref, x_hbm, g_ref, b_ref, o_hbm, obf_hbm, acc, x_sem, o_sem, bf_buf, bf_sem = refs[ns:]
    else:
        w_ref, x_hbm, g_ref, b_ref, o_hbm, acc, x_sem, o_sem = refs[ns:]
    i = pl.program_id(0)
    k = pl.program_id(1)
    _, tm, d = acc.shape
    rb, cb = min(128, tm), min(512, d)
    n_rb = tm // rb
    slot = i % 2
    o_ref = acc.at[slot]

    def rows_of(blk):
        return pl.ds(pl.multiple_of(blk * tm, tm), tm)

    def x_copy(blk, s):
        return pltpu.make_async_copy(x_hbm.at[rows_of(blk), :], acc.at[s], x_sem.at[s])

    def o_copy(blk, s):
        return pltpu.make_async_copy(acc.at[s], o_hbm.at[rows_of(blk), :], o_sem.at[s])

    def bf_copy(r, bslot):
        row0 = pl.multiple_of(i * tm + r * rb, rb)
        return pltpu.make_async_copy(bf_buf.at[bslot], obf_hbm.at[pl.ds(row0, rb), :], bf_sem.at[bslot])

    @pl.when((i == 0) & (k == 0))
    def _():
        x_copy(0, 0).start()

    @pl.when(k == 0)
    def _():
        x_copy(i, slot).wait()
        a = a_refs[0][...]
        for c in range(0, d, cb):
            o_ref[:, c:c + cb] = alpha * o_ref[:, c:c + cb] + _dot(a, w_ref[:, c:c + cb])

    for a_ref, (start, n) in zip(a_refs, segs):
        @pl.when((k >= max(start, 1)) & (k < start + n))
        def _(a_ref=a_ref):
            a = a_ref[...]
            for c in range(0, d, cb):
                o_ref[:, c:c + cb] += _dot(a, w_ref[:, c:c + cb])

    @pl.when(k == nk - 1)
    def _():
        @pl.when(i >= 1)
        def _():
            o_copy(i - 1, 1 - slot).wait()

        @pl.when(i + 1 < n_blocks)
        def _():
            x_copy(i + 1, 1 - slot).start()

        def norm(r, carry):
            rows = pl.ds(pl.multiple_of(r * rb, rb), rb)
            y = o_ref[rows, :]
            mu = jnp.mean(y, axis=-1, keepdims=True)
            yc = y - mu
            var = jnp.mean(yc * yc, axis=-1, keepdims=True)
            out = yc * lax.rsqrt(var + eps) * g_ref[...] + b_ref[...]
            o_ref[rows, :] = out
            if emit_bf16:
                n_slot = bf_buf.shape[0]
                bslot = r % n_slot

                @pl.when(r >= n_slot)
                def _():
                    bf_copy(r - n_slot, bslot).wait()

                bf_buf[bslot] = out.astype(bf_buf.dtype)
                bf_copy(r, bslot).start()
            return carry

        lax.fori_loop(0, n_rb, norm, 0)
        o_copy(i, slot).start()
        if emit_bf16:
            for r in range(max(n_rb - bf_buf.shape[0], 0), n_rb):
                bf_copy(r, r % bf_buf.shape[0]).wait()

        @pl.when(i == n_blocks - 1)
        def _():
            o_copy(i, slot).wait()


def _outproj_ln(acts, w_bf, x2d, ln_g, ln_b, alpha, emit_bf16, tm=1024, tk=512):
    m, d = x2d.shape
    tm = min(tm, m)
    segs, start = [], 0
    for a in acts:
        assert a.shape[0] == m and a.shape[1] % tk == 0
        segs.append((start, a.shape[1] // tk))
        start += a.shape[1] // tk
    nk = start
    assert nk * tk == w_bf.shape[0]

    def a_spec(s0, n):
        return pl.BlockSpec((tm, tk), lambda i, k: (i, jnp.clip(k - s0, 0, n - 1)))

    kern = functools.partial(_outproj_kernel, segs=tuple(segs), nk=nk, n_blocks=m // tm, alpha=alpha, eps=1e-5,
                             emit_bf16=emit_bf16)
    rb = min(128, tm)
    out_specs = [pl.BlockSpec(memory_space=pl.ANY)]
    out_shape = [jax.ShapeDtypeStruct((m, d), F32)]
    scratch = [pltpu.VMEM((2, tm, d), F32), pltpu.SemaphoreType.DMA((2,)), pltpu.SemaphoreType.DMA((2,))]
    if emit_bf16:
        out_specs.append(pl.BlockSpec(memory_space=pl.ANY))
        out_shape.append(jax.ShapeDtypeStruct((m, d), BF16))
        n_slot = 4
        scratch += [pltpu.VMEM((n_slot, rb, d), BF16), pltpu.SemaphoreType.DMA((n_slot,))]
    outs = pl.pallas_call(
        kern,
        grid=(m // tm, nk),
        in_specs=[a_spec(s0, n) for s0, n in segs] + [
            pl.BlockSpec((tk, d), lambda i, k: (k, 0)),
            pl.BlockSpec(memory_space=pl.ANY),
            pl.BlockSpec((1, d), lambda i, k: (0, 0)),
            pl.BlockSpec((1, d), lambda i, k: (0, 0))],
        out_specs=out_specs,
        out_shape=out_shape,
        scratch_shapes=scratch,
        compiler_params=_params("arbitrary", "arbitrary"),
        name="outproj_layernorm",
    )(*acts, w_bf, x2d, ln_g, ln_b)
    return (outs[0], outs[1]) if emit_bf16 else (outs[0], None)


def _mlstm_pre_kernel(x_ref, halo_ref, cw_ref, cb_ref, wq_ref, wk_ref, wkt_ref, wv_ref,
                      iq_ref, ik_ref, iv_ref, bif_ref,
                      xc_ref, q_ref, k_ref, kt_ref, v_ref, g_ref, *, n_sub, bw, n_taps):
    t = pl.program_id(1)
    c = pl.program_id(2)
    tm = x_ref.shape[0]

    @pl.when(c == 0)
    def _():
        g_ref[...] = jnp.broadcast_to(bif_ref[...], g_ref.shape)

    gates = jnp.zeros(g_ref.shape, F32)
    for j in range(n_sub):
        cs = slice(j * bw, (j + 1) * bw)
        x_bf = x_ref[:, cs]
        x32 = x_bf.astype(F32)
        halo = jnp.where(t > 0, halo_ref[:, cs].astype(F32), 0.0)
        ext = jnp.concatenate([halo, x32], axis=0)
        prev = pltpu.roll(ext, 1, axis=0)
        acc = jnp.broadcast_to(cb_ref[:, cs], (tm, bw))
        for p in range(n_taps // 2):
            tap = n_taps - 1 - 2 * p
            pair = ext * cw_ref[tap:tap + 1, cs] + prev * cw_ref[tap - 1:tap, cs]
            acc = acc + pair[8 - 2 * p:8 - 2 * p + tm, :]
        xc_bf = _silu(acc).astype(BF16)
        xc_ref[:, cs] = xc_bf
        q_bf = _dot(xc_bf, wq_ref[j]).astype(BF16)
        k_bf = _dot(xc_bf, wk_ref[j]).astype(BF16)
        v_bf = _dot(x_bf, wv_ref[j]).astype(BF16)
        q_ref[:, cs] = q_bf
        k_ref[:, cs] = k_bf
        v_ref[:, cs] = v_bf
        kt_ref[cs, :] = _dot_nt(wkt_ref[j], xc_bf).astype(BF16)
        gates = gates + _dot(q_bf, iq_ref[cs, :]) + _dot(k_bf, ik_ref[cs, :]) + _dot(v_bf, iv_ref[cs, :])
    g_ref[...] += gates


def _block_diag(w, bw):
    nb, bi, bo = w.shape
    assert bi == bo and bw % bi == 0 and (nb * bi) % bw == 0
    per = bw // bi
    w4 = w.reshape(nb // per, per, bi, bo)
    eye = jnp.eye(per, dtype=w.dtype)
    return jnp.einsum("cnio,nm->cnimo", w4, eye).reshape(nb // per, bw, bw)


def _mlstm_pre_call(h_bf, conv_w, conv_b, w_q, w_k, w_v, w_if, b_if, *, width, tm=512, tc=1024, bw=256):
    B, T, _ = h_bf.shape
    tm, tc = min(tm, T), min(tc, width)
    n_taps = conv_w.shape[0]
    assert tm % 8 == 0 and n_taps <= 8 and n_taps % 2 == 0 and width % tc == 0 and tc % bw == 0
    n_sub = tc // bw
    wq = _block_diag(w_q, bw).astype(BF16)
    wk = _block_diag(w_k, bw)
    wkt = jnp.swapaxes(wk, 1, 2).astype(BF16)
    wk = wk.astype(BF16)
    wv = _block_diag(w_v, bw).astype(BF16)
    n_gate = w_if.shape[1]
    wif = jnp.pad(w_if, ((0, 0), (0, 128 - n_gate))).astype(BF16)
    bif = jnp.pad(b_if.astype(F32), (0, 128 - n_gate)).reshape(1, 128)
    hb = tm // 8
    tile_spec = pl.BlockSpec((n_sub, bw, bw), lambda b, t, c: (c, 0, 0))
    act_spec = pl.BlockSpec((None, tm, tc), lambda b, t, c: (b, t, c))
    kern = functools.partial(_mlstm_pre_kernel, n_sub=n_sub, bw=bw, n_taps=n_taps)
    shp = jax.ShapeDtypeStruct((B, T, width), BF16)
    return pl.pallas_call(
        kern,
        grid=(B, T // tm, width // tc),
        in_specs=[act_spec,
                  pl.BlockSpec((None, 8, tc), lambda b, t, c: (b, jnp.maximum(t * hb - 1, 0), c)),
                  pl.BlockSpec((n_taps, tc), lambda b, t, c: (0, c)),
                  pl.BlockSpec((1, tc), lambda b, t, c: (0, c)),
                  tile_spec, tile_spec, tile_spec, tile_spec,
                  pl.BlockSpec((tc, 128), lambda b, t, c: (c, 0)),
                  pl.BlockSpec((tc, 128), lambda b, t, c: (width // tc + c, 0)),
                  pl.BlockSpec((tc, 128), lambda b, t, c: (2 * (width // tc) + c, 0)),
                  pl.BlockSpec((1, 128), lambda b, t, c: (0, 0))],
        out_specs=[act_spec, act_spec, act_spec,
                   pl.BlockSpec((None, tc, tm), lambda b, t, c: (b, c, t)),
                   act_spec,
                   pl.BlockSpec((None, tm, 128), lambda b, t, c: (b, t, 0))],
        out_shape=[shp, shp, shp, jax.ShapeDtypeStruct((B, width, T), BF16), shp,
                   jax.ShapeDtypeStruct((B, T, 128), F32)],
        compiler_params=_params("parallel", "parallel", "arbitrary"),
        name="mlstm_frontend",
    )(h_bf, h_bf, conv_w, conv_b.reshape(1, width), wq, wk, wkt, wv, wif, wif, wif, bif)


def _mlstm_kernel(q_ref, k_ref, kt_ref, v_ref, gc_ref, gr_ref, xc_ref, z_ref, ng_ref, sk_ref, o_ref,
                  c_ref, hh_ref, n_ref, m_ref, *, n_chunks, heads, k_scale, dvb):
    h = pl.program_id(1)
    L = MLSTM_CHUNK
    dk, dv = c_ref.shape

    @pl.when(pl.program_id(2) == 0)
    def _():
        c_ref[...] = jnp.zeros_like(c_ref)
        n_ref[...] = jnp.zeros_like(n_ref)
        m_ref[...] = jnp.zeros_like(m_ref)

    row = lax.broadcasted_iota(jnp.int32, (L, L), 0)
    col = lax.broadcasted_iota(jnp.int32, (L, L), 1)
    causal = row >= col
    glane = lax.broadcasted_iota(jnp.int32, (L, gc_ref.shape[1]), 1)

    for c in range(n_chunks):
        rs = slice(c * L, (c + 1) * L)
        qc = q_ref[rs, :]
        kc = k_ref[rs, :]
        vc = v_ref[rs, :]
        gcol = gc_ref[rs, :]
        i_col = jnp.sum(jnp.where(glane == h, gcol, 0.0), axis=1, keepdims=True)
        f_col = jnp.sum(jnp.where(glane == heads + h, gcol, 0.0), axis=1, keepdims=True)
        i_row = gr_ref[0:1, rs]
        f_row = gr_ref[1:2, rs]
        lf_col = _log_sigmoid(f_col)
        lf_row = _log_sigmoid(f_row)
        b_col = jnp.sum(jnp.where(causal, lf_row, 0.0), axis=1, keepdims=True)
        b_row = jnp.sum(jnp.where(row <= col, lf_col, 0.0), axis=0, keepdims=True)
        b_last = b_col[L - 1:L, :]
        m_prev = m_ref[0:1, 0:1]

        d = jnp.where(causal, b_col - b_row + i_row, NEG_BIG)
        inter_log = b_col + m_prev
        m_i = jnp.maximum(inter_log, jnp.max(d, axis=1, keepdims=True))
        w_intra = jnp.exp(d - m_i)
        w_inter = jnp.exp(inter_log - m_i) * k_scale
        scores = _dot_nt(qc, kc) * k_scale * w_intra
        n_row = n_ref[...]
        den = (jnp.sum(scores, axis=1, keepdims=True)
               + w_inter * jnp.sum(qc.astype(F32) * n_row, axis=1, keepdims=True))
        inv = 1.0 / jnp.maximum(jnp.abs(den), jnp.exp(-m_i))
        scores_bf = scores.astype(BF16)

        log_wj_row = b_last - b_row + i_row
        log_wj_col = b_last - b_col + i_col
        m_new = jnp.maximum(b_last + m_prev, jnp.max(log_wj_row, axis=1, keepdims=True))
        wj_row = jnp.exp(log_wj_row - m_new)
        wj_col = jnp.exp(log_wj_col - m_new)
        dec = jnp.exp(b_last + m_prev - m_new)
        ktw = (kt_ref[:, rs].astype(F32) * wj_row).astype(BF16)

        for j in range(dv // dvb):
            cs = slice(j * dvb, (j + 1) * dvb)
            c_old = c_ref[:, cs]
            num = _dot(scores_bf, vc[:, cs]) + w_inter * _dot(qc, c_old.astype(BF16))
            hh_ref[:, cs] = num * inv
            c_ref[:, cs] = c_old * dec + _dot(ktw, vc[:, cs])
        n_ref[...] = n_row * dec + jnp.sum(kc.astype(F32) * wj_col, axis=0, keepdims=True)
        m_ref[...] = jnp.broadcast_to(m_new, m_ref.shape)

        hh = hh_ref[...]
        mu = jnp.mean(hh, axis=-1, keepdims=True)
        hc = hh - mu
        var = jnp.mean(hc * hc, axis=-1, keepdims=True)
        hn = hc * lax.rsqrt(var + 1e-6) * ng_ref[...]
        out = (hn + sk_ref[...] * xc_ref[rs, :].astype(F32)) * _silu(z_ref[rs, :].astype(F32))
        o_ref[rs, :] = out.astype(o_ref.dtype)


def _mlstm_call(q, k, kt, v, gates, gates_rows, xc, h_bf, norm_g, skip, *, heads, z_col, tb=MLSTM_CHUNK):
    B, T, width = q.shape
    dh = width // heads
    tb = min(tb, T)
    assert tb % MLSTM_CHUNK == 0 and z_col % dh == 0
    dvb = min(512, dh)
    hs = pl.BlockSpec((None, tb, dh), lambda b, h, t: (b, t, h))
    kern = functools.partial(_mlstm_kernel, n_chunks=tb // MLSTM_CHUNK, heads=heads,
                             k_scale=dh ** -0.5, dvb=dvb)
    return pl.pallas_call(
        kern,
        grid=(B, heads, T // tb),
        in_specs=[hs, hs,
                  pl.BlockSpec((None, dh, tb), lambda b, h, t: (b, h, t)),
                  hs,
                  pl.BlockSpec((None, tb, gates.shape[2]), lambda b, h, t: (b, t, 0)),
                  pl.BlockSpec((None, None, 2, tb), lambda b, h, t: (b, h, 0, t)),
                  hs,
                  pl.BlockSpec((None, tb, dh), lambda b, h, t: (b, t, z_col // dh + h)),
                  pl.BlockSpec((1, dh), lambda b, h, t: (0, h)),
                  pl.BlockSpec((1, dh), lambda b, h, t: (0, h))],
        out_specs=hs,
        out_shape=jax.ShapeDtypeStruct((B, T, width), BF16),
        scratch_shapes=[pltpu.VMEM((dh, dh), F32), pltpu.VMEM((MLSTM_CHUNK, dh), F32),
                        pltpu.VMEM((1, dh), F32), pltpu.VMEM((8, 128), F32)],
        compiler_params=_params("parallel", "parallel", "arbitrary"),
        name="mlstm_chunks",
    )(q, k, kt, v, gates, gates_rows, xc, h_bf, norm_g, skip)


def _mem_kv(mem_bf, w_k, w_v, B):
    mw = w_k.shape[1]
    k = _matmul(mem_bf, w_k.astype(BF16), BF16).reshape(B, -1, mw)
    v = _matmul(mem_bf, w_v.astype(BF16), BF16).reshape(B, -1, mw)
    return k, v


def _even_layer(x, x_bf, emit_bf16, mem_bf, lb, w_in_stack, li, gla_w_a2, gla_b_a, gla_norm_g, hgrn_norm_g,
                mem_w_k, mem_w_v, w_out, ln_g, ln_b, alpha):
    B, T, D = x.shape
    lowrank, kw = gla_w_a2.shape
    gdv = gla_norm_g.shape[0]
    gw = GLA_HEADS * gdv
    gdk = kw // GLA_HEADS
    fw = lb.shape[0]
    hheads = fw // HGRN_EXPAND
    hdv = hgrn_norm_g.shape[0]
    hw = hheads * hdv
    mw = mem_w_k.shape[1]
    sizes = (kw, kw, gw, gw, lowrank, fw, fw, hw, hw, mw, mw)
    assert sum(sizes) == w_in_stack.shape[2]
    bounds = [0]
    for s in sizes:
        bounds.append(bounds[-1] + s)
    ga_w = 256
    w_a = _window_cast(w_in_stack, li, 0, bounds[4])
    w_b = _window_cast(w_in_stack, li, bounds[5], bounds[-1] - bounds[5])
    w_ga = jnp.pad(w_in_stack[li, :, bounds[4]:bounds[5]], ((0, 0), (0, ga_w - lowrank))).astype(BF16)
    offs_a = {"gq": 0, "gk": kw, "gv": 2 * kw, "gg": 2 * kw + gw}
    offs_b = {"hq": 0, "hi": fw, "hg": fw + hw, "mq": fw + 2 * hw, "mg": fw + 2 * hw + mw}

    x2d = x.reshape(B * T, D)
    if x_bf is None:
        h_ga, x_bf = _matmul_cast(x2d, w_ga, F32)
    else:
        h_ga = _matmul(x_bf, w_ga, F32)
    h_a = _matmul(x_bf, w_a, BF16)
    h_b = _matmul(x_bf, w_b, BF16, cols=[(0, fw), (2 * fw, bounds[-1] - bounds[5] - 2 * fw)])
    h_hf = _matmul(x_bf, w_b, F32, cols=[(fw, fw)])
    h_a, h_b, h_hf, h_ga = (h.reshape(B, T, -1) for h in (h_a, h_b, h_hf, h_ga))

    wa_pad = jnp.pad(gla_w_a2, ((0, ga_w - lowrank), (0, 0))).astype(BF16)
    gla_out = _gla_call(h_a, h_ga, wa_pad, gla_b_a.reshape(1, kw), gla_norm_g.reshape(1, gdv),
                        offs=offs_a, ga_col=0, dk=gdk, dv=gdv, tb=512)
    hgrn_out = _hgrn_call(h_b, h_hf, lb.reshape(1, fw), hgrn_norm_g.reshape(1, hdv),
                          offs=offs_b, heads=hheads, dk=HGRN_EXPAND, dv=hdv, tb=512)
    k_mem, v_mem = _mem_kv(mem_bf, mem_w_k, mem_w_v, B)
    mem_out = _memattn_call(h_b, k_mem, v_mem, q_col=offs_b["mq"], g_col=offs_b["mg"])

    acts = [a.reshape(B * T, -1) for a in (gla_out, hgrn_out, mem_out)]
    y, y_bf = _outproj_ln(acts, w_out.astype(BF16), x2d, ln_g.reshape(1, D), ln_b.reshape(1, D), alpha, emit_bf16)
    return y.reshape(B, T, D), y_bf


def _odd_layer(x, x_bf, emit_bf16, mem_bf, w_in, conv_w, conv_b, w_q, w_k, w_v, w_if, b_if, mh_norm_g, skip,
               mem_w_k, mem_w_v, w_out, ln_g, ln_b, alpha):
    B, T, D = x.shape
    width = conv_w.shape[1]
    heads = b_if.shape[0] // 2
    mw = mem_w_k.shape[1]
    assert w_in.shape[1] == 2 * width + 2 * mw

    x2d = x.reshape(B * T, D)
    if x_bf is None:
        x_bf = x2d.astype(BF16)
    h_bf = _matmul(x_bf, w_in.astype(BF16), BF16).reshape(B, T, -1)

    xc, q, k, kt, v, gates = _mlstm_pre_call(h_bf, conv_w, conv_b, w_q, w_k, w_v, w_if, b_if, width=width)
    gates_rows = jnp.transpose(gates[:, :, :2 * heads].reshape(B, T, 2, heads), (0, 3, 2, 1))
    mlstm_out = _mlstm_call(q, k, kt, v, gates, gates_rows, xc, h_bf,
                            mh_norm_g.reshape(1, width), skip.reshape(1, width), heads=heads, z_col=width)
    k_mem, v_mem = _mem_kv(mem_bf, mem_w_k, mem_w_v, B)
    mem_out = _memattn_call(h_bf, k_mem, v_mem, q_col=2 * width, g_col=2 * width + mw)

    acts = [a.reshape(B * T, -1) for a in (mlstm_out, mem_out)]
    y, y_bf = _outproj_ln(acts, w_out.astype(BF16), x2d, ln_g.reshape(1, D), ln_b.reshape(1, D), alpha, emit_bf16)
    return y.reshape(B, T, D), y_bf


def kernel(x, mem, hgrn_lb_logits, ev_w_in, ev_gla_w_a2, ev_gla_b_a, ev_gla_norm_g, ev_hgrn_norm_g, ev_mem_w_k, ev_mem_w_v, ev_w_out, ev_ln_g, ev_ln_b, od_w_in, od_conv_w, od_conv_b, od_w_q, od_w_k, od_w_v, od_w_if, od_b_if, od_mh_norm_g, od_skip, od_mem_w_k, od_mem_w_v, od_w_out, od_ln_g, od_ln_b):
    depth = ev_w_in.shape[0] + od_w_in.shape[0]
    alpha = (2 * depth) ** 0.25
    B, ml, D = mem.shape
    mem_bf = mem.reshape(B * ml, D).astype(BF16)
    lb_all = jnp.cumsum(jax.nn.softmax(hgrn_lb_logits.astype(F32), axis=0), axis=0)
    x_bf = None
    for layer in range(depth):
        i = layer // 2
        emit_bf16 = layer + 1 < depth
        if layer % 2 == 0:
            x, x_bf = _even_layer(x, x_bf, emit_bf16, mem_bf, lb_all[layer], ev_w_in, i, ev_gla_w_a2[i],
                                  ev_gla_b_a[i], ev_gla_norm_g[i], ev_hgrn_norm_g[i], ev_mem_w_k[i],
                                  ev_mem_w_v[i], ev_w_out[i], ev_ln_g[i], ev_ln_b[i], alpha)
        else:
            x, x_bf = _odd_layer(x, x_bf, emit_bf16, mem_bf, od_w_in[i], od_conv_w[i], od_conv_b[i], od_w_q[i],
                                 od_w_k[i], od_w_v[i], od_w_if[i], od_b_if[i], od_mh_norm_g[i], od_skip[i],
                                 od_mem_w_k[i], od_mem_w_v[i], od_w_out[i], od_ln_g[i], od_ln_b[i], alpha)
    return x
```

```python
import functools

import jax
import jax.numpy as jnp
import numpy as np
from jax import lax
from jax.experimental import pallas as pl
from jax.experimental.pallas import tpu as pltpu

F32 = jnp.float32
BF16 = jnp.bfloat16

GLA_HEADS = 4
GLA_TAU = 16.0
HGRN_EXPAND = 128
MEM_HEADS = 4

LIN_CHUNK = 128
MLSTM_CHUNK = 256

V7X_VMEM_LIMIT_BYTES = 56 * 1024 * 1024
V7X_MXU_WIDTH = 256
LANES = 128
SUBLANES = 8
NEG_BIG = -1e30
LOG2_E = 1.4426950408889634


def _sigmoid(x):
    return 1.0 / (1.0 + jnp.exp(-x))


def _silu(x):
    return x * _sigmoid(x)


def _log_sigmoid(x):
    return jnp.minimum(x, 0.0) - jnp.log(1.0 + jnp.exp(-jnp.abs(x)))


def _dot(a, b):
    return jnp.dot(a, b, preferred_element_type=F32)


def _dot_nt(a, b):
    return lax.dot_general(a, b, (((1,), (1,)), ((), ())), preferred_element_type=F32)


def _dot_tn(a, b):
    return lax.dot_general(a, b, (((0,), (0,)), ((), ())), preferred_element_type=F32)


def _params(*sem):
    return pltpu.CompilerParams(dimension_semantics=sem, vmem_limit_bytes=V7X_VMEM_LIMIT_BYTES)


def _mm_kernel(a_ref, w_ref, o_ref):
    o_ref[...] = _dot(a_ref[...], w_ref[...]).astype(o_ref.dtype)


def _matmul(a, w, out_dtype, tm=1024, tn=1024, cols=None):
    m, k = a.shape
    cols = [(0, w.shape[1])] if cols is None else list(cols)
    tm = min(tm, m)
    while any(v % tn for rng in cols for v in rng):
        tn //= 2
    assert m % tm == 0 and tn % LANES == 0, (m, cols, tm, tn)
    n_out = sum(n for _, n in cols)

    def w_block(i, j):
        blk, done = None, 0
        for first, n in cols:
            here = first // tn + (j - done)
            blk = here if blk is None else jnp.where(j >= done, here, blk)
            done += n // tn
        return 0, blk

    return pl.pallas_call(
        _mm_kernel,
        grid=(m // tm, n_out // tn),
        in_specs=[pl.BlockSpec((tm, k), lambda i, j: (i, 0)),
                  pl.BlockSpec((k, tn), w_block)],
        out_specs=pl.BlockSpec((tm, tn), lambda i, j: (i, j)),
        out_shape=jax.ShapeDtypeStruct((m, n_out), out_dtype),
        compiler_params=_params("parallel", "arbitrary"),
        name="proj_matmul",
    )(a, w)


def _mm_cast_kernel(a_ref, w_ref, o_ref, abf_ref):
    a_bf = a_ref[...].astype(abf_ref.dtype)
    abf_ref[...] = a_bf
    o_ref[...] = _dot(a_bf, w_ref[...]).astype(o_ref.dtype)


def _matmul_cast(a, w, out_dtype, tm=512):
    m, k = a.shape
    n = w.shape[1]
    tm = min(tm, m)
    assert m % tm == 0 and n % LANES == 0
    return pl.pallas_call(
        _mm_cast_kernel,
        grid=(m // tm,),
        in_specs=[pl.BlockSpec((tm, k), lambda i: (i, 0)), pl.BlockSpec((k, n), lambda i: (0, 0))],
        out_specs=[pl.BlockSpec((tm, n), lambda i: (i, 0)), pl.BlockSpec((tm, k), lambda i: (i, 0))],
        out_shape=[jax.ShapeDtypeStruct((m, n), out_dtype), jax.ShapeDtypeStruct((m, k), BF16)],
        compiler_params=_params("parallel"),
        name="proj_matmul_cast",
    )(a, w)


def _window_cast_kernel(x_ref, o_ref):
    o_ref[...] = x_ref[0].T.astype(o_ref.dtype)


def _window_cast(w_stack, layer, col0, ncols, tr=1024, tkb=1024):
    wt = jnp.swapaxes(w_stack, 1, 2)
    _, n, k = wt.shape
    tkb = min(tkb, k)
    while ncols % tr:
        tr //= 2
    assert col0 % SUBLANES == 0 and tr % LANES == 0 and k % tkb == 0 and col0 + ncols <= n, (col0, ncols, tr)
    return pl.pallas_call(
        _window_cast_kernel,
        grid=(ncols // tr, k // tkb),
        in_specs=[pl.BlockSpec((pl.Element(1), pl.Element(tr), pl.Element(tkb)),
                               lambda i, j: (layer, pl.multiple_of(col0 + i * tr, SUBLANES),
                                             pl.multiple_of(j * tkb, LANES)))],
        out_specs=pl.BlockSpec((tkb, tr), lambda i, j: (j, i)),
        out_shape=jax.ShapeDtypeStruct((k, ncols), BF16),
        compiler_params=_params("parallel", "parallel"),
        name="weight_window_cast",
    )(wt)


def _lin_constants(L):
    i = np.arange(L)[:, None]
    t = np.arange(L)[None, :]
    sums = [t <= i]
    masks = [i == t]
    m = L // 2
    while m >= 1:
        r = (i // (2 * m)) * 2 * m + m - 1
        second = (i & m) != 0
        sums.append(np.where(second, (t > r) & (t <= i), (t > i) & (t <= r)))
        masks.append((i // (2 * m) == t // (2 * m)) & second & ((t & m) == 0))
        m //= 2
    sums.append(t > i)
    s = np.concatenate(sums, axis=0).astype(np.float32)
    return (jnp.asarray(np.concatenate([s, s], axis=1), BF16),
            jnp.asarray(np.stack(masks).astype(np.float32)))


def _block_rows(x, grp, d):
    if grp == 1:
        return x
    lane = lax.broadcasted_iota(jnp.int32, x.shape, 1)
    zero = jnp.zeros_like(x)
    return jnp.concatenate([jnp.where((lane >= r * d) & (lane < (r + 1) * d), x, zero) for r in range(grp)], axis=0)


def _lin_attn_chunk(q, k, v_bf, g, st_ref, sums_ref, mask_ref, nh):
    L, wk = q.shape
    dk = wk // nh
    dv = v_bf.shape[1] // nh
    grp = st_ref.shape[1] // dv
    n_lvl = mask_ref.shape[0] - 1
    g2 = g * LOG2_E
    hi = g2.astype(BF16)
    lo = (g2 - hi.astype(F32)).astype(BF16)
    p = jnp.exp2(_dot(sums_ref[...], jnp.concatenate([hi, lo], axis=0)))
    p_last = p[L - 1:L, :]
    qd = (q * p[0:L]).astype(BF16)
    kd = (k * p[(n_lvl + 1) * L:(n_lvl + 2) * L]).astype(BF16)
    q_bf = q.astype(BF16)
    k_bf = k.astype(BF16)
    rowi = lax.broadcasted_iota(jnp.int32, (L, 1), 0)
    xs = []
    for lvl in range(n_lvl):
        second = (rowi & (L >> (lvl + 1))) != 0
        xs.append((jnp.where(second, q, k) * p[(lvl + 1) * L:(lvl + 2) * L]).astype(BF16))
    def mask(l):
        m = mask_ref[l]
        return m if grp == 1 else jnp.concatenate([m] * grp, axis=1)

    outs = []
    for h in range(nh // grp):
        ck = slice(h * grp * dk, (h + 1) * grp * dk)
        cv = slice(h * grp * dv, (h + 1) * grp * dv)
        a = _dot_nt(q_bf[:, ck], _block_rows(k_bf[:, ck], grp, dk)) * mask(0)
        for lvl in range(n_lvl):
            x = xs[lvl][:, ck]
            a = a + _dot_nt(x, _block_rows(x, grp, dk)) * mask(lvl + 1)
        st = st_ref[h]
        v_blk = _block_rows(v_bf[:, cv], grp, dv)
        o = _dot_nt(qd[:, ck], st.astype(BF16)) + _dot(a.astype(BF16), v_blk)
        st_ref[h] = st * p_last[:, ck] + _dot_tn(v_blk, _block_rows(kd[:, ck], grp, dk))
        outs.extend(o[:, r * dv:(r + 1) * dv] for r in range(grp))
    return outs


def _rms_gate(o, norm_g, gate, eps=1e-6):
    ms = jnp.mean(o * o, axis=-1, keepdims=True)
    return o * lax.rsqrt(ms + eps) * norm_g * _silu(gate)


def _gla_kernel(q_ref, k_ref, v_ref, gg_ref, ga_ref, wa_ref, ba_ref, ng_ref, sums_ref, mask_ref, o_ref,
                st_ref, *, n_chunks, q_scale):
    @pl.when(pl.program_id(2) == 0)
    def _():
        st_ref[...] = jnp.zeros_like(st_ref)

    def chunk(c, carry):
        rows = pl.ds(pl.multiple_of(c * LIN_CHUNK, LIN_CHUNK), LIN_CHUNK)
        q = q_ref[rows, :].astype(F32) * q_scale
        k = k_ref[rows, :].astype(F32)
        a_pre = _dot(ga_ref[rows, :].astype(BF16), wa_ref[...]) + ba_ref[...]
        g = _log_sigmoid(a_pre) * (1.0 / GLA_TAU)
        (o,) = _lin_attn_chunk(q, k, v_ref[rows, :], g, st_ref, sums_ref, mask_ref, 1)
        o_ref[rows, :] = _rms_gate(o, ng_ref[...], gg_ref[rows, :].astype(F32)).astype(o_ref.dtype)
        return carry

    lax.fori_loop(0, n_chunks, chunk, 0, unroll=4)


def _hgrn_kernel(q_ref, f_ref, v_ref, hg_ref, lb_ref, ng_ref, sums_ref, mask_ref, o_ref,
                 st_ref, *, n_chunks, nh):
    dv = ng_ref.shape[1]

    @pl.when(pl.program_id(2) == 0)
    def _():
        st_ref[...] = jnp.zeros_like(st_ref)

    def chunk(c, carry):
        rows = pl.ds(pl.multiple_of(c * LIN_CHUNK, LIN_CHUNK), LIN_CHUNK)
        lb = lb_ref[...]
        f = lb + (1.0 - lb) * _sigmoid(f_ref[rows, :])
        q = _silu(q_ref[rows, :].astype(F32))
        outs = _lin_attn_chunk(q, 1.0 - f, v_ref[rows, :], jnp.log(f), st_ref, sums_ref, mask_ref, nh)
        gate = hg_ref[rows, :].astype(F32)
        for h, o in enumerate(outs):
            cv = slice(h * dv, (h + 1) * dv)
            o_ref[rows, cv] = _rms_gate(o, ng_ref[...], gate[:, cv]).astype(o_ref.dtype)
        return carry

    lax.fori_loop(0, n_chunks, chunk, 0, unroll=4)


def _col_spec(tb, width, col0):
    assert col0 % width == 0, (col0, width)
    base = col0 // width
    return pl.BlockSpec((None, tb, width), lambda b, h, t: (b, t, base + h))


def _const_specs(*arrays):
    return [pl.BlockSpec(a.shape, lambda b, h, t, nd=a.ndim: (0,) * nd) for a in arrays]


def _gla_call(h_bf, h_f, wa_pad, b_a, norm_g, *, offs, ga_col, dk, dv, tb):
    B, T, _ = h_bf.shape
    tb = min(tb, T)
    ga_w = wa_pad.shape[0]
    assert ga_col % ga_w == 0
    consts = _lin_constants(LIN_CHUNK)
    kern = functools.partial(_gla_kernel, n_chunks=tb // LIN_CHUNK, q_scale=dk ** -0.5)
    return pl.pallas_call(
        kern,
        grid=(B, GLA_HEADS, T // tb),
        in_specs=[_col_spec(tb, dk, offs["gq"]), _col_spec(tb, dk, offs["gk"]),
                  _col_spec(tb, dv, offs["gv"]), _col_spec(tb, dv, offs["gg"]),
                  pl.BlockSpec((None, tb, ga_w), lambda b, h, t: (b, t, ga_col // ga_w)),
                  pl.BlockSpec((ga_w, dk), lambda b, h, t: (0, h)),
                  pl.BlockSpec((1, dk), lambda b, h, t: (0, h)),
                  pl.BlockSpec((1, dv), lambda b, h, t: (0, 0))] + _const_specs(*consts),
        out_specs=pl.BlockSpec((None, tb, dv), lambda b, h, t: (b, t, h)),
        out_shape=jax.ShapeDtypeStruct((B, T, GLA_HEADS * dv), BF16),
        scratch_shapes=[pltpu.VMEM((1, dv, dk), F32)],
        compiler_params=_params("parallel", "parallel", "arbitrary"),
        name="gla_chunks",
    )(h_bf, h_bf, h_bf, h_bf, h_f, wa_pad, b_a, norm_g, *consts)


def _hgrn_call(h_bf, h_f, lb, norm_g, *, offs, heads, dk, dv, tb, nh=4):
    B, T, _ = h_bf.shape
    tb = min(tb, T)
    nh = min(nh, heads)
    assert heads % nh == 0
    grp = 2 if (nh % 2 == 0 and dk == dv and 2 * dk == V7X_MXU_WIDTH) else 1
    consts = _lin_constants(LIN_CHUNK)
    kern = functools.partial(_hgrn_kernel, n_chunks=tb // LIN_CHUNK, nh=nh)
    return pl.pallas_call(
        kern,
        grid=(B, heads // nh, T // tb),
        in_specs=[_col_spec(tb, nh * dk, offs["hq"]),
                  pl.BlockSpec((None, tb, nh * dk), lambda b, h, t: (b, t, h)),
                  _col_spec(tb, nh * dv, offs["hi"]), _col_spec(tb, nh * dv, offs["hg"]),
                  pl.BlockSpec((1, nh * dk), lambda b, h, t: (0, h)),
                  pl.BlockSpec((1, dv), lambda b, h, t: (0, 0))] + _const_specs(*consts),
        out_specs=pl.BlockSpec((None, tb, nh * dv), lambda b, h, t: (b, t, h)),
        out_shape=jax.ShapeDtypeStruct((B, T, heads * dv), BF16),
        scratch_shapes=[pltpu.VMEM((nh // grp, grp * dv, grp * dk), F32)],
        compiler_params=_params("parallel", "parallel", "arbitrary"),
        name="hgrn_chunks",
    )(h_bf, h_f, h_bf, h_bf, lb, norm_g, *consts)


def _memattn_kernel(q_ref, g_ref, k_ref, v_ref, o_ref, *, dh):
    scale = dh ** -0.5
    for h in range(MEM_HEADS):
        cs = slice(h * dh, (h + 1) * dh)
        s = _dot_nt(q_ref[:, cs], k_ref[:, cs]) * scale
        s = s - jnp.max(s, axis=-1, keepdims=True)
        p = jnp.exp(s)
        p = p / jnp.sum(p, axis=-1, keepdims=True)
        o = _dot(p.astype(BF16), v_ref[:, cs])
        o_ref[:, cs] = (o * _silu(g_ref[:, cs].astype(F32))).astype(o_ref.dtype)


def _memattn_call(h_bf, k_mem, v_mem, *, q_col, g_col, tm=512):
    B, T, _ = h_bf.shape
    _, ml, mw = k_mem.shape
    tm = min(tm, T)
    assert q_col % mw == 0 and g_col % mw == 0
    return pl.pallas_call(
        functools.partial(_memattn_kernel, dh=mw // MEM_HEADS),
        grid=(B, T // tm),
        in_specs=[pl.BlockSpec((None, tm, mw), lambda b, t: (b, t, q_col // mw)),
                  pl.BlockSpec((None, tm, mw), lambda b, t: (b, t, g_col // mw)),
                  pl.BlockSpec((None, ml, mw), lambda b, t: (b, 0, 0)),
                  pl.BlockSpec((None, ml, mw), lambda b, t: (b, 0, 0))],
        out_specs=pl.BlockSpec((None, tm, mw), lambda b, t: (b, t, 0)),
        out_shape=jax.ShapeDtypeStruct((B, T, mw), BF16),
        compiler_params=_params("parallel", "parallel"),
        name="mem_attention",
    )(h_bf, h_bf, k_mem, v_mem)


def _outproj_kernel(*refs, segs, nk, n_blocks, alpha, eps, emit_bf16):
    ns = len(segs)
    a_refs = refs[:ns]
    if emit_bf16:
        w_ref, x_hbm, g_ref, b_ref, o_hbm, obf_hbm, acc, x_sem, o_sem, bf_buf, bf_sem = refs[ns:]
    else:
        w_ref, x_hbm, g_ref, b_ref, o_hbm, acc, x_sem, o_sem = refs[ns:]
    i = pl.program_id(0)
    k = pl.program_id(1)
    _, tm, d = acc.shape
    rb, cb = min(128, tm), min(512, d)
    n_rb = tm // rb
    slot = i % 2
    o_ref = acc.at[slot]

    def rows_of(blk):
        return pl.ds(pl.multiple_of(blk * tm, tm), tm)

    def x_copy(blk, s):
        return pltpu.make_async_copy(x_hbm.at[rows_of(blk), :], acc.at[s], x_sem.at[s])

    def o_copy(blk, s):
        return pltpu.make_async_copy(acc.at[s], o_hbm.at[rows_of(blk), :], o_sem.at[s])

    def bf_copy(r, bslot):
        row0 = pl.multiple_of(i * tm + r * rb, rb)
        return pltpu.make_async_copy(bf_buf.at[bslot], obf_hbm.at[pl.ds(row0, rb), :], bf_sem.at[bslot])

    @pl.when((i == 0) & (k == 0))
    def _():
        x_copy(0, 0).start()

    @pl.when(k == 0)
    def _():
        x_copy(i, slot).wait()
        a = a_refs[0][...]
        for c in range(0, d, cb):
            o_ref[:, c:c + cb] = alpha * o_ref[:, c:c + cb] + _dot(a, w_ref[:, c:c + cb])

    for a_ref, (start, n) in zip(a_refs, segs):
        @pl.when((k >= max(start, 1)) & (k < start + n))
        def _(a_ref=a_ref):
            a = a_ref[...]
            for c in range(0, d, cb):
                o_ref[:, c:c + cb] += _dot(a, w_ref[:, c:c + cb])

    @pl.when(k == nk - 1)
    def _():
        @pl.when(i >= 1)
        def _():
            o_copy(i - 1, 1 - slot).wait()

        @pl.when(i + 1 < n_blocks)
        def _():
            x_copy(i + 1, 1 - slot).start()

        def norm(r, carry):
            rows = pl.ds(pl.multiple_of(r * rb, rb), rb)
            y = o_ref[rows, :]
            mu = jnp.mean(y, axis=-1, keepdims=True)
            yc = y - mu
            var = jnp.mean(yc * yc, axis=-1, keepdims=True)
            out = yc * lax.rsqrt(var + eps) * g_ref[...] + b_ref[...]
            o_ref[rows, :] = out
            if emit_bf16:
                n_slot = bf_buf.shape[0]
                bslot = r % n_slot

                @pl.when(r >= n_slot)
                def _():
                    bf_copy(r - n_slot, bslot).wait()

                bf_buf[bslot] = out.astype(bf_buf.dtype)
                bf_copy(r, bslot).start()
            return carry

        lax.fori_loop(0, n_rb, norm, 0)
        o_copy(i, slot).start()
        if emit_bf16:
            for r in range(max(n_rb - bf_buf.shape[0], 0), n_rb):
                bf_copy(r, r % bf_buf.shape[0]).wait()

        @pl.when(i == n_blocks - 1)
        def _():
            o_copy(i, slot).wait()


def _outproj_ln(acts, w_bf, x2d, ln_g, ln_b, alpha, emit_bf16, tm=1024, tk=512):
    m, d = x2d.shape
    tm = min(tm, m)
    segs, start = [], 0
    for a in acts:
        assert a.shape[0] == m and a.shape[1] % tk == 0
        segs.append((start, a.shape[1] // tk))
        start += a.shape[1] // tk
    nk = start
    assert nk * tk == w_bf.shape[0]

    def a_spec(s0, n):
        return pl.BlockSpec((tm, tk), lambda i, k: (i, jnp.clip(k - s0, 0, n - 1)))

    kern = functools.partial(_outproj_kernel, segs=tuple(segs), nk=nk, n_blocks=m // tm, alpha=alpha, eps=1e-5,
                             emit_bf16=emit_bf16)
    rb = min(128, tm)
    out_specs = [pl.BlockSpec(memory_space=pl.ANY)]
    out_shape = [jax.ShapeDtypeStruct((m, d), F32)]
    scratch = [pltpu.VMEM((2, tm, d), F32), pltpu.SemaphoreType.DMA((2,)), pltpu.SemaphoreType.DMA((2,))]
    if emit_bf16:
        out_specs.append(pl.BlockSpec(memory_space=pl.ANY))
        out_shape.append(jax.ShapeDtypeStruct((m, d), BF16))
        n_slot = 4
        scratch += [pltpu.VMEM((n_slot, rb, d), BF16), pltpu.SemaphoreType.DMA((n_slot,))]
    outs = pl.pallas_call(
        kern,
        grid=(m // tm, nk),
        in_specs=[a_spec(s0, n) for s0, n in segs] + [
            pl.BlockSpec((tk, d), lambda i, k: (k, 0)),
            pl.BlockSpec(memory_space=pl.ANY),
            pl.BlockSpec((1, d), lambda i, k: (0, 0)),
            pl.BlockSpec((1, d), lambda i, k: (0, 0))],
        out_specs=out_specs,
        out_shape=out_shape,
        scratch_shapes=scratch,
        compiler_params=_params("arbitrary", "arbitrary"),
        name="outproj_layernorm",
    )(*acts, w_bf, x2d, ln_g, ln_b)
    return (outs[0], outs[1]) if emit_bf16 else (outs[0], None)


def _mlstm_pre_kernel(x_ref, halo_ref, cw_ref, cb_ref, wq_ref, wk_ref, wv_ref,
                      iq_ref, ik_ref, iv_ref, bif_ref,
                      xc_ref, q_ref, k_ref, v_ref, g_ref, *, n_sub, bw, n_taps):
    t = pl.program_id(1)
    c = pl.program_id(2)
    tm = x_ref.shape[0]

    @pl.when(c == 0)
    def _():
        g_ref[...] = jnp.broadcast_to(bif_ref[...], g_ref.shape)

    gates = jnp.zeros(g_ref.shape, F32)
    for j in range(n_sub):
        cs = slice(j * bw, (j + 1) * bw)
        x_bf = x_ref[:, cs]
        x32 = x_bf.astype(F32)
        halo = jnp.where(t > 0, halo_ref[:, cs].astype(F32), 0.0)
        ext = jnp.concatenate([halo, x32], axis=0)
        prev = pltpu.roll(ext, 1, axis=0)
        acc = jnp.broadcast_to(cb_ref[:, cs], (tm, bw))
        for p in range(n_taps // 2):
            tap = n_taps - 1 - 2 * p
            pair = ext * cw_ref[tap:tap + 1, cs] + prev * cw_ref[tap - 1:tap, cs]
            acc = acc + pair[SUBLANES - 2 * p:SUBLANES - 2 * p + tm, :]
        xc_bf = _silu(acc).astype(BF16)
        xc_ref[:, cs] = xc_bf
        q_bf = _dot(xc_bf, wq_ref[j]).astype(BF16)
        k_bf = _dot(xc_bf, wk_ref[j]).astype(BF16)
        v_bf = _dot(x_bf, wv_ref[j]).astype(BF16)
        q_ref[:, cs] = q_bf
        k_ref[:, cs] = k_bf
        v_ref[:, cs] = v_bf
        gates = gates + _dot(q_bf, iq_ref[cs, :]) + _dot(k_bf, ik_ref[cs, :]) + _dot(v_bf, iv_ref[cs, :])
    g_ref[...] += gates


def _block_diag(w, bw):
    nb, bi, bo = w.shape
    assert bi == bo and bw % bi == 0 and (nb * bi) % bw == 0
    per = bw // bi
    w4 = w.reshape(nb // per, per, bi, bo)
    eye = jnp.eye(per, dtype=w.dtype)
    return jnp.einsum("cnio,nm->cnimo", w4, eye).reshape(nb // per, bw, bw)


def _mlstm_pre_call(h_bf, conv_w, conv_b, w_q, w_k, w_v, w_if, b_if, *, width, tm=1024, tc=1024, bw=256):
    B, T, _ = h_bf.shape
    tm, tc = min(tm, T), min(tc, width)
    n_taps = conv_w.shape[0]
    assert tm % SUBLANES == 0 and n_taps <= SUBLANES and n_taps % 2 == 0 and width % tc == 0 and tc % bw == 0
    n_sub = tc // bw
    wq = _block_diag(w_q, bw).astype(BF16)
    wk = _block_diag(w_k, bw).astype(BF16)
    wv = _block_diag(w_v, bw).astype(BF16)
    n_gate = w_if.shape[1]
    wif = jnp.pad(w_if, ((0, 0), (0, LANES - n_gate))).astype(BF16)
    bif = jnp.pad(b_if.astype(F32), (0, LANES - n_gate)).reshape(1, LANES)
    hb = tm // SUBLANES
    tile_spec = pl.BlockSpec((n_sub, bw, bw), lambda b, t, c: (c, 0, 0))
    act_spec = pl.BlockSpec((None, tm, tc), lambda b, t, c: (b, t, c))
    kern = functools.partial(_mlstm_pre_kernel, n_sub=n_sub, bw=bw, n_taps=n_taps)
    shp = jax.ShapeDtypeStruct((B, T, width), BF16)
    return pl.pallas_call(
        kern,
        grid=(B, T // tm, width // tc),
        in_specs=[act_spec,
                  pl.BlockSpec((None, SUBLANES, tc), lambda b, t, c: (b, jnp.maximum(t * hb - 1, 0), c)),
                  pl.BlockSpec((n_taps, tc), lambda b, t, c: (0, c)),
                  pl.BlockSpec((1, tc), lambda b, t, c: (0, c)),
                  tile_spec, tile_spec, tile_spec,
                  pl.BlockSpec((tc, LANES), lambda b, t, c: (c, 0)),
                  pl.BlockSpec((tc, LANES), lambda b, t, c: (width // tc + c, 0)),
                  pl.BlockSpec((tc, LANES), lambda b, t, c: (2 * (width // tc) + c, 0)),
                  pl.BlockSpec((1, LANES), lambda b, t, c: (0, 0))],
        out_specs=[act_spec, act_spec, act_spec, act_spec,
                   pl.BlockSpec((None, tm, LANES), lambda b, t, c: (b, t, 0))],
        out_shape=[shp, shp, shp, shp,
                   jax.ShapeDtypeStruct((B, T, LANES), F32)],
        compiler_params=_params("parallel", "parallel", "arbitrary"),
        name="mlstm_frontend",
    )(h_bf, h_bf, conv_w, conv_b.reshape(1, width), wq, wk, wv, wif, wif, wif, bif)


def _mlstm_kernel(q_ref, k_ref, v_ref, gc_ref, gr_ref, xc_ref, z_ref, ng_ref, sk_ref, o_ref,
                  c_ref, hh_ref, n_ref, m_ref, *, n_chunks, heads, k_scale, dvb):
    h = pl.program_id(1)
    L = MLSTM_CHUNK
    dk, dv = c_ref.shape

    @pl.when(pl.program_id(2) == 0)
    def _():
        c_ref[...] = jnp.zeros_like(c_ref)
        n_ref[...] = jnp.zeros_like(n_ref)
        m_ref[...] = jnp.zeros_like(m_ref)

    row = lax.broadcasted_iota(jnp.int32, (L, L), 0)
    col = lax.broadcasted_iota(jnp.int32, (L, L), 1)
    causal = row >= col
    glane = lax.broadcasted_iota(jnp.int32, (L, gc_ref.shape[1]), 1)

    for c in range(n_chunks):
        rs = slice(c * L, (c + 1) * L)
        qc = q_ref[rs, :]
        kc = k_ref[rs, :]
        vc = v_ref[rs, :]
        gcol = gc_ref[rs, :]
        i_col = jnp.sum(jnp.where(glane == h, gcol, 0.0), axis=1, keepdims=True)
        f_col = jnp.sum(jnp.where(glane == heads + h, gcol, 0.0), axis=1, keepdims=True)
        i_row = gr_ref[0:1, rs]
        f_row = gr_ref[1:2, rs]
        lf_col = _log_sigmoid(f_col)
        lf_row = _log_sigmoid(f_row)
        b_col = jnp.sum(jnp.where(causal, lf_row, 0.0), axis=1, keepdims=True)
        b_row = jnp.sum(jnp.where(row <= col, lf_col, 0.0), axis=0, keepdims=True)
        b_last = b_col[L - 1:L, :]
        m_prev = m_ref[0:1, 0:1]

        d = jnp.where(causal, b_col - b_row + i_row, NEG_BIG)
        inter_log = b_col + m_prev
        m_i = jnp.maximum(inter_log, jnp.max(d, axis=1, keepdims=True))
        w_intra = jnp.exp(d - m_i)
        w_inter = jnp.exp(inter_log - m_i) * k_scale
        scores = _dot_nt(qc, kc) * k_scale * w_intra
        n_row = n_ref[...]
        den = (jnp.sum(scores, axis=1, keepdims=True)
               + w_inter * jnp.sum(qc.astype(F32) * n_row, axis=1, keepdims=True))
        inv = 1.0 / jnp.maximum(jnp.abs(den), jnp.exp(-m_i))
        scores_bf = scores.astype(BF16)

        log_wj_row = b_last - b_row + i_row
        log_wj_col = b_last - b_col + i_col
        m_new = jnp.maximum(b_last + m_prev, jnp.max(log_wj_row, axis=1, keepdims=True))
        wj_row = jnp.exp(log_wj_row - m_new)
        wj_col = jnp.exp(log_wj_col - m_new)
        dec = jnp.exp(b_last + m_prev - m_new)
        kw32 = kc.astype(F32) * wj_col
        kw = kw32.astype(BF16)

        for j in range(dv // dvb):
            cs = slice(j * dvb, (j + 1) * dvb)
            c_old = c_ref[:, cs]
            num = _dot(scores_bf, vc[:, cs]) + w_inter * _dot(qc, c_old.astype(BF16))
            hh_ref[:, cs] = num * inv
            c_ref[:, cs] = c_old * dec + _dot_tn(kw, vc[:, cs])
        n_ref[...] = n_row * dec + jnp.sum(kw32, axis=0, keepdims=True)
        m_ref[...] = jnp.broadcast_to(m_new, m_ref.shape)

        hh = hh_ref[...]
        mu = jnp.mean(hh, axis=-1, keepdims=True)
        hc = hh - mu
        var = jnp.mean(hc * hc, axis=-1, keepdims=True)
        hn = hc * lax.rsqrt(var + 1e-6) * ng_ref[...]
        out = (hn + sk_ref[...] * xc_ref[rs, :].astype(F32)) * _silu(z_ref[rs, :].astype(F32))
        o_ref[rs, :] = out.astype(o_ref.dtype)


def _mlstm_call(q, k, v, gates, gates_rows, xc, h_bf, norm_g, skip, *, heads, z_col, tb=2 * MLSTM_CHUNK):
    B, T, width = q.shape
    dh = width // heads
    tb = min(tb, T)
    assert tb % MLSTM_CHUNK == 0 and z_col % dh == 0
    dvb = min(512, dh)
    hs = pl.BlockSpec((None, tb, dh), lambda b, h, t: (b, t, h))
    kern = functools.partial(_mlstm_kernel, n_chunks=tb // MLSTM_CHUNK, heads=heads,
                             k_scale=dh ** -0.5, dvb=dvb)
    return pl.pallas_call(
        kern,
        grid=(B, heads, T // tb),
        in_specs=[hs, hs, hs,
                  pl.BlockSpec((None, tb, gates.shape[2]), lambda b, h, t: (b, t, 0)),
                  pl.BlockSpec((None, None, 2, tb), lambda b, h, t: (b, h, 0, t)),
                  hs,
                  pl.BlockSpec((None, tb, dh), lambda b, h, t: (b, t, z_col // dh + h)),
                  pl.BlockSpec((1, dh), lambda b, h, t: (0, h)),
                  pl.BlockSpec((1, dh), lambda b, h, t: (0, h))],
        out_specs=hs,
        out_shape=jax.ShapeDtypeStruct((B, T, width), BF16),
        scratch_shapes=[pltpu.VMEM((dh, dh), F32), pltpu.VMEM((MLSTM_CHUNK, dh), F32),
                        pltpu.VMEM((1, dh), F32), pltpu.VMEM((SUBLANES, LANES), F32)],
        compiler_params=_params("parallel", "parallel", "arbitrary"),
        name="mlstm_chunks",
    )(q, k, v, gates, gates_rows, xc, h_bf, norm_g, skip)


def _mem_kv(mem_bf, w_k, w_v, B):
    mw = w_k.shape[1]
    k = _matmul(mem_bf, w_k.astype(BF16), BF16).reshape(B, -1, mw)
    v = _matmul(mem_bf, w_v.astype(BF16), BF16).reshape(B, -1, mw)
    return k, v


def _even_layer(x, x_bf, emit_bf16, mem_bf, lb, w_in_stack, li, gla_w_a2, gla_b_a, gla_norm_g, hgrn_norm_g,
                mem_w_k, mem_w_v, w_out, ln_g, ln_b, alpha):
    B, T, D = x.shape
    lowrank, kw = gla_w_a2.shape
    gdv = gla_norm_g.shape[0]
    gw = GLA_HEADS * gdv
    gdk = kw // GLA_HEADS
    fw = lb.shape[0]
    hheads = fw // HGRN_EXPAND
    hdv = hgrn_norm_g.shape[0]
    hw = hheads * hdv
    mw = mem_w_k.shape[1]
    sizes = (kw, kw, gw, gw, lowrank, fw, fw, hw, hw, mw, mw)
    assert sum(sizes) == w_in_stack.shape[2]
    bounds = [0]
    for s in sizes:
        bounds.append(bounds[-1] + s)
    ga_w = 256
    w_a = _window_cast(w_in_stack, li, 0, bounds[4])
    w_b = _window_cast(w_in_stack, li, bounds[5], bounds[-1] - bounds[5])
    w_ga = jnp.pad(w_in_stack[li, :, bounds[4]:bounds[5]], ((0, 0), (0, ga_w - lowrank))).astype(BF16)
    offs_a = {"gq": 0, "gk": kw, "gv": 2 * kw, "gg": 2 * kw + gw}
    offs_b = {"hq": 0, "hi": fw, "hg": fw + hw, "mq": fw + 2 * hw, "mg": fw + 2 * hw + mw}

    x2d = x.reshape(B * T, D)
    if x_bf is None:
        h_ga, x_bf = _matmul_cast(x2d, w_ga, F32)
    else:
        h_ga = _matmul(x_bf, w_ga, F32)
    h_a = _matmul(x_bf, w_a, BF16)
    h_b = _matmul(x_bf, w_b, BF16, cols=[(0, fw), (2 * fw, bounds[-1] - bounds[5] - 2 * fw)])
    h_hf = _matmul(x_bf, w_b, F32, cols=[(fw, fw)])
    h_a, h_b, h_hf, h_ga = (h.reshape(B, T, -1) for h in (h_a, h_b, h_hf, h_ga))

    wa_pad = jnp.pad(gla_w_a2, ((0, ga_w - lowrank), (0, 0))).astype(BF16)
    gla_out = _gla_call(h_a, h_ga, wa_pad, gla_b_a.reshape(1, kw), gla_norm_g.reshape(1, gdv),
                        offs=offs_a, ga_col=0, dk=gdk, dv=gdv, tb=1024)
    hgrn_out = _hgrn_call(h_b, h_hf, lb.reshape(1, fw), hgrn_norm_g.reshape(1, hdv),
                          offs=offs_b, heads=hheads, dk=HGRN_EXPAND, dv=hdv, tb=1024)
    k_mem, v_mem = _mem_kv(mem_bf, mem_w_k, mem_w_v, B)
    mem_out = _memattn_call(h_b, k_mem, v_mem, q_col=offs_b["mq"], g_col=offs_b["mg"])

    acts = [a.reshape(B * T, -1) for a in (gla_out, hgrn_out, mem_out)]
    y, y_bf = _outproj_ln(acts, w_out.astype(BF16), x2d, ln_g.reshape(1, D), ln_b.reshape(1, D), alpha, emit_bf16)
    return y.reshape(B, T, D), y_bf


def _odd_layer(x, x_bf, emit_bf16, mem_bf, w_in, conv_w, conv_b, w_q, w_k, w_v, w_if, b_if, mh_norm_g, skip,
               mem_w_k, mem_w_v, w_out, ln_g, ln_b, alpha):
    B, T, D = x.shape
    width = conv_w.shape[1]
    heads = b_if.shape[0] // 2
    mw = mem_w_k.shape[1]
    assert w_in.shape[1] == 2 * width + 2 * mw

    x2d = x.reshape(B * T, D)
    if x_bf is None:
        x_bf = x2d.astype(BF16)
    h_bf = _matmul(x_bf, w_in.astype(BF16), BF16).reshape(B, T, -1)

    xc, q, k, v, gates = _mlstm_pre_call(h_bf, conv_w, conv_b, w_q, w_k, w_v, w_if, b_if, width=width)
    gates_rows = jnp.transpose(gates[:, :, :2 * heads].reshape(B, T, 2, heads), (0, 3, 2, 1))
    mlstm_out = _mlstm_call(q, k, v, gates, gates_rows, xc, h_bf,
                            mh_norm_g.reshape(1, width), skip.reshape(1, width), heads=heads, z_col=width)
    k_mem, v_mem = _mem_kv(mem_bf, mem_w_k, mem_w_v, B)
    mem_out = _memattn_call(h_bf, k_mem, v_mem, q_col=2 * width, g_col=2 * width + mw)

    acts = [a.reshape(B * T, -1) for a in (mlstm_out, mem_out)]
    y, y_bf = _outproj_ln(acts, w_out.astype(BF16), x2d, ln_g.reshape(1, D), ln_b.reshape(1, D), alpha, emit_bf16)
    return y.reshape(B, T, D), y_bf


def kernel(x, mem, hgrn_lb_logits, ev_w_in, ev_gla_w_a2, ev_gla_b_a, ev_gla_norm_g, ev_hgrn_norm_g, ev_mem_w_k, ev_mem_w_v, ev_w_out, ev_ln_g, ev_ln_b, od_w_in, od_conv_w, od_conv_b, od_w_q, od_w_k, od_w_v, od_w_if, od_b_if, od_mh_norm_g, od_skip, od_mem_w_k, od_mem_w_v, od_w_out, od_ln_g, od_ln_b):
    depth = ev_w_in.shape[0] + od_w_in.shape[0]
    alpha = (2 * depth) ** 0.25
    B, ml, D = mem.shape
    mem_bf = mem.reshape(B * ml, D).astype(BF16)
    lb_all = jnp.cumsum(jax.nn.softmax(hgrn_lb_logits.astype(F32), axis=0), axis=0)
    x_bf = None
    for layer in range(depth):
        i = layer // 2
        emit_bf16 = layer + 1 < depth
        if layer % 2 == 0:
            x, x_bf = _even_layer(x, x_bf, emit_bf16, mem_bf, lb_all[layer], ev_w_in, i, ev_gla_w_a2[i],
                                  ev_gla_b_a[i], ev_gla_norm_g[i], ev_hgrn_norm_g[i], ev_mem_w_k[i],
                                  ev_mem_w_v[i], ev_w_out[i], ev_ln_g[i], ev_ln_b[i], alpha)
        else:
            x, x_bf = _odd_layer(x, x_bf, emit_bf16, mem_bf, od_w_in[i], od_conv_w[i], od_conv_b[i], od_w_q[i],
                                 od_w_k[i], od_w_v[i], od_w_if[i], od_b_if[i], od_mh_norm_g[i], od_skip[i],
                                 od_mem_w_k[i], od_mem_w_v[i], od_w_out[i], od_ln_g[i], od_ln_b[i], alpha)
    return x
```

```python
import functools

import jax
import jax.numpy as jnp
import numpy as np
from jax import lax
from jax.experimental import pallas as pl
from jax.experimental.pallas import tpu as pltpu

F32 = jnp.float32
BF16 = jnp.bfloat16

GLA_HEADS = 4
GLA_TAU = 16.0
HGRN_EXPAND = 128
MEM_HEADS = 4

LIN_CHUNK = 128
MLSTM_CHUNK = 256

V7X_VMEM_LIMIT_BYTES = 56 * 1024 * 1024
V7X_MXU_WIDTH = 256
LANES = 128
SUBLANES = 8
NEG_BIG = -1e30
LOG2_E = 1.4426950408889634


def _sigmoid(x):
    return 1.0 / (1.0 + jnp.exp(-x))


def _silu(x):
    return x * _sigmoid(x)


def _log_sigmoid(x):
    return jnp.minimum(x, 0.0) - jnp.log(1.0 + jnp.exp(-jnp.abs(x)))


def _dot(a, b):
    return jnp.dot(a, b, preferred_element_type=F32)


def _dot_nt(a, b):
    return lax.dot_general(a, b, (((1,), (1,)), ((), ())), preferred_element_type=F32)


def _dot_tn(a, b):
    return lax.dot_general(a, b, (((0,), (0,)), ((), ())), preferred_element_type=F32)


def _params(*sem):
    return pltpu.CompilerParams(dimension_semantics=sem, vmem_limit_bytes=V7X_VMEM_LIMIT_BYTES)


def _mm_kernel(a_ref, w_ref, o_ref):
    o_ref[...] = _dot(a_ref[...], w_ref[...]).astype(o_ref.dtype)


def _matmul(a, w, out_dtype, tm=1024, tn=1024, cols=None):
    m, k = a.shape
    cols = [(0, w.shape[1])] if cols is None else list(cols)
    tm = min(tm, m)
    while any(v % tn for rng in cols for v in rng):
        tn //= 2
    assert m % tm == 0 and tn % LANES == 0, (m, cols, tm, tn)
    n_out = sum(n for _, n in cols)

    def w_block(i, j):
        blk, done = None, 0
        for first, n in cols:
            here = first // tn + (j - done)
            blk = here if blk is None else jnp.where(j >= done, here, blk)
            done += n // tn
        return 0, blk

    return pl.pallas_call(
        _mm_kernel,
        grid=(m // tm, n_out // tn),
        in_specs=[pl.BlockSpec((tm, k), lambda i, j: (i, 0)),
                  pl.BlockSpec((k, tn), w_block)],
        out_specs=pl.BlockSpec((tm, tn), lambda i, j: (i, j)),
        out_shape=jax.ShapeDtypeStruct((m, n_out), out_dtype),
        compiler_params=_params("parallel", "arbitrary"),
        name="proj_matmul",
    )(a, w)


def _mm_cast_kernel(a_ref, w_ref, o_ref, abf_ref):
    a_bf = a_ref[...].astype(abf_ref.dtype)
    abf_ref[...] = a_bf
    o_ref[...] = _dot(a_bf, w_ref[...]).astype(o_ref.dtype)


def _matmul_cast(a, w, out_dtype, tm=512):
    m, k = a.shape
    n = w.shape[1]
    tm = min(tm, m)
    assert m % tm == 0 and n % LANES == 0
    return pl.pallas_call(
        _mm_cast_kernel,
        grid=(m // tm,),
        in_specs=[pl.BlockSpec((tm, k), lambda i: (i, 0)), pl.BlockSpec((k, n), lambda i: (0, 0))],
        out_specs=[pl.BlockSpec((tm, n), lambda i: (i, 0)), pl.BlockSpec((tm, k), lambda i: (i, 0))],
        out_shape=[jax.ShapeDtypeStruct((m, n), out_dtype), jax.ShapeDtypeStruct((m, k), BF16)],
        compiler_params=_params("parallel"),
        name="proj_matmul_cast",
    )(a, w)


def _window_cast_kernel(x_ref, o_ref):
    o_ref[...] = x_ref[0].T.astype(o_ref.dtype)


def _window_cast(w_stack, layer, col0, ncols, tr=1024, tkb=1024):
    wt = jnp.swapaxes(w_stack, 1, 2)
    _, n, k = wt.shape
    tkb = min(tkb, k)
    while ncols % tr:
        tr //= 2
    assert col0 % SUBLANES == 0 and tr % LANES == 0 and k % tkb == 0 and col0 + ncols <= n, (col0, ncols, tr)
    return pl.pallas_call(
        _window_cast_kernel,
        grid=(ncols // tr, k // tkb),
        in_specs=[pl.BlockSpec((pl.Element(1), pl.Element(tr), pl.Element(tkb)),
                               lambda i, j: (layer, pl.multiple_of(col0 + i * tr, SUBLANES),
                                             pl.multiple_of(j * tkb, LANES)))],
        out_specs=pl.BlockSpec((tkb, tr), lambda i, j: (j, i)),
        out_shape=jax.ShapeDtypeStruct((k, ncols), BF16),
        compiler_params=_params("parallel", "parallel"),
        name="weight_window_cast",
    )(wt)


def _lin_constants(L):
    i = np.arange(L)[:, None]
    t = np.arange(L)[None, :]
    sums = [t <= i]
    masks = [i == t]
    m = L // 2
    while m >= 1:
        r = (i // (2 * m)) * 2 * m + m - 1
        second = (i & m) != 0
        sums.append(np.where(second, (t > r) & (t <= i), (t > i) & (t <= r)))
        masks.append((i // (2 * m) == t // (2 * m)) & second & ((t & m) == 0))
        m //= 2
    sums.append(t > i)
    s = np.concatenate(sums, axis=0).astype(np.float32)
    return (jnp.asarray(np.concatenate([s, s], axis=1), BF16),
            jnp.asarray(np.stack(masks).astype(np.float32)))


def _block_rows(x, grp, d):
    if grp == 1:
        return x
    lane = lax.broadcasted_iota(jnp.int32, x.shape, 1)
    zero = jnp.zeros_like(x)
    return jnp.concatenate([jnp.where((lane >= r * d) & (lane < (r + 1) * d), x, zero) for r in range(grp)], axis=0)


def _lin_attn_chunk(q, k, v_bf, g, st_ref, sums_ref, mask_ref, nh):
    L, wk = q.shape
    dk = wk // nh
    dv = v_bf.shape[1] // nh
    grp = st_ref.shape[1] // dv
    n_lvl = mask_ref.shape[0] - 1
    g2 = g * LOG2_E
    hi = g2.astype(BF16)
    lo = (g2 - hi.astype(F32)).astype(BF16)
    p = jnp.exp2(_dot(sums_ref[...], jnp.concatenate([hi, lo], axis=0)))
    p_last = p[L - 1:L, :]
    qd = (q * p[0:L]).astype(BF16)
    kd = (k * p[(n_lvl + 1) * L:(n_lvl + 2) * L]).astype(BF16)
    q_bf = q.astype(BF16)
    k_bf = k.astype(BF16)
    rowi = lax.broadcasted_iota(jnp.int32, (L, 1), 0)
    xs = []
    for lvl in range(n_lvl):
        second = (rowi & (L >> (lvl + 1))) != 0
        xs.append((jnp.where(second, q, k) * p[(lvl + 1) * L:(lvl + 2) * L]).astype(BF16))
    def mask(l):
        m = mask_ref[l]
        return m if grp == 1 else jnp.concatenate([m] * grp, axis=1)

    outs = []
    for h in range(nh // grp):
        ck = slice(h * grp * dk, (h + 1) * grp * dk)
        cv = slice(h * grp * dv, (h + 1) * grp * dv)
        a = _dot_nt(q_bf[:, ck], _block_rows(k_bf[:, ck], grp, dk)) * mask(0)
        for lvl in range(n_lvl):
            x = xs[lvl][:, ck]
            a = a + _dot_nt(x, _block_rows(x, grp, dk)) * mask(lvl + 1)
        st = st_ref[h]
        v_blk = _block_rows(v_bf[:, cv], grp, dv)
        o = _dot_nt(qd[:, ck], st.astype(BF16)) + _dot(a.astype(BF16), v_blk)
        st_ref[h] = st * p_last[:, ck] + _dot_tn(v_blk, _block_rows(kd[:, ck], grp, dk))
        outs.extend(o[:, r * dv:(r + 1) * dv] for r in range(grp))
    return outs


def _rms_gate(o, norm_g, gate, eps=1e-6):
    ms = jnp.mean(o * o, axis=-1, keepdims=True)
    return o * lax.rsqrt(ms + eps) * norm_g * _silu(gate)


def _gla_kernel(q_ref, k_ref, v_ref, gg_ref, ga_ref, wa_ref, ba_ref, ng_ref, sums_ref, mask_ref, o_ref,
                st_ref, *, n_chunks, q_scale):
    @pl.when(pl.program_id(2) == 0)
    def _():
        st_ref[...] = jnp.zeros_like(st_ref)

    def chunk(c, carry):
        rows = pl.ds(pl.multiple_of(c * LIN_CHUNK, LIN_CHUNK), LIN_CHUNK)
        q = q_ref[rows, :].astype(F32) * q_scale
        k = k_ref[rows, :].astype(F32)
        a_pre = _dot(ga_ref[rows, :].astype(BF16), wa_ref[...]) + ba_ref[...]
        g = _log_sigmoid(a_pre) * (1.0 / GLA_TAU)
        (o,) = _lin_attn_chunk(q, k, v_ref[rows, :], g, st_ref, sums_ref, mask_ref, 1)
        o_ref[rows, :] = _rms_gate(o, ng_ref[...], gg_ref[rows, :].astype(F32)).astype(o_ref.dtype)
        return carry

    lax.fori_loop(0, n_chunks, chunk, 0, unroll=4)


def _hgrn_kernel(q_ref, f_ref, v_ref, hg_ref, lb_ref, ng_ref, sums_ref, mask_ref, o_ref,
                 st_ref, *, n_chunks, nh):
    dv = ng_ref.shape[1]

    @pl.when(pl.program_id(2) == 0)
    def _():
        st_ref[...] = jnp.zeros_like(st_ref)

    def chunk(c, carry):
        rows = pl.ds(pl.multiple_of(c * LIN_CHUNK, LIN_CHUNK), LIN_CHUNK)
        lb = lb_ref[...]
        f = lb + (1.0 - lb) * _sigmoid(f_ref[rows, :])
        q = _silu(q_ref[rows, :].astype(F32))
        outs = _lin_attn_chunk(q, 1.0 - f, v_ref[rows, :], jnp.log(f), st_ref, sums_ref, mask_ref, nh)
        gate = hg_ref[rows, :].astype(F32)
        for h, o in enumerate(outs):
            cv = slice(h * dv, (h + 1) * dv)
            o_ref[rows, cv] = _rms_gate(o, ng_ref[...], gate[:, cv]).astype(o_ref.dtype)
        return carry

    lax.fori_loop(0, n_chunks, chunk, 0, unroll=4)


def _col_spec(tb, width, col0):
    assert col0 % width == 0, (col0, width)
    base = col0 // width
    return pl.BlockSpec((None, tb, width), lambda b, h, t: (b, t, base + h))


def _const_specs(*arrays):
    return [pl.BlockSpec(a.shape, lambda b, h, t, nd=a.ndim: (0,) * nd) for a in arrays]


def _gla_call(h_bf, h_f, wa_pad, b_a, norm_g, *, offs, ga_col, dk, dv, tb):
    B, T, _ = h_bf.shape
    tb = min(tb, T)
    ga_w = wa_pad.shape[0]
    assert ga_col % ga_w == 0
    consts = _lin_constants(LIN_CHUNK)
    kern = functools.partial(_gla_kernel, n_chunks=tb // LIN_CHUNK, q_scale=dk ** -0.5)
    return pl.pallas_call(
        kern,
        grid=(B, GLA_HEADS, T // tb),
        in_specs=[_col_spec(tb, dk, offs["gq"]), _col_spec(tb, dk, offs["gk"]),
                  _col_spec(tb, dv, offs["gv"]), _col_spec(tb, dv, offs["gg"]),
                  pl.BlockSpec((None, tb, ga_w), lambda b, h, t: (b, t, ga_col // ga_w)),
                  pl.BlockSpec((ga_w, dk), lambda b, h, t: (0, h)),
                  pl.BlockSpec((1, dk), lambda b, h, t: (0, h)),
                  pl.BlockSpec((1, dv), lambda b, h, t: (0, 0))] + _const_specs(*consts),
        out_specs=pl.BlockSpec((None, tb, dv), lambda b, h, t: (b, t, h)),
        out_shape=jax.ShapeDtypeStruct((B, T, GLA_HEADS * dv), BF16),
        scratch_shapes=[pltpu.VMEM((1, dv, dk), F32)],
        compiler_params=_params("parallel", "parallel", "arbitrary"),
        name="gla_chunks",
    )(h_bf, h_bf, h_bf, h_bf, h_f, wa_pad, b_a, norm_g, *consts)


def _hgrn_call(h_bf, h_f, lb, norm_g, *, offs, heads, dk, dv, tb, nh=4):
    B, T, _ = h_bf.shape
    tb = min(tb, T)
    nh = min(nh, heads)
    assert heads % nh == 0
    grp = 2 if (nh % 2 == 0 and dk == dv and 2 * dk == V7X_MXU_WIDTH) else 1
    consts = _lin_constants(LIN_CHUNK)
    kern = functools.partial(_hgrn_kernel, n_chunks=tb // LIN_CHUNK, nh=nh)
    return pl.pallas_call(
        kern,
        grid=(B, heads // nh, T // tb),
        in_specs=[_col_spec(tb, nh * dk, offs["hq"]),
                  pl.BlockSpec((None, tb, nh * dk), lambda b, h, t: (b, t, h)),
                  _col_spec(tb, nh * dv, offs["hi"]), _col_spec(tb, nh * dv, offs["hg"]),
                  pl.BlockSpec((1, nh * dk), lambda b, h, t: (0, h)),
                  pl.BlockSpec((1, dv), lambda b, h, t: (0, 0))] + _const_specs(*consts),
        out_specs=pl.BlockSpec((None, tb, nh * dv), lambda b, h, t: (b, t, h)),
        out_shape=jax.ShapeDtypeStruct((B, T, heads * dv), BF16),
        scratch_shapes=[pltpu.VMEM((nh // grp, grp * dv, grp * dk), F32)],
        compiler_params=_params("parallel", "parallel", "arbitrary"),
        name="hgrn_chunks",
    )(h_bf, h_f, h_bf, h_bf, lb, norm_g, *consts)


def _memattn_kernel(q_ref, g_ref, k_ref, v_ref, o_ref, *, dh):
    scale = dh ** -0.5
    for h in range(MEM_HEADS):
        cs = slice(h * dh, (h + 1) * dh)
        s = _dot_nt(q_ref[:, cs], k_ref[:, cs]) * scale
        s = s - jnp.max(s, axis=-1, keepdims=True)
        p = jnp.exp(s)
        p = p / jnp.sum(p, axis=-1, keepdims=True)
        o = _dot(p.astype(BF16), v_ref[:, cs])
        o_ref[:, cs] = (o * _silu(g_ref[:, cs].astype(F32))).astype(o_ref.dtype)


def _memattn_call(h_bf, k_mem, v_mem, *, q_col, g_col, tm=512):
    B, T, _ = h_bf.shape
    _, ml, mw = k_mem.shape
    tm = min(tm, T)
    assert q_col % mw == 0 and g_col % mw == 0
    return pl.pallas_call(
        functools.partial(_memattn_kernel, dh=mw // MEM_HEADS),
        grid=(B, T // tm),
        in_specs=[pl.BlockSpec((None, tm, mw), lambda b, t: (b, t, q_col // mw)),
                  pl.BlockSpec((None, tm, mw), lambda b, t: (b, t, g_col // mw)),
                  pl.BlockSpec((None, ml, mw), lambda b, t: (b, 0, 0)),
                  pl.BlockSpec((None, ml, mw), lambda b, t: (b, 0, 0))],
        out_specs=pl.BlockSpec((None, tm, mw), lambda b, t: (b, t, 0)),
        out_shape=jax.ShapeDtypeStruct((B, T, mw), BF16),
        compiler_params=_params("parallel", "parallel"),
        name="mem_attention",
    )(h_bf, h_bf, k_mem, v_mem)


def _outproj_kernel(*refs, segs, nk, n_blocks, alpha, eps, emit_bf16):
    ns = len(segs)
    a_refs = refs[:ns]
    if emit_bf16:
        w_ref, x_hbm, g_ref, b_ref, o_hbm, obf_hbm, acc, x_sem, o_sem, bf_buf, bf_sem = refs[ns:]
    else:
        w_ref, x_hbm, g_ref, b_ref, o_hbm, acc, x_sem, o_sem = refs[ns:]
    i = pl.program_id(0)
    k = pl.program_id(1)
    _, tm, d = acc.shape
    rb, cb = min(128, tm), min(512, d)
    n_rb = tm // rb
    slot = i % 2
    o_ref = acc.at[slot]

    def rows_of(blk):
        return pl.ds(pl.multiple_of(blk * tm, tm), tm)

    def x_copy(blk, s):
        return pltpu.make_async_copy(x_hbm.at[rows_of(blk), :], acc.at[s], x_sem.at[s])

    def o_copy(blk, s):
        return pltpu.make_async_copy(acc.at[s], o_hbm.at[rows_of(blk), :], o_sem.at[s])

    def bf_copy(r, bslot):
        row0 = pl.multiple_of(i * tm + r * rb, rb)
        return pltpu.make_async_copy(bf_buf.at[bslot], obf_hbm.at[pl.ds(row0, rb), :], bf_sem.at[bslot])

    @pl.when((i == 0) & (k == 0))
    def _():
        x_copy(0, 0).start()

    @pl.when(k == 0)
    def _():
        x_copy(i, slot).wait()
        a = a_refs[0][...]
        for c in range(0, d, cb):
            o_ref[:, c:c + cb] = alpha * o_ref[:, c:c + cb] + _dot(a, w_ref[:, c:c + cb])

    for a_ref, (start, n) in zip(a_refs, segs):
        @pl.when((k >= max(start, 1)) & (k < start + n))
        def _(a_ref=a_ref):
            a = a_ref[...]
            for c in range(0, d, cb):
                o_ref[:, c:c + cb] += _dot(a, w_ref[:, c:c + cb])

    @pl.when(k == nk - 1)
    def _():
        @pl.when(i >= 1)
        def _():
            o_copy(i - 1, 1 - slot).wait()

        @pl.when(i + 1 < n_blocks)
        def _():
            x_copy(i + 1, 1 - slot).start()

        def norm(r, carry):
            rows = pl.ds(pl.multiple_of(r * rb, rb), rb)
            y = o_ref[rows, :]
            mu = jnp.mean(y, axis=-1, keepdims=True)
            yc = y - mu
            var = jnp.mean(yc * yc, axis=-1, keepdims=True)
            out = yc * lax.rsqrt(var + eps) * g_ref[...] + b_ref[...]
            o_ref[rows, :] = out
            if emit_bf16:
                n_slot = bf_buf.shape[0]
                bslot = r % n_slot

                @pl.when(r >= n_slot)
                def _():
                    bf_copy(r - n_slot, bslot).wait()

                bf_buf[bslot] = out.astype(bf_buf.dtype)
                bf_copy(r, bslot).start()
            return carry

        lax.fori_loop(0, n_rb, norm, 0)
        o_copy(i, slot).start()
        if emit_bf16:
            for r in range(max(n_rb - bf_buf.shape[0], 0), n_rb):
                bf_copy(r, r % bf_buf.shape[0]).wait()

        @pl.when(i == n_blocks - 1)
        def _():
            o_copy(i, slot).wait()


def _outproj_ln(acts, w_bf, x2d, ln_g, ln_b, alpha, emit_bf16, tm=1024, tk=512):
    m, d = x2d.shape
    tm = min(tm, m)
    segs, start = [], 0
    for a in acts:
        assert a.shape[0] == m and a.shape[1] % tk == 0
        segs.append((start, a.shape[1] // tk))
        start += a.shape[1] // tk
    nk = start
    assert nk * tk == w_bf.shape[0]

    def a_spec(s0, n):
        return pl.BlockSpec((tm, tk), lambda i, k: (i, jnp.clip(k - s0, 0, n - 1)))

    kern = functools.partial(_outproj_kernel, segs=tuple(segs), nk=nk, n_blocks=m // tm, alpha=alpha, eps=1e-5,
                             emit_bf16=emit_bf16)
    rb = min(128, tm)
    out_specs = [pl.BlockSpec(memory_space=pl.ANY)]
    out_shape = [jax.ShapeDtypeStruct((m, d), F32)]
    scratch = [pltpu.VMEM((2, tm, d), F32), pltpu.SemaphoreType.DMA((2,)), pltpu.SemaphoreType.DMA((2,))]
    if emit_bf16:
        out_specs.append(pl.BlockSpec(memory_space=pl.ANY))
        out_shape.append(jax.ShapeDtypeStruct((m, d), BF16))
        n_slot = 4
        scratch += [pltpu.VMEM((n_slot, rb, d), BF16), pltpu.SemaphoreType.DMA((n_slot,))]
    outs = pl.pallas_call(
        kern,
        grid=(m // tm, nk),
        in_specs=[a_spec(s0, n) for s0, n in segs] + [
            pl.BlockSpec((tk, d), lambda i, k: (k, 0)),
            pl.BlockSpec(memory_space=pl.ANY),
            pl.BlockSpec((1, d), lambda i, k: (0, 0)),
            pl.BlockSpec((1, d), lambda i, k: (0, 0))],
        out_specs=out_specs,
        out_shape=out_shape,
        scratch_shapes=scratch,
        compiler_params=_params("arbitrary", "arbitrary"),
        name="outproj_layernorm",
    )(*acts, w_bf, x2d, ln_g, ln_b)
    return (outs[0], outs[1]) if emit_bf16 else (outs[0], None)


def _mlstm_pre_kernel(x_ref, halo_ref, cw_ref, cb_ref, wq_ref, wk_ref, wv_ref,
                      iq_ref, ik_ref, iv_ref, bif_ref,
                      xc_ref, q_ref, k_ref, v_ref, g_ref, *, n_sub, bw, n_taps):
    t = pl.program_id(1)
    c = pl.program_id(2)
    tm = x_ref.shape[0]

    @pl.when(c == 0)
    def _():
        g_ref[...] = jnp.broadcast_to(bif_ref[...], g_ref.shape)

    gates = jnp.zeros(g_ref.shape, F32)
    for j in range(n_sub):
        cs = slice(j * bw, (j + 1) * bw)
        x_bf = x_ref[:, cs]
        x32 = x_bf.astype(F32)
        halo = jnp.where(t > 0, halo_ref[:, cs].astype(F32), 0.0)
        ext = jnp.concatenate([halo, x32], axis=0)
        prev = pltpu.roll(ext, 1, axis=0)
        acc = jnp.broadcast_to(cb_ref[:, cs], (tm, bw))
        for p in range(n_taps // 2):
            tap = n_taps - 1 - 2 * p
            pair = ext * cw_ref[tap:tap + 1, cs] + prev * cw_ref[tap - 1:tap, cs]
            acc = acc + pair[SUBLANES - 2 * p:SUBLANES - 2 * p + tm, :]
        xc_bf = _silu(acc).astype(BF16)
        xc_ref[:, cs] = xc_bf
        q_bf = _dot(xc_bf, wq_ref[j]).astype(BF16)
        k_bf = _dot(xc_bf, wk_ref[j]).astype(BF16)
        v_bf = _dot(x_bf, wv_ref[j]).astype(BF16)
        q_ref[:, cs] = q_bf
        k_ref[:, cs] = k_bf
        v_ref[:, cs] = v_bf
        gates = gates + _dot(q_bf, iq_ref[cs, :]) + _dot(k_bf, ik_ref[cs, :]) + _dot(v_bf, iv_ref[cs, :])
    g_ref[...] += gates


def _block_diag(w, bw):
    nb, bi, bo = w.shape
    assert bi == bo and bw % bi == 0 and (nb * bi) % bw == 0
    per = bw // bi
    w4 = w.reshape(nb // per, per, bi, bo)
    eye = jnp.eye(per, dtype=w.dtype)
    return jnp.einsum("cnio,nm->cnimo", w4, eye).reshape(nb // per, bw, bw)


def _mlstm_pre_call(h_bf, conv_w, conv_b, w_q, w_k, w_v, w_if, b_if, *, width, tm=1024, tc=2048, bw=256):
    B, T, _ = h_bf.shape
    tm, tc = min(tm, T), min(tc, width)
    n_taps = conv_w.shape[0]
    assert tm % SUBLANES == 0 and n_taps <= SUBLANES and n_taps % 2 == 0 and width % tc == 0 and tc % bw == 0
    n_sub = tc // bw
    wq = _block_diag(w_q, bw).astype(BF16)
    wk = _block_diag(w_k, bw).astype(BF16)
    wv = _block_diag(w_v, bw).astype(BF16)
    n_gate = w_if.shape[1]
    wif = jnp.pad(w_if, ((0, 0), (0, LANES - n_gate))).astype(BF16)
    bif = jnp.pad(b_if.astype(F32), (0, LANES - n_gate)).reshape(1, LANES)
    hb = tm // SUBLANES
    tile_spec = pl.BlockSpec((n_sub, bw, bw), lambda b, t, c: (c, 0, 0))
    act_spec = pl.BlockSpec((None, tm, tc), lambda b, t, c: (b, t, c))
    kern = functools.partial(_mlstm_pre_kernel, n_sub=n_sub, bw=bw, n_taps=n_taps)
    shp = jax.ShapeDtypeStruct((B, T, width), BF16)
    return pl.pallas_call(
        kern,
        grid=(B, T // tm, width // tc),
        in_specs=[act_spec,
                  pl.BlockSpec((None, SUBLANES, tc), lambda b, t, c: (b, jnp.maximum(t * hb - 1, 0), c)),
                  pl.BlockSpec((n_taps, tc), lambda b, t, c: (0, c)),
                  pl.BlockSpec((1, tc), lambda b, t, c: (0, c)),
                  tile_spec, tile_spec, tile_spec,
                  pl.BlockSpec((tc, LANES), lambda b, t, c: (c, 0)),
                  pl.BlockSpec((tc, LANES), lambda b, t, c: (width // tc + c, 0)),
                  pl.BlockSpec((tc, LANES), lambda b, t, c: (2 * (width // tc) + c, 0)),
                  pl.BlockSpec((1, LANES), lambda b, t, c: (0, 0))],
        out_specs=[act_spec, act_spec, act_spec, act_spec,
                   pl.BlockSpec((None, tm, LANES), lambda b, t, c: (b, t, 0))],
        out_shape=[shp, shp, shp, shp,
                   jax.ShapeDtypeStruct((B, T, LANES), F32)],
        compiler_params=_params("parallel", "parallel", "arbitrary"),
        name="mlstm_frontend",
    )(h_bf, h_bf, conv_w, conv_b.reshape(1, width), wq, wk, wv, wif, wif, wif, bif)


def _mlstm_kernel(q_ref, k_ref, v_ref, gc_ref, gr_ref, xc_ref, z_ref, ng_ref, sk_ref, o_ref,
                  c_ref, hh_ref, n_ref, m_ref, *, n_chunks, heads, k_scale, dvb):
    h = pl.program_id(1)
    L = MLSTM_CHUNK
    dk, dv = c_ref.shape

    @pl.when(pl.program_id(2) == 0)
    def _():
        c_ref[...] = jnp.zeros_like(c_ref)
        n_ref[...] = jnp.zeros_like(n_ref)
        m_ref[...] = jnp.zeros_like(m_ref)

    row = lax.broadcasted_iota(jnp.int32, (L, L), 0)
    col = lax.broadcasted_iota(jnp.int32, (L, L), 1)
    causal = row >= col
    glane = lax.broadcasted_iota(jnp.int32, (L, gc_ref.shape[1]), 1)

    for c in range(n_chunks):
        rs = slice(c * L, (c + 1) * L)
        qc = q_ref[rs, :]
        kc = k_ref[rs, :]
        vc = v_ref[rs, :]
        gcol = gc_ref[rs, :]
        i_col = jnp.sum(jnp.where(glane == h, gcol, 0.0), axis=1, keepdims=True)
        f_col = jnp.sum(jnp.where(glane == heads + h, gcol, 0.0), axis=1, keepdims=True)
        i_row = gr_ref[0:1, rs]
        f_row = gr_ref[1:2, rs]
        lf_col = _log_sigmoid(f_col)
        lf_row = _log_sigmoid(f_row)
        b_col = jnp.sum(jnp.where(causal, lf_row, 0.0), axis=1, keepdims=True)
        b_row = jnp.sum(jnp.where(row <= col, lf_col, 0.0), axis=0, keepdims=True)
        b_last = b_col[L - 1:L, :]
        m_prev = m_ref[0:1, 0:1]

        d = jnp.where(causal, b_col - b_row + i_row, NEG_BIG)
        inter_log = b_col + m_prev
        m_i = jnp.maximum(inter_log, jnp.max(d, axis=1, keepdims=True))
        w_intra = jnp.exp(d - m_i)
        w_inter = jnp.exp(inter_log - m_i) * k_scale
        scores = _dot_nt(qc, kc) * k_scale * w_intra
        n_row = n_ref[...]
        den = (jnp.sum(scores, axis=1, keepdims=True)
               + w_inter * jnp.sum(qc.astype(F32) * n_row, axis=1, keepdims=True))
        inv = 1.0 / jnp.maximum(jnp.abs(den), jnp.exp(-m_i))
        scores_bf = scores.astype(BF16)

        log_wj_row = b_last - b_row + i_row
        log_wj_col = b_last - b_col + i_col
        m_new = jnp.maximum(b_last + m_prev, jnp.max(log_wj_row, axis=1, keepdims=True))
        wj_row = jnp.exp(log_wj_row - m_new)
        wj_col = jnp.exp(log_wj_col - m_new)
        dec = jnp.exp(b_last + m_prev - m_new)
        kw32 = kc.astype(F32) * wj_col
        kw = kw32.astype(BF16)

        for j in range(dv // dvb):
            cs = slice(j * dvb, (j + 1) * dvb)
            c_old = c_ref[:, cs]
            num = _dot(scores_bf, vc[:, cs]) + w_inter * _dot(qc, c_old.astype(BF16))
            hh_ref[:, cs] = num * inv
            c_ref[:, cs] = c_old * dec + _dot_tn(kw, vc[:, cs])
        n_ref[...] = n_row * dec + jnp.sum(kw32, axis=0, keepdims=True)
        m_ref[...] = jnp.broadcast_to(m_new, m_ref.shape)

        hh = hh_ref[...]
        mu = jnp.mean(hh, axis=-1, keepdims=True)
        hc = hh - mu
        var = jnp.mean(hc * hc, axis=-1, keepdims=True)
        hn = hc * lax.rsqrt(var + 1e-6) * ng_ref[...]
        out = (hn + sk_ref[...] * xc_ref[rs, :].astype(F32)) * _silu(z_ref[rs, :].astype(F32))
        o_ref[rs, :] = out.astype(o_ref.dtype)


def _mlstm_call(q, k, v, gates, gates_rows, xc, h_bf, norm_g, skip, *, heads, z_col, tb=2 * MLSTM_CHUNK):
    B, T, width = q.shape
    dh = width // heads
    tb = min(tb, T)
    assert tb % MLSTM_CHUNK == 0 and z_col % dh == 0
    dvb = min(512, dh)
    hs = pl.BlockSpec((None, tb, dh), lambda b, h, t: (b, t, h))
    kern = functools.partial(_mlstm_kernel, n_chunks=tb // MLSTM_CHUNK, heads=heads,
                             k_scale=dh ** -0.5, dvb=dvb)
    return pl.pallas_call(
        kern,
        grid=(B, heads, T // tb),
        in_specs=[hs, hs, hs,
                  pl.BlockSpec((None, tb, gates.shape[2]), lambda b, h, t: (b, t, 0)),
                  pl.BlockSpec((None, None, 2, tb), lambda b, h, t: (b, h, 0, t)),
                  hs,
                  pl.BlockSpec((None, tb, dh), lambda b, h, t: (b, t, z_col // dh + h)),
                  pl.BlockSpec((1, dh), lambda b, h, t: (0, h)),
                  pl.BlockSpec((1, dh), lambda b, h, t: (0, h))],
        out_specs=hs,
        out_shape=jax.ShapeDtypeStruct((B, T, width), BF16),
        scratch_shapes=[pltpu.VMEM((dh, dh), F32), pltpu.VMEM((MLSTM_CHUNK, dh), F32),
                        pltpu.VMEM((1, dh), F32), pltpu.VMEM((SUBLANES, LANES), F32)],
        compiler_params=_params("parallel", "parallel", "arbitrary"),
        name="mlstm_chunks",
    )(q, k, v, gates, gates_rows, xc, h_bf, norm_g, skip)


def _mem_kv(mem_bf, w_k, w_v, B):
    mw = w_k.shape[1]
    k = _matmul(mem_bf, w_k.astype(BF16), BF16).reshape(B, -1, mw)
    v = _matmul(mem_bf, w_v.astype(BF16), BF16).reshape(B, -1, mw)
    return k, v


def _even_layer(x, x_bf, emit_bf16, mem_bf, lb, w_in_stack, li, gla_w_a2, gla_b_a, gla_norm_g, hgrn_norm_g,
                mem_w_k, mem_w_v, w_out, ln_g, ln_b, alpha):
    B, T, D = x.shape
    lowrank, kw = gla_w_a2.shape
    gdv = gla_norm_g.shape[0]
    gw = GLA_HEADS * gdv
    gdk = kw // GLA_HEADS
    fw = lb.shape[0]
    hheads = fw // HGRN_EXPAND
    hdv = hgrn_norm_g.shape[0]
    hw = hheads * hdv
    mw = mem_w_k.shape[1]
    sizes = (kw, kw, gw, gw, lowrank, fw, fw, hw, hw, mw, mw)
    assert sum(sizes) == w_in_stack.shape[2]
    bounds = [0]
    for s in sizes:
        bounds.append(bounds[-1] + s)
    ga_w = 256
    w_a = _window_cast(w_in_stack, li, 0, bounds[4])
    w_b = _window_cast(w_in_stack, li, bounds[5], bounds[-1] - bounds[5])
    w_ga = jnp.pad(w_in_stack[li, :, bounds[4]:bounds[5]], ((0, 0), (0, ga_w - lowrank))).astype(BF16)
    offs_a = {"gq": 0, "gk": kw, "gv": 2 * kw, "gg": 2 * kw + gw}
    offs_b = {"hq": 0, "hi": fw, "hg": fw + hw, "mq": fw + 2 * hw, "mg": fw + 2 * hw + mw}

    x2d = x.reshape(B * T, D)
    if x_bf is None:
        h_ga, x_bf = _matmul_cast(x2d, w_ga, F32)
    else:
        h_ga = _matmul(x_bf, w_ga, F32)
    h_a = _matmul(x_bf, w_a, BF16)
    h_b = _matmul(x_bf, w_b, BF16, cols=[(0, fw), (2 * fw, bounds[-1] - bounds[5] - 2 * fw)])
    h_hf = _matmul(x_bf, w_b, F32, cols=[(fw, fw)])
    h_a, h_b, h_hf, h_ga = (h.reshape(B, T, -1) for h in (h_a, h_b, h_hf, h_ga))

    wa_pad = jnp.pad(gla_w_a2, ((0, ga_w - lowrank), (0, 0))).astype(BF16)
    gla_out = _gla_call(h_a, h_ga, wa_pad, gla_b_a.reshape(1, kw), gla_norm_g.reshape(1, gdv),
                        offs=offs_a, ga_col=0, dk=gdk, dv=gdv, tb=1024)
    hgrn_out = _hgrn_call(h_b, h_hf, lb.reshape(1, fw), hgrn_norm_g.reshape(1, hdv),
                          offs=offs_b, heads=hheads, dk=HGRN_EXPAND, dv=hdv, tb=1024)
    k_mem, v_mem = _mem_kv(mem_bf, mem_w_k, mem_w_v, B)
    mem_out = _memattn_call(h_b, k_mem, v_mem, q_col=offs_b["mq"], g_col=offs_b["mg"])

    acts = [a.reshape(B * T, -1) for a in (gla_out, hgrn_out, mem_out)]
    y, y_bf = _outproj_ln(acts, w_out.astype(BF16), x2d, ln_g.reshape(1, D), ln_b.reshape(1, D), alpha, emit_bf16)
    return y.reshape(B, T, D), y_bf


def _odd_layer(x, x_bf, emit_bf16, mem_bf, w_in, conv_w, conv_b, w_q, w_k, w_v, w_if, b_if, mh_norm_g, skip,
               mem_w_k, mem_w_v, w_out, ln_g, ln_b, alpha):
    B, T, D = x.shape
    width = conv_w.shape[1]
    heads = b_if.shape[0] // 2
    mw = mem_w_k.shape[1]
    assert w_in.shape[1] == 2 * width + 2 * mw

    x2d = x.reshape(B * T, D)
    if x_bf is None:
        x_bf = x2d.astype(BF16)
    h_bf = _matmul(x_bf, w_in.astype(BF16), BF16).reshape(B, T, -1)

    xc, q, k, v, gates = _mlstm_pre_call(h_bf, conv_w, conv_b, w_q, w_k, w_v, w_if, b_if, width=width)
    gates_rows = jnp.transpose(gates[:, :, :2 * heads].reshape(B, T, 2, heads), (0, 3, 2, 1))
    mlstm_out = _mlstm_call(q, k, v, gates, gates_rows, xc, h_bf,
                            mh_norm_g.reshape(1, width), skip.reshape(1, width), heads=heads, z_col=width)
    k_mem, v_mem = _mem_kv(mem_bf, mem_w_k, mem_w_v, B)
    mem_out = _memattn_call(h_bf, k_mem, v_mem, q_col=2 * width, g_col=2 * width + mw)

    acts = [a.reshape(B * T, -1) for a in (mlstm_out, mem_out)]
    y, y_bf = _outproj_ln(acts, w_out.astype(BF16), x2d, ln_g.reshape(1, D), ln_b.reshape(1, D), alpha, emit_bf16)
    return y.reshape(B, T, D), y_bf


def kernel(x, mem, hgrn_lb_logits, ev_w_in, ev_gla_w_a2, ev_gla_b_a, ev_gla_norm_g, ev_hgrn_norm_g, ev_mem_w_k, ev_mem_w_v, ev_w_out, ev_ln_g, ev_ln_b, od_w_in, od_conv_w, od_conv_b, od_w_q, od_w_k, od_w_v, od_w_if, od_b_if, od_mh_norm_g, od_skip, od_mem_w_k, od_mem_w_v, od_w_out, od_ln_g, od_ln_b):
    depth = ev_w_in.shape[0] + od_w_in.shape[0]
    alpha = (2 * depth) ** 0.25
    B, ml, D = mem.shape
    mem_bf = mem.reshape(B * ml, D).astype(BF16)
    lb_all = jnp.cumsum(jax.nn.softmax(hgrn_lb_logits.astype(F32), axis=0), axis=0)
    x_bf = None
    for layer in range(depth):
        i = layer // 2
        emit_bf16 = layer + 1 < depth
        if layer % 2 == 0:
            x, x_bf = _even_layer(x, x_bf, emit_bf16, mem_bf, lb_all[layer], ev_w_in, i, ev_gla_w_a2[i],
                                  ev_gla_b_a[i], ev_gla_norm_g[i], ev_hgrn_norm_g[i], ev_mem_w_k[i],
                                  ev_mem_w_v[i], ev_w_out[i], ev_ln_g[i], ev_ln_b[i], alpha)
        else:
            x, x_bf = _odd_layer(x, x_bf, emit_bf16, mem_bf, od_w_in[i], od_conv_w[i], od_conv_b[i], od_w_q[i],
                                 od_w_k[i], od_w_v[i], od_w_if[i], od_b_if[i], od_mh_norm_g[i], od_skip[i],
                                 od_mem_w_k[i], od_mem_w_v[i], od_w_out[i], od_ln_g[i], od_ln_b[i], alpha)
    return x
```
